```python
import math
import jax, jax.numpy as jnp
from jax import lax
import numpy as np

D_MODEL = 2048
BATCH = 8
SEQ = 4096
DEPTH = 4

MEM_LEN = 256
D_BRANCH = 1024
N_BRANCH = 4
S5_GROUP = 16
S5_GROUPS = D_BRANCH // S5_GROUP
S5_STATE = 64
ML_HEADS = 4
ML_HEAD_DIM = D_BRANCH // ML_HEADS
ML_CHUNK = 128
ML_CONV = 4
DA_HEADS = 8
DA_HEAD_DIM = 64
DA_V_DIM = 2 * DA_HEAD_DIM
DA_BLOCK = 128
XA_HEADS = 4
XA_HEAD_DIM = D_BRANCH // XA_HEADS
ROPE_THETA = 10000.0
EPS = 1e-6

IN_SPLITS = (D_BRANCH, D_BRANCH,
             D_BRANCH, D_BRANCH, D_BRANCH, D_BRANCH, D_BRANCH,
             ML_HEADS, ML_HEADS,
             D_BRANCH, D_BRANCH, D_BRANCH, D_BRANCH,
             D_BRANCH, D_BRANCH,
             N_BRANCH * D_MODEL)
D_IN = sum(IN_SPLITS)

kernel_name = 'hybrid_s5_mlstm_diffattn_memxattn'


def split_cols(y, sizes):
    idx = np.cumsum(sizes)[:-1].tolist()
    return jnp.split(y, idx, axis=-1)


def rms_norm(x, g):
    xf = x.astype(jnp.float32)
    y = xf * lax.rsqrt(jnp.mean(xf * xf, axis=-1, keepdims=True) + EPS)
    return (y * g.astype(jnp.float32)).astype(x.dtype)


def rope_tables(seq, dim):
    inv = 1.0 / (ROPE_THETA ** (jnp.arange(0, dim, 2, dtype=jnp.float32) / dim))
    ang = jnp.arange(seq, dtype=jnp.float32)[:, None] * inv[None, :]
    return jnp.cos(ang), jnp.sin(ang)


def apply_rope(x, cos, sin):
    x1, x2 = jnp.split(x.astype(jnp.float32), 2, axis=-1)
    shape = (1, cos.shape[0]) + (1,) * (x.ndim - 3) + (cos.shape[1],)
    c = cos.reshape(shape)
    s = sin.reshape(shape)
    return jnp.concatenate([x1 * c - x2 * s, x2 * c + x1 * s], axis=-1).astype(x.dtype)


def s5_mixer(u, lam_re, lam_im, log_dt, b_re, b_im, c_re, c_im, d_skip, w_glu, b_glu):
    bsz, seq, _ = u.shape
    f32 = jnp.float32
    uf = u.astype(f32)
    lam = lax.complex(jnp.minimum(lam_re.astype(f32), -1e-4), lam_im.astype(f32))
    dt = jnp.exp(log_dt.astype(f32))[:, None]
    lam_bar = jnp.exp(lam * dt)
    b = lax.complex(b_re.astype(f32), b_im.astype(f32))
    b_bar = ((lam_bar - 1.0) / lam)[..., None] * b
    ug = uf.reshape(bsz, seq, S5_GROUPS, S5_GROUP).astype(jnp.complex64)
    bu = jnp.einsum('gpc,blgc->blgp', b_bar, ug)
    a = jnp.broadcast_to(lam_bar, (1, seq) + lam_bar.shape)

    def combine(e1, e2):
        a1, x1 = e1
        a2, x2 = e2
        return a1 * a2, a2 * x1 + x2

    _, states = lax.associative_scan(combine, (a, bu), axis=1)
    c = lax.complex(c_re.astype(f32), c_im.astype(f32))
    y = jnp.real(jnp.einsum('gcp,blgp->blgc', c, states)).reshape(bsz, seq, D_BRANCH)
    y = y + d_skip.astype(f32) * uf
    y = jax.nn.gelu(y).astype(u.dtype)
    return y * jax.nn.sigmoid(y @ w_glu + b_glu)


def mlstm_mixer(q, k, v, o_pre, i_pre, f_pre, conv_w, conv_b, b_i, b_f, norm_g):
    bsz, seq, _ = q.shape
    f32 = jnp.float32
    qk = jnp.concatenate([q, k], axis=-1)
    qk = lax.conv_general_dilated(qk, conv_w[:, None, :], window_strides=(1,),
                                  padding=((ML_CONV - 1, 0),),
                                  dimension_numbers=('NWC', 'WIO', 'NWC'),
                                  feature_group_count=2 * D_BRANCH) + conv_b
    q, k = jnp.split(jax.nn.silu(qk), 2, axis=-1)
    nc = seq // ML_CHUNK

    def heads(t):
        return t.astype(f32).reshape(bsz, nc, ML_CHUNK, ML_HEADS, ML_HEAD_DIM).transpose(1, 0, 3, 2, 4)

    def gates(t):
        return t.reshape(bsz, nc, ML_CHUNK, ML_HEADS).transpose(1, 0, 3, 2)

    qh = heads(q)
    kh = heads(k) * (ML_HEAD_DIM ** -0.5)
    vh = heads(v)
    ig = gates(i_pre.astype(f32) + b_i.astype(f32))
    lf = gates(jax.nn.log_sigmoid(f_pre.astype(f32) + b_f.astype(f32)))
    causal = jnp.tril(jnp.ones((ML_CHUNK, ML_CHUNK), dtype=bool))

    def step(carry, inp):
        cmat, nvec, m = carry
        qc, kc, vc, ic, fc = inp
        acum = jnp.cumsum(fc, axis=-1)
        gtot = acum[..., -1]
        dmat = jnp.where(causal, acum[..., :, None] - acum[..., None, :] + ic[..., None, :], -jnp.inf)
        inter = acum + m[..., None]
        m_row = jnp.maximum(jnp.max(dmat, axis=-1), inter)
        s = jnp.einsum('bhqd,bhkd->bhqk', qc, kc) * jnp.exp(dmat - m_row[..., None])
        sc = jnp.exp(inter - m_row)
        num = jnp.einsum('bhqk,bhkd->bhqd', s, vc) + sc[..., None] * jnp.einsum('bhqd,bhde->bhqe', qc, cmat)
        den = jnp.sum(s, axis=-1) + sc * jnp.einsum('bhqd,bhd->bhq', qc, nvec)
        h = num / jnp.maximum(jnp.abs(den), jnp.exp(-m_row))[..., None]
        kw_log = gtot[..., None] - acum + ic
        m_new = jnp.maximum(gtot + m, jnp.max(kw_log, axis=-1))
        kw = jnp.exp(kw_log - m_new[..., None])
        decay = jnp.exp(gtot + m - m_new)
        cmat = decay[..., None, None] * cmat + jnp.einsum('bhkd,bhke->bhde', kc * kw[..., None], vc)
        nvec = decay[..., None] * nvec + jnp.einsum('bhk,bhkd->bhd', kw, kc)
        return (cmat, nvec, m_new), h

    init = (jnp.zeros((bsz, ML_HEADS, ML_HEAD_DIM, ML_HEAD_DIM), f32),
            jnp.zeros((bsz, ML_HEADS, ML_HEAD_DIM), f32),
            jnp.zeros((bsz, ML_HEADS), f32))
    _, hs = lax.scan(step, init, (qh, kh, vh, ig, lf))
    hs = hs.transpose(1, 0, 3, 2, 4).reshape(bsz, seq, ML_HEADS, ML_HEAD_DIM)
    hs = rms_norm(hs, norm_g.reshape(ML_HEADS, ML_HEAD_DIM)).reshape(bsz, seq, D_BRANCH)
    return (jax.nn.sigmoid(o_pre.astype(f32)) * hs).astype(q.dtype)


def diff_attention(q, k, v, cos, sin, lq1, lk1, lq2, lk2, subln_g, lambda_init):
    bsz, seq, _ = q.shape
    f32 = jnp.float32
    q = apply_rope(q.reshape(bsz, seq, DA_HEADS, 2, DA_HEAD_DIM), cos, sin)
    k = apply_rope(k.reshape(bsz, seq, DA_HEADS, 2, DA_HEAD_DIM), cos, sin)
    v = v.reshape(bsz, seq, DA_HEADS, DA_V_DIM)
    lam = (jnp.exp(jnp.sum(lq1.astype(f32) * lk1.astype(f32)))
           - jnp.exp(jnp.sum(lq2.astype(f32) * lk2.astype(f32))) + lambda_init)
    scale = DA_HEAD_DIM ** -0.5
    kpos = jnp.arange(seq)

    def block(j):
        start = j * DA_BLOCK
        qb = lax.dynamic_slice_in_dim(q, start, DA_BLOCK, axis=1)
        s = jnp.einsum('bqhcd,bkhcd->bhcqk', qb, k).astype(f32) * scale
        qpos = start + jnp.arange(DA_BLOCK)
        s = jnp.where(kpos[None, :] <= qpos[:, None], s, -jnp.inf)
        p = jax.nn.softmax(s, axis=-1)
        attn = (p[:, :, 0] - lam * p[:, :, 1]).astype(v.dtype)
        return jnp.einsum('bhqk,bkhd->bqhd', attn, v)

    o = lax.map(block, jnp.arange(seq // DA_BLOCK))
    o = o.transpose(1, 0, 2, 3, 4).reshape(bsz, seq, DA_HEADS, DA_V_DIM)
    o = rms_norm(o, subln_g) * (1.0 - lambda_init)
    return o.reshape(bsz, seq, D_BRANCH)


def memory_attention(q, mem_n, w_kv):
    bsz, seq, _ = q.shape
    kmem, vmem = jnp.split(mem_n @ w_kv, 2, axis=-1)
    q = q.reshape(bsz, seq, XA_HEADS, XA_HEAD_DIM)
    kmem = kmem.reshape(bsz, -1, XA_HEADS, XA_HEAD_DIM)
    vmem = vmem.reshape(bsz, -1, XA_HEADS, XA_HEAD_DIM)
    s = jnp.einsum('blhd,bmhd->bhlm', q, kmem).astype(jnp.float32) * (XA_HEAD_DIM ** -0.5)
    p = jax.nn.softmax(s, axis=-1).astype(vmem.dtype)
    return jnp.einsum('bhlm,bmhd->blhd', p, vmem).reshape(bsz, seq, D_BRANCH)


def setup_inputs(seed: int = 0) -> dict:
    key = jax.random.key(seed)
    k = jax.random.split(key, 32)
    f32 = jnp.float32

    def nrm(i, shape, scale):
        return scale * jax.random.normal(k[i], shape, f32)

    def gain(i, shape):
        return 1.0 + nrm(i, shape, 0.02)

    G, P, C = S5_GROUPS, S5_STATE, S5_GROUP
    lam_im = jnp.broadcast_to(jnp.pi * jnp.arange(P, dtype=f32), (DEPTH, G, P)) + nrm(5, (DEPTH, G, P), 0.01)
    return {
        'x': nrm(0, (BATCH, SEQ, D_MODEL), 1.0),
        'mem': nrm(1, (BATCH, MEM_LEN, D_MODEL), 1.0),
        'g_pre': gain(2, (DEPTH, D_MODEL)),
        'w_in': nrm(3, (DEPTH, D_MODEL, D_IN), D_MODEL ** -0.5),
        's5_lam_re': -0.5 + nrm(4, (DEPTH, G, P), 0.01),
        's5_lam_im': lam_im,
        's5_log_dt': jax.random.uniform(k[6], (DEPTH, G), f32, minval=math.log(1e-3), maxval=math.log(1e-1)),
        's5_b_re': nrm(7, (DEPTH, G, P, C), (2 * C) ** -0.5),
        's5_b_im': nrm(8, (DEPTH, G, P, C), (2 * C) ** -0.5),
        's5_c_re': nrm(9, (DEPTH, G, C, P), (2 * P) ** -0.5),
        's5_c_im': nrm(10, (DEPTH, G, C, P), (2 * P) ** -0.5),
        's5_d': nrm(11, (DEPTH, D_BRANCH), 1.0),
        's5_w_glu': nrm(12, (DEPTH, D_BRANCH, D_BRANCH), D_BRANCH ** -0.5),
        's5_b_glu': nrm(13, (DEPTH, D_BRANCH), 0.01),
        'ml_conv_w': nrm(14, (DEPTH, ML_CONV, 2 * D_BRANCH), ML_CONV ** -0.5),
        'ml_conv_b': nrm(15, (DEPTH, 2 * D_BRANCH), 0.01),
        'ml_b_i': nrm(16, (DEPTH, ML_HEADS), 0.1),
        'ml_b_f': jnp.linspace(3.0, 6.0, ML_HEADS, dtype=f32)[None, :] + nrm(17, (DEPTH, ML_HEADS), 0.01),
        'ml_norm_g': gain(18, (DEPTH, D_BRANCH)),
        'da_lq1': nrm(19, (DEPTH, DA_HEAD_DIM), 0.1),
        'da_lk1': nrm(20, (DEPTH, DA_HEAD_DIM), 0.1),
        'da_lq2': nrm(21, (DEPTH, DA_HEAD_DIM), 0.1),
        'da_lk2': nrm(22, (DEPTH, DA_HEAD_DIM), 0.1),
        'da_subln_g': gain(23, (DEPTH, DA_V_DIM)),
        'g_mem': gain(24, (DEPTH, D_MODEL)),
        'xa_w_kv': nrm(25, (DEPTH, D_MODEL, 2 * D_BRANCH), D_MODEL ** -0.5),
        'w_branch': nrm(26, (DEPTH, N_BRANCH, D_BRANCH, D_MODEL), D_BRANCH ** -0.5),
        'w_out': nrm(27, (DEPTH, D_MODEL, D_MODEL), D_MODEL ** -0.5),
        'g_post': gain(28, (DEPTH, D_MODEL)),
    }


def reference(x, mem, g_pre, w_in, s5_lam_re, s5_lam_im, s5_log_dt, s5_b_re, s5_b_im, s5_c_re, s5_c_im,
              s5_d, s5_w_glu, s5_b_glu, ml_conv_w, ml_conv_b, ml_b_i, ml_b_f, ml_norm_g,
              da_lq1, da_lk1, da_lq2, da_lk2, da_subln_g, g_mem, xa_w_kv, w_branch, w_out, g_post):
    bsz, seq, _ = x.shape
    cos, sin = rope_tables(seq, DA_HEAD_DIM)
    for l in range(DEPTH):
        lambda_init = 0.8 - 0.6 * math.exp(-0.3 * l)
        h = rms_norm(x, g_pre[l])
        (s5_u, s5_z, ml_q, ml_k, ml_v, ml_o, ml_z, ml_i, ml_f,
         da_q, da_k, da_v, da_z, xa_q, xa_z, gate_pre) = split_cols(h @ w_in[l], IN_SPLITS)
        y_s5 = s5_mixer(s5_u, s5_lam_re[l], s5_lam_im[l], s5_log_dt[l], s5_b_re[l], s5_b_im[l],
                        s5_c_re[l], s5_c_im[l], s5_d[l], s5_w_glu[l], s5_b_glu[l]) * jax.nn.silu(s5_z)
        y_ml = mlstm_mixer(ml_q, ml_k, ml_v, ml_o, ml_i, ml_f, ml_conv_w[l], ml_conv_b[l],
                           ml_b_i[l], ml_b_f[l], ml_norm_g[l]) * jax.nn.silu(ml_z)
        y_da = diff_attention(da_q, da_k, da_v, cos, sin, da_lq1[l], da_lk1[l], da_lq2[l], da_lk2[l],
                              da_subln_g[l], lambda_init) * jax.nn.silu(da_z)
        y_xa = memory_attention(xa_q, rms_norm(mem, g_mem[l]), xa_w_kv[l]) * jax.nn.silu(xa_z)
        gates = jax.nn.sigmoid(gate_pre.reshape(bsz, seq, N_BRANCH, D_MODEL))
        branches = (y_s5, y_ml, y_da, y_xa)
        merged = gates[:, :, 0] * (branches[0] @ w_branch[l, 0])
        for b in range(1, N_BRANCH):
            merged = merged + gates[:, :, b] * (branches[b] @ w_branch[l, b])
        x = x + rms_norm(merged @ w_out[l], g_post[l])
    return x
```

```python
import functools
import math

import jax
import jax.numpy as jnp
from jax import lax
from jax.experimental import pallas as pl
from jax.experimental.pallas import tpu as pltpu

F32 = jnp.float32
BF16 = jnp.bfloat16

D_MODEL = 2048
D_BRANCH = 1024
N_BRANCH = 4
S5_GROUP = 16
S5_GROUPS = D_BRANCH // S5_GROUP
S5_STATE = 64
ML_HEADS = 4
ML_HEAD_DIM = D_BRANCH // ML_HEADS
ML_CHUNK = 128
ML_CONV = 4
DA_HEADS = 8
DA_HEAD_DIM = 64
DA_V_DIM = 2 * DA_HEAD_DIM
XA_HEADS = 4
XA_HEAD_DIM = D_BRANCH // XA_HEADS
ROPE_THETA = 10000.0
EPS = 1e-6

LANES = 128
S5_T = 16
S5_LANE_GROUPS = LANES // S5_GROUP
S5_NBLK = D_BRANCH // LANES
S5_SBLK = S5_LANE_GROUPS * S5_STATE
S5_SLAB = S5_SBLK // LANES
N_MAIN = 13 * D_BRANCH + N_BRANCH * D_MODEL
GATE_COL0 = 13 * D_BRANCH
VMEM_LIMIT = 48 * 1024 * 1024


def _params(sem):
    return pltpu.CompilerParams(dimension_semantics=sem, vmem_limit_bytes=VMEM_LIMIT)


def _silu(x):
    return x * jax.nn.sigmoid(x)


def _dot(a, b):
    return jnp.dot(a, b, preferred_element_type=F32)


def _dot_nt(a, b):
    return lax.dot_general(a, b, (((1,), (1,)), ((), ())), preferred_element_type=F32)


def _norm_proj_kernel(x_ref, g_ref, w_ref, *rest, with_aux):
    if with_aux:
        wa_ref, y_ref, a_ref, h_scr = rest
    else:
        y_ref, h_scr = rest

    @pl.when(pl.program_id(1) == 0)
    def _():
        x = x_ref[...]
        ms = jnp.mean(x * x, axis=-1, keepdims=True)
        h = (x * lax.rsqrt(ms + EPS) * g_ref[...]).astype(BF16)
        h_scr[...] = h
        if with_aux:
            a_ref[...] = _dot(h, wa_ref[...])

    y_ref[...] = _dot(h_scr[...], w_ref[...]).astype(y_ref.dtype)


def _norm_proj(x, g, w, w_aux, tm, tn):
    m, k = x.shape
    n = w.shape[1]
    tm = min(tm, m)
    with_aux = w_aux is not None
    in_specs = [pl.BlockSpec((tm, k), lambda i, j: (i, 0)),
                pl.BlockSpec((1, k), lambda i, j: (0, 0)),
                pl.BlockSpec((k, tn), lambda i, j: (0, j))]
    out_shape = [jax.ShapeDtypeStruct((m, n), BF16)]
    out_specs = [pl.BlockSpec((tm, tn), lambda i, j: (i, j))]
    args = [x, g.reshape(1, k), w]
    if with_aux:
        in_specs.append(pl.BlockSpec((k, LANES), lambda i, j: (0, 0)))
        out_shape.append(jax.ShapeDtypeStruct((m, LANES), F32))
        out_specs.append(pl.BlockSpec((tm, LANES), lambda i, j: (i, 0)))
        args.append(w_aux)
    res = pl.pallas_call(
        functools.partial(_norm_proj_kernel, with_aux=with_aux),
        grid=(m // tm, n // tn),
        in_specs=in_specs, out_specs=out_specs, out_shape=out_shape,
        scratch_shapes=[pltpu.VMEM((tm, k), BF16)],
        compiler_params=_params(("parallel", "arbitrary")),
        name="norm_proj_aux" if with_aux else "norm_proj",
    )(*args)
    return res if with_aux else res[0]


def _s5_operators(lam_re, lam_im, log_dt, b_re, b_im, c_re, c_im):
    hi = lax.Precision.HIGHEST
    G, P, C, T, J, LG = S5_GROUPS, S5_STATE, S5_GROUP, S5_T, S5_NBLK, S5_LANE_GROUPS
    lam = lax.complex(jnp.minimum(lam_re.astype(F32), -1e-4), lam_im.astype(F32))
    dt = jnp.exp(log_dt.astype(F32))[:, None]
    z = lam * dt
    lam_bar = jnp.exp(z)
    b_bar = ((lam_bar - 1.0) / lam)[..., None] * lax.complex(b_re.astype(F32), b_im.astype(F32))
    c = lax.complex(c_re.astype(F32), c_im.astype(F32))
    steps = jnp.arange(T + 1, dtype=F32)
    pw = jnp.exp(z[None] * steps[:, None, None].astype(jnp.complex64))
    eye = jnp.eye(LG, dtype=F32)

    kern = jnp.real(jnp.einsum('gop,kgp,gpi->kgoi', c, pw[:T], b_bar, precision=hi))
    s_idx = jnp.arange(T)[:, None]
    t_idx = jnp.arange(T)[None, :]
    lag = jnp.clip(t_idx - s_idx, 0, T - 1)
    toe = kern[lag] * (t_idx >= s_idx)[:, :, None, None, None].astype(F32)
    toe = toe.reshape(T, T, J, LG, C, C).transpose(2, 0, 3, 5, 1, 4)
    m_intra = toe[:, :, :, :, :, None, :] * eye[None, None, :, None, None, :, None]
    m_intra = m_intra.reshape(J, T * LANES, T * LANES).astype(BF16)

    bs = pw[T - 1 - jnp.arange(T)][..., None] * b_bar[None]
    bs = jnp.stack([jnp.real(bs), jnp.imag(bs)], axis=0)
    bs = bs.reshape(2, T, J, LG, P, C).transpose(2, 1, 3, 5, 0, 4)
    m_in = bs[:, :, :, :, :, None, :] * eye[None, None, :, None, None, :, None]
    m_in = m_in.reshape(J, T * LANES, 2 * S5_SBLK).astype(BF16)

    cs = c[None] * pw[1:T + 1][:, :, None, :]
    cs = jnp.stack([jnp.real(cs), -jnp.imag(cs)], axis=0)
    cs = cs.reshape(2, T, J, LG, C, P).transpose(2, 0, 3, 5, 1, 4)
    m_out = cs[:, :, :, :, :, None, :] * eye[None, None, :, None, None, :, None]
    m_out = m_out.reshape(J, 2 * S5_SBLK, T * LANES).astype(BF16)

    lam_t = pw[T].reshape(1, G * P)
    return m_intra, m_in, m_out, jnp.real(lam_t), jnp.imag(lam_t)


def _s5_local_kernel(*refs):
    u_refs, (min_ref, s_ref) = refs[:S5_T], refs[S5_T:]
    ucat = jnp.concatenate([r[...] for r in u_refs], axis=1)
    s = _dot(ucat, min_ref[0])
    for c in range(2 * S5_SLAB):
        s_ref[c] = s[:, c * LANES:(c + 1) * LANES]


def _s5_scan_kernel(sr_ref, si_ref, ar_ref, ai_ref, xr_ref, xi_ref, *, bsz, nchunk):
    ar = [jnp.broadcast_to(ar_ref[:, c * LANES:(c + 1) * LANES], (bsz, LANES)) for c in range(S5_SLAB)]
    ai = [jnp.broadcast_to(ai_ref[:, c * LANES:(c + 1) * LANES], (bsz, LANES)) for c in range(S5_SLAB)]

    def step(k, carry):
        rows = pl.ds(k, bsz, stride=nchunk)
        new = []
        for c in range(S5_SLAB):
            xr, xi = carry[c]
            xr_ref[c, rows, :] = xr
            xi_ref[c, rows, :] = xi
            nr = ar[c] * xr - ai[c] * xi + sr_ref[c, rows, :]
            ni = ar[c] * xi + ai[c] * xr + si_ref[c, rows, :]
            new.append((nr, ni))
        return tuple(new)

    zero = jnp.zeros((bsz, LANES), F32)
    lax.fori_loop(0, nchunk, step, tuple((zero, zero) for _ in range(S5_SLAB)))


def _s5_out_kernel(*refs):
    u_refs = refs[:S5_T]
    mi_ref, mo_ref, xr_ref, xi_ref, d_ref, out_ref = refs[S5_T:]
    us = [r[...] for r in u_refs]
    ucat = jnp.concatenate(us, axis=1)
    xcat = jnp.concatenate([xr_ref[c] for c in range(S5_SLAB)] + [xi_ref[c] for c in range(S5_SLAB)],
                           axis=1).astype(BF16)
    y = _dot(ucat, mi_ref[0]) + _dot(xcat, mo_ref[0])
    d = d_ref[...]
    for t in range(S5_T):
        yt = y[:, t * LANES:(t + 1) * LANES] + d * us[t].astype(F32)
        out_ref[t] = jax.nn.gelu(yt).astype(BF16)


def _s5_glu_kernel(g_ref, z_ref, w_ref, b_ref, out_ref):
    g = g_ref[...]
    a = _dot(g, w_ref[...]) + b_ref[...]
    out_ref[...] = (g.astype(F32) * jax.nn.sigmoid(a) * _silu(z_ref[...].astype(F32))).astype(BF16)


def _s5_branch(y_main, bsz, seq, ops, d_skip, w_glu, b_glu, rt):
    m_intra, m_in, m_out, lam_r, lam_i = ops
    nrow = bsz * seq // S5_T
    nchunk = seq // S5_T
    rt = min(rt, nrow)
    ncolblk = N_MAIN // LANES
    y_rows = y_main.reshape(nrow, S5_T * N_MAIN)
    u_specs = [pl.BlockSpec((rt, LANES), lambda j, r, s=s: (r, s * ncolblk + j)) for s in range(S5_T)]

    s_loc = pl.pallas_call(
        _s5_local_kernel,
        grid=(S5_NBLK, nrow // rt),
        in_specs=u_specs + [pl.BlockSpec((1, S5_T * LANES, 2 * S5_SBLK), lambda j, r: (j, 0, 0))],
        out_specs=pl.BlockSpec((2 * S5_SLAB, rt, LANES), lambda j, r: (j, r, 0)),
        out_shape=jax.ShapeDtypeStruct((S5_NBLK * 2 * S5_SLAB, nrow, LANES), F32),
        compiler_params=_params(("parallel", "parallel")),
        name="s5_local",
    )(*([y_rows] * S5_T), m_in)

    xr, xi = pl.pallas_call(
        functools.partial(_s5_scan_kernel, bsz=bsz, nchunk=nchunk),
        grid=(S5_NBLK,),
        in_specs=[pl.BlockSpec((S5_SLAB, nrow, LANES), lambda j: (2 * j, 0, 0)),
                  pl.BlockSpec((S5_SLAB, nrow, LANES), lambda j: (2 * j + 1, 0, 0)),
                  pl.BlockSpec((1, S5_SBLK), lambda j: (0, j)),
                  pl.BlockSpec((1, S5_SBLK), lambda j: (0, j))],
        out_specs=[pl.BlockSpec((S5_SLAB, nrow, LANES), lambda j: (j, 0, 0))] * 2,
        out_shape=[jax.ShapeDtypeStruct((S5_NBLK * S5_SLAB, nrow, LANES), F32)] * 2,
        compiler_params=_params(("parallel",)),
        name="s5_scan",
    )(s_loc, s_loc, lam_r, lam_i)

    gel = pl.pallas_call(
        _s5_out_kernel,
        grid=(S5_NBLK, nrow // rt),
        in_specs=u_specs + [
            pl.BlockSpec((1, S5_T * LANES, S5_T * LANES), lambda j, r: (j, 0, 0)),
            pl.BlockSpec((1, 2 * S5_SBLK, S5_T * LANES), lambda j, r: (j, 0, 0)),
            pl.BlockSpec((S5_SLAB, rt, LANES), lambda j, r: (j, r, 0)),
            pl.BlockSpec((S5_SLAB, rt, LANES), lambda j, r: (j, r, 0)),
            pl.BlockSpec((1, LANES), lambda j, r: (0, j))],
        out_specs=pl.BlockSpec((S5_T, rt, LANES), lambda j, r: (0, r, j)),
        out_shape=jax.ShapeDtypeStruct((S5_T, nrow, D_BRANCH), BF16),
        compiler_params=_params(("parallel", "parallel")),
        name="s5_out",
    )(*([y_rows] * S5_T), m_intra, m_out, xr, xi, d_skip.reshape(1, D_BRANCH))

    nmain_blk = N_MAIN // D_BRANCH
    y = pl.pallas_call(
        _s5_glu_kernel,
        grid=(S5_T, nrow // rt),
        in_specs=[pl.BlockSpec((pl.Squeezed(), rt, D_BRANCH), lambda t, r: (t, r, 0)),
                  pl.BlockSpec((rt, D_BRANCH), lambda t, r: (r, t * nmain_blk + 1)),
                  pl.BlockSpec((D_BRANCH, D_BRANCH), lambda t, r: (0, 0)),
                  pl.BlockSpec((1, D_BRANCH), lambda t, r: (0, 0))],
        out_specs=pl.BlockSpec((rt, D_BRANCH), lambda t, r: (r, t)),
        out_shape=jax.ShapeDtypeStruct((nrow, S5_T * D_BRANCH), BF16),
        compiler_params=_params(("parallel", "parallel")),
        name="s5_glu",
    )(gel, y_rows, w_glu.astype(BF16), b_glu.reshape(1, D_BRANCH).astype(F32))
    return y.reshape(bsz * seq, D_BRANCH)


def _split3(x):
    hi = x.astype(BF16)
    r1 = x - hi.astype(F32)
    mid = r1.astype(BF16)
    lo = (r1 - mid.astype(F32)).astype(BF16)
    return hi, mid, lo


def _mlstm_kernel(q_ref, k_ref, v_ref, o_ref, z_ref, if_ref, cw_ref, cb_ref, bif_ref, ng_ref, out_ref,
                  ext_scr, c_scr, n_scr, m_scr):
    lc, dh, nh = ML_CHUNK, ML_HEAD_DIM, ML_HEADS
    tail = 8
    cidx = pl.program_id(1)

    @pl.when(cidx == 0)
    def _():
        ext_scr[0:tail, :] = jnp.zeros((tail, 2 * D_BRANCH), F32)
        c_scr[...] = jnp.zeros_like(c_scr)
        n_scr[...] = jnp.zeros_like(n_scr)
        m_scr[...] = jnp.zeros_like(m_scr)

    @pl.when(cidx > 0)
    def _():
        ext_scr[0:tail, :] = ext_scr[lc:lc + tail, :]

    ext_scr[tail:tail + lc, 0:D_BRANCH] = q_ref[...].astype(F32)
    ext_scr[tail:tail + lc, D_BRANCH:2 * D_BRANCH] = k_ref[...].astype(F32)
    conv = jnp.broadcast_to(cb_ref[...], (lc, 2 * D_BRANCH))
    for w in range(ML_CONV):
        off = tail - (ML_CONV - 1) + w
        conv = conv + cw_ref[w:w + 1, :] * ext_scr[off:off + lc, :]
    qk = _silu(conv)

    gts = if_ref[...] + bif_ref[...]
    lf = jnp.minimum(gts, 0.0) - jnp.log(1.0 + jnp.exp(-jnp.abs(gts)))
    gts_t = jnp.transpose(gts)[0:16, :]
    lf_t = jnp.transpose(lf)[0:16, :]
    row = lax.broadcasted_iota(jnp.int32, (lc, lc), 0)
    col = lax.broadcasted_iota(jnp.int32, (lc, lc), 1)
    causal = col <= row
    tril = jnp.where(causal, 1.0, 0.0).astype(BF16)
    triu = jnp.where(row <= col, 1.0, 0.0).astype(BF16)
    acum_c = sum(_dot(tril, p) for p in _split3(lf))
    acum_r = sum(_dot(p, triu) for p in _split3(lf_t))

    for h in range(nh):
        sl = slice(h * dh, (h + 1) * dh)
        qh = qk[:, h * dh:(h + 1) * dh]
        kh = qk[:, D_BRANCH + h * dh:D_BRANCH + (h + 1) * dh] * (dh ** -0.5)
        vh = v_ref[:, sl]
        qb = qh.astype(BF16)
        a_c = acum_c[:, nh + h:nh + h + 1]
        a_r = acum_r[nh + h:nh + h + 1, :]
        i_c = gts[:, h:h + 1]
        i_r = gts_t[h:h + 1, :]
        gtot = a_c[lc - 1:lc, :]
        m_old = m_scr[h][:, 0:1]
        dmat = jnp.where(causal, a_c - a_r + i_r, -jnp.inf)
        inter = a_c + m_old
        m_row = jnp.maximum(jnp.max(dmat, axis=-1, keepdims=True), inter)
        s = _dot_nt(qb, kh.astype(BF16)) * jnp.exp(dmat - m_row)
        sc = jnp.exp(inter - m_row)
        num = _dot(s.astype(BF16), vh) + sc * _dot(qb, c_scr[h].astype(BF16))
        den = jnp.sum(s, axis=-1, keepdims=True) + sc * jnp.sum(qh * n_scr[h], axis=-1, keepdims=True)
        hout = num / jnp.maximum(jnp.abs(den), jnp.exp(-m_row))

        kw_log = gtot - a_c + i_c
        m_new = jnp.maximum(gtot + m_old, jnp.max(kw_log, axis=0, keepdims=True))
        kw = jnp.exp(kw_log - m_new)
        decay = jnp.exp(gtot + m_old - m_new)
        kk = kh * kw
        c_scr[h] = decay * c_scr[h] + _dot(jnp.transpose(kk).astype(BF16), vh)
        n_scr[h] = decay * n_scr[h] + jnp.sum(kk, axis=0, keepdims=True)
        m_scr[h] = jnp.broadcast_to(m_new, (1, LANES))

        hn = hout * lax.rsqrt(jnp.mean(hout * hout, axis=-1, keepdims=True) + EPS) * ng_ref[:, sl]
        y = jax.nn.sigmoid(o_ref[:, sl].astype(F32)) * hn * _silu(z_ref[:, sl].astype(F32))
        out_ref[:, sl] = y.astype(BF16)


def _mlstm_branch(y_main, gates, bsz, seq, conv_w, conv_b, b_i, b_f, norm_g):
    lc = ML_CHUNK
    nc = seq // lc
    bif = jnp.zeros((1, LANES), F32).at[0, 0:ML_HEADS].set(b_i.astype(F32))
    bif = bif.at[0, ML_HEADS:2 * ML_HEADS].set(b_f.astype(F32))

    def col(cb):
        return pl.BlockSpec((lc, D_BRANCH), lambda b, c: (b * nc + c, cb))

    def const(shape):
        return pl.BlockSpec(shape, lambda b, c: (0,) * len(shape))

    return pl.pallas_call(
        _mlstm_kernel,
        grid=(bsz, nc),
        in_specs=[col(2), col(3), col(4), col(5), col(6),
                  pl.BlockSpec((lc, LANES), lambda b, c: (b * nc + c, 0)),
                  const((ML_CONV, 2 * D_BRANCH)), const((1, 2 * D_BRANCH)), const((1, LANES)),
                  const((1, D_BRANCH))],
        out_specs=pl.BlockSpec((lc, D_BRANCH), lambda b, c: (b * nc + c, 0)),
        out_shape=jax.ShapeDtypeStruct((bsz * seq, D_BRANCH), BF16),
        scratch_shapes=[pltpu.VMEM((lc + 8, 2 * D_BRANCH), F32),
                        pltpu.VMEM((ML_HEADS, ML_HEAD_DIM, ML_HEAD_DIM), F32),
                        pltpu.VMEM((ML_HEADS, 1, ML_HEAD_DIM), F32),
                        pltpu.VMEM((ML_HEADS, 1, LANES), F32)],
        compiler_params=_params(("parallel", "arbitrary")),
        name="mlstm",
    )(y_main, y_main, y_main, y_main, y_main, gates,
      conv_w.astype(F32), conv_b.reshape(1, 2 * D_BRANCH).astype(F32), bif,
      norm_g.reshape(1, D_BRANCH).astype(F32))


def _rope_tables(seq):
    half = DA_HEAD_DIM // 2
    inv = 1.0 / (ROPE_THETA ** (jnp.arange(0, DA_HEAD_DIM, 2, dtype=F32) / DA_HEAD_DIM))
    ang = jnp.arange(seq, dtype=F32)[:, None] * inv[None, :]
    cos, sin = jnp.cos(ang), jnp.sin(ang)
    cos_t = jnp.tile(jnp.concatenate([cos, cos], axis=1), (1, LANES // DA_HEAD_DIM))
    sin_t = jnp.tile(jnp.concatenate([-sin, sin], axis=1), (1, LANES // DA_HEAD_DIM))
    del half
    return cos_t, sin_t


def _rope_kernel(q_ref, k_ref, cos_ref, sin_ref, q1_ref, q2_ref, kr_ref):
    rep = D_BRANCH // LANES
    cos = jnp.concatenate([cos_ref[...]] * rep, axis=1)
    sin = jnp.concatenate([sin_ref[...]] * rep, axis=1)
    lane = lax.broadcasted_iota(jnp.int32, (1, D_BRANCH), 1)
    first_half = (lane % DA_HEAD_DIM) < (DA_HEAD_DIM // 2)
    comp0 = (lane % DA_V_DIM) < DA_HEAD_DIM
    half = DA_HEAD_DIM // 2

    def rot(x):
        partner = jnp.where(first_half, pltpu.roll(x, D_BRANCH - half, 1), pltpu.roll(x, half, 1))
        return x * cos + partner * sin

    q = rot(q_ref[...].astype(F32)) * (DA_HEAD_DIM ** -0.5)
    q1_ref[...] = jnp.where(comp0, q, 0.0).astype(BF16)
    q2_ref[...] = jnp.where(comp0, 0.0, q).astype(BF16)
    kr_ref[...] = rot(k_ref[...].astype(F32)).astype(BF16)


def _diff_attn_kernel(q1_ref, q2_ref, k_ref, v_ref, z_ref, lq1_ref, lk1_ref, lq2_ref, lk2_ref, g_ref,
                      out_ref, m_scr, l_scr, acc_scr, *, lambda_init, tq):
    qi = pl.program_id(2)
    ki = pl.program_id(3)

    @pl.when(ki == 0)
    def _():
        m_scr[...] = jnp.full_like(m_scr, -jnp.inf)
        l_scr[...] = jnp.zeros_like(l_scr)
        acc_scr[...] = jnp.zeros_like(acc_scr)

    def update(masked):
        k = k_ref[...]
        v = v_ref[...]
        if masked:
            row = lax.broadcasted_iota(jnp.int32, (tq, tq), 0)
            col = lax.broadcasted_iota(jnp.int32, (tq, tq), 1)
            keep = col <= row
        for c, q_ref in enumerate((q1_ref, q2_ref)):
            s = _dot_nt(q_ref[...], k)
            if masked:
                s = jnp.where(keep, s, -jnp.inf)
            m_prev = m_scr[c]
            m_new = jnp.maximum(m_prev, jnp.max(s, axis=-1, keepdims=True))
            alpha = jnp.exp(m_prev - m_new)
            p = jnp.exp(s - m_new[:, 0:1])
            l_scr[c] = alpha * l_scr[c] + jnp.sum(p, axis=-1, keepdims=True)
            acc_scr[c] = alpha * acc_scr[c] + _dot(p.astype(BF16), v)
            m_scr[c] = m_new

    @pl.when(ki < qi)
    def _():
        update(False)

    @pl.when(ki == qi)
    def _():
        update(True)
        lam = (jnp.exp(jnp.sum(lq1_ref[...] * lk1_ref[...], axis=-1, keepdims=True))
               - jnp.exp(jnp.sum(lq2_ref[...] * lk2_ref[...], axis=-1, keepdims=True)) + lambda_init)
        o = acc_scr[0] / l_scr[0] - lam * (acc_scr[1] / l_scr[1])
        o = o * lax.rsqrt(jnp.mean(o * o, axis=-1, keepdims=True) + EPS) * g_ref[...]
        o = o * (1.0 - lambda_init) * _silu(z_ref[...].astype(F32))
        out_ref[...] = o.astype(BF16)


def _diff_attn_branch(y_main, bsz, seq, cos_t, sin_t, lq1, lk1, lq2, lk2, subln_g, lambda_init, tq, rope_rows):
    ntok = bsz * seq
    tr = min(rope_rows, seq)
    nrb = seq // tr
    q1, q2, kr = pl.pallas_call(
        _rope_kernel,
        grid=(bsz, nrb),
        in_specs=[pl.BlockSpec((tr, D_BRANCH), lambda b, r: (b * nrb + r, 7)),
                  pl.BlockSpec((tr, D_BRANCH), lambda b, r: (b * nrb + r, 8)),
                  pl.BlockSpec((tr, LANES), lambda b, r: (r, 0)),
                  pl.BlockSpec((tr, LANES), lambda b, r: (r, 0))],
        out_specs=[pl.BlockSpec((tr, D_BRANCH), lambda b, r: (b * nrb + r, 0))] * 3,
        out_shape=[jax.ShapeDtypeStruct((ntok, D_BRANCH), BF16)] * 3,
        compiler_params=_params(("parallel", "parallel")),
        name="rope",
    )(y_main, y_main, cos_t, sin_t)

    tq = min(tq, seq)
    nq = seq // tq
    vblk0 = 9 * D_BRANCH // DA_V_DIM
    zblk0 = 10 * D_BRANCH // DA_V_DIM

    def qspec():
        return pl.BlockSpec((tq, DA_V_DIM), lambda b, h, i, j: (b * nq + i, h))

    def kspec(blk0):
        return pl.BlockSpec((tq, DA_V_DIM), lambda b, h, i, j: (b * nq + jnp.minimum(i, j), blk0 + h))

    def vec(n):
        return pl.BlockSpec((1, n), lambda b, h, i, j: (0, 0))

    lvec = [a.reshape(1, DA_HEAD_DIM).astype(F32) for a in (lq1, lk1, lq2, lk2)]
    return pl.pallas_call(
        functools.partial(_diff_attn_kernel, lambda_init=lambda_init, tq=tq),
        grid=(bsz, DA_HEADS, nq, nq),
        in_specs=[qspec(), qspec(), kspec(0), kspec(vblk0),
                  pl.BlockSpec((tq, DA_V_DIM), lambda b, h, i, j: (b * nq + i, zblk0 + h)),
                  vec(DA_HEAD_DIM), vec(DA_HEAD_DIM), vec(DA_HEAD_DIM), vec(DA_HEAD_DIM), vec(DA_V_DIM)],
        out_specs=pl.BlockSpec((tq, DA_V_DIM), lambda b, h, i, j: (b * nq + i, h)),
        out_shape=jax.ShapeDtypeStruct((ntok, D_BRANCH), BF16),
        scratch_shapes=[pltpu.VMEM((2, tq, LANES), F32), pltpu.VMEM((2, tq, LANES), F32),
                        pltpu.VMEM((2, tq, DA_V_DIM), F32)],
        compiler_params=_params(("parallel", "parallel", "parallel", "arbitrary")),
        name="diff_attn",
    )(q1, q2, kr, y_main, y_main, *lvec, subln_g.reshape(1, DA_V_DIM).astype(F32))


def _mem_attn_kernel(q_ref, z_ref, k_ref, v_ref, out_ref):
    dh = XA_HEAD_DIM
    for h in range(XA_HEADS):
        sl = slice(h * dh, (h + 1) * dh)
        s = _dot_nt(q_ref[:, sl], k_ref[:, sl]) * (dh ** -0.5)
        p = jnp.exp(s - jnp.max(s, axis=-1, keepdims=True))
        p = p / jnp.sum(p, axis=-1, keepdims=True)
        o = _dot(p.astype(BF16), v_ref[:, sl])
        out_ref[:, sl] = (o * _silu(z_ref[:, sl].astype(F32))).astype(BF16)


def _mem_attn_branch(y_main, kv, bsz, seq, mem_len, tq):
    tq = min(tq, seq)
    nq = seq // tq
    return pl.pallas_call(
        _mem_attn_kernel,
        grid=(bsz, nq),
        in_specs=[pl.BlockSpec((tq, D_BRANCH), lambda b, i: (b * nq + i, 11)),
                  pl.BlockSpec((tq, D_BRANCH), lambda b, i: (b * nq + i, 12)),
                  pl.BlockSpec((mem_len, D_BRANCH), lambda b, i: (b, 0)),
                  pl.BlockSpec((mem_len, D_BRANCH), lambda b, i: (b, 1))],
        out_specs=pl.BlockSpec((tq, D_BRANCH), lambda b, i: (b * nq + i, 0)),
        out_shape=jax.ShapeDtypeStruct((bsz * seq, D_BRANCH), BF16),
        compiler_params=_params(("parallel", "parallel")),
        name="mem_attn",
    )(y_main, y_main, kv, kv)


def _merge_kernel(*refs):
    y_refs = refs[0:N_BRANCH]
    g_refs = refs[N_BRANCH:2 * N_BRANCH]
    w_refs = refs[2 * N_BRANCH:3 * N_BRANCH]
    out_ref = refs[3 * N_BRANCH]
    acc = None
    for y_ref, g_ref, w_ref in zip(y_refs, g_refs, w_refs):
        term = jax.nn.sigmoid(g_ref[...].astype(F32)) * _dot(y_ref[...], w_ref[0])
        acc = term if acc is None else acc + term
    out_ref[...] = acc.astype(BF16)


def _merge(branches, y_main, w_branch, tm, tn):
    ntok = y_main.shape[0]
    tm = min(tm, ntok)
    ncol = D_MODEL // tn
    y_specs = [pl.BlockSpec((tm, D_BRANCH), lambda i, n: (i, 0))] * N_BRANCH
    g_specs = [pl.BlockSpec((tm, tn), lambda i, n, b=b: (i, (GATE_COL0 + b * D_MODEL) // tn + n))
               for b in range(N_BRANCH)]
    w_specs = [pl.BlockSpec((1, D_BRANCH, tn), lambda i, n, b=b: (b, 0, n)) for b in range(N_BRANCH)]
    return pl.pallas_call(
        _merge_kernel,
        grid=(ntok // tm, ncol),
        in_specs=y_specs + g_specs + w_specs,
        out_specs=pl.BlockSpec((tm, tn), lambda i, n: (i, n)),
        out_shape=jax.ShapeDtypeStruct((ntok, D_MODEL), BF16),
        compiler_params=_params(("parallel", "arbitrary")),
        name="merge",
    )(*branches, *([y_main] * N_BRANCH), *([w_branch] * N_BRANCH))


def _out_proj_kernel(m_ref, w_ref, g_ref, x_ref, out_ref):
    o = _dot(m_ref[...], w_ref[...])
    o = o * lax.rsqrt(jnp.mean(o * o, axis=-1, keepdims=True) + EPS) * g_ref[...]
    out_ref[...] = x_ref[...] + o


def _out_proj(merged, w_out, g_post, x, tm):
    ntok = x.shape[0]
    tm = min(tm, ntok)
    return pl.pallas_call(
        _out_proj_kernel,
        grid=(ntok // tm,),
        in_specs=[pl.BlockSpec((tm, D_MODEL), lambda i: (i, 0)),
                  pl.BlockSpec((D_MODEL, D_MODEL), lambda i: (0, 0)),
                  pl.BlockSpec((1, D_MODEL), lambda i: (0, 0)),
                  pl.BlockSpec((tm, D_MODEL), lambda i: (i, 0))],
        out_specs=pl.BlockSpec((tm, D_MODEL), lambda i: (i, 0)),
        out_shape=jax.ShapeDtypeStruct((ntok, D_MODEL), F32),
        compiler_params=_params(("parallel",)),
        name="out_proj",
    )(merged, w_out, g_post.reshape(1, D_MODEL).astype(F32), x)


def kernel(x, mem, g_pre, w_in, s5_lam_re, s5_lam_im, s5_log_dt, s5_b_re, s5_b_im, s5_c_re, s5_c_im, s5_d, s5_w_glu, s5_b_glu, ml_conv_w, ml_conv_b, ml_b_i, ml_b_f, ml_norm_g, da_lq1, da_lk1, da_lq2, da_lk2, da_subln_g, g_mem, xa_w_kv, w_branch, w_out, g_post):
    bsz, seq, _ = x.shape
    mem_len = mem.shape[1]
    depth = w_in.shape[0]
    ntok = bsz * seq
    cos_t, sin_t = _rope_tables(seq)
    xf = x.reshape(ntok, D_MODEL).astype(F32)
    memf = mem.reshape(bsz * mem_len, D_MODEL).astype(F32)
    gate0 = 7 * D_BRANCH
    for l in range(depth):
        lambda_init = 0.8 - 0.6 * math.exp(-0.3 * l)
        w_l = w_in[l]
        w_main = jnp.concatenate([w_l[:, :gate0], w_l[:, gate0 + 2 * ML_HEADS:]], axis=1).astype(BF16)
        w_gate = jnp.pad(w_l[:, gate0:gate0 + 2 * ML_HEADS], ((0, 0), (0, LANES - 2 * ML_HEADS))).astype(BF16)
        y_main, gates = _norm_proj(xf, g_pre[l], w_main, w_gate, tm=1024, tn=1024)
        kv = _norm_proj(memf, g_mem[l], xa_w_kv[l].astype(BF16), None, tm=1024, tn=1024)

        ops = _s5_operators(s5_lam_re[l], s5_lam_im[l], s5_log_dt[l], s5_b_re[l], s5_b_im[l],
                            s5_c_re[l], s5_c_im[l])
        y_s5 = _s5_branch(y_main, bsz, seq, ops, s5_d[l].astype(F32), s5_w_glu[l], s5_b_glu[l], rt=512)
        y_ml = _mlstm_branch(y_main, gates, bsz, seq, ml_conv_w[l], ml_conv_b[l], ml_b_i[l], ml_b_f[l],
                             ml_norm_g[l])
        y_da = _diff_attn_branch(y_main, bsz, seq, cos_t, sin_t, da_lq1[l], da_lk1[l], da_lq2[l], da_lk2[l],
                                 da_subln_g[l], lambda_init, tq=512, rope_rows=512)
        y_xa = _mem_attn_branch(y_main, kv, bsz, seq, mem_len, tq=512)

        merged = _merge((y_s5, y_ml, y_da, y_xa), y_main, w_branch[l].astype(BF16), tm=512, tn=512)
        xf = _out_proj(merged, w_out[l].astype(BF16), g_post[l], xf, tm=256)
    return xf.reshape(bsz, seq, D_MODEL).astype(x.dtype)
```

```python
import functools
import math

import jax
import jax.numpy as jnp
from jax import lax
from jax.experimental import pallas as pl
from jax.experimental.pallas import tpu as pltpu

F32 = jnp.float32
BF16 = jnp.bfloat16

D_MODEL = 2048
D_BRANCH = 1024
N_BRANCH = 4
S5_GROUP = 16
S5_GROUPS = D_BRANCH // S5_GROUP
S5_STATE = 64
ML_HEADS = 4
ML_HEAD_DIM = D_BRANCH // ML_HEADS
ML_CHUNK = 128
ML_CONV = 4
DA_HEADS = 8
DA_HEAD_DIM = 64
DA_V_DIM = 2 * DA_HEAD_DIM
XA_HEADS = 4
XA_HEAD_DIM = D_BRANCH // XA_HEADS
ROPE_THETA = 10000.0
EPS = 1e-6

LANES = 128
S5_T = 16
S5_LANE_GROUPS = LANES // S5_GROUP
S5_NBLK = D_BRANCH // LANES
S5_SBLK = S5_LANE_GROUPS * S5_STATE
S5_SLAB = S5_SBLK // LANES
N_S5IN = 2 * D_BRANCH
N_MAIN = 11 * D_BRANCH + N_BRANCH * D_MODEL
ML_BLK0, DA_BLK0, XA_BLK0 = 0, 5, 9
GATE_COL0 = 11 * D_BRANCH
MIB = 1024 * 1024
VMEM_LIMIT = 48 * MIB


def _params(sem, vmem=VMEM_LIMIT):
    return pltpu.CompilerParams(dimension_semantics=sem, vmem_limit_bytes=vmem)


def _silu(x):
    return x * jax.nn.sigmoid(x)


def _dot(a, b):
    return jnp.dot(a, b, preferred_element_type=F32)


def _dot_nt(a, b):
    return lax.dot_general(a, b, (((1,), (1,)), ((), ())), preferred_element_type=F32)


def _rms_norm_bf16(x, g):
    ms = jnp.mean(x * x, axis=-1, keepdims=True)
    return (x * lax.rsqrt(ms + EPS) * g).astype(BF16)


def _norm_proj_kernel(x_ref, g_ref, w_ref, y_ref, h_scr):
    @pl.when(pl.program_id(1) == 0)
    def _():
        h_scr[...] = _rms_norm_bf16(x_ref[...], g_ref[...])

    y_ref[...] = _dot(h_scr[...], w_ref[...]).astype(y_ref.dtype)


def _norm_proj(x, g, w, tm, tn):
    m, k = x.shape
    n = w.shape[1]
    tm = min(tm, m)
    return pl.pallas_call(
        _norm_proj_kernel,
        grid=(m // tm, n // tn),
        in_specs=[pl.BlockSpec((tm, k), lambda i, j: (i, 0)),
                  pl.BlockSpec((1, k), lambda i, j: (0, 0)),
                  pl.BlockSpec((k, tn), lambda i, j: (0, j))],
        out_specs=pl.BlockSpec((tm, tn), lambda i, j: (i, j)),
        out_shape=jax.ShapeDtypeStruct((m, n), BF16),
        scratch_shapes=[pltpu.VMEM((tm, k), BF16)],
        compiler_params=_params(("parallel", "arbitrary")),
        name="norm_proj",
    )(x, g.reshape(1, k), w)


def _in_proj_kernel(x_ref, g_ref, w_ref, wa_ref, y_ref, s5_ref, a_ref, h_scr, perm_scr):
    j = pl.program_id(1)
    tm, tn = y_ref.shape
    n_s5 = N_S5IN // tn

    @pl.when(j == 0)
    def _():
        h = _rms_norm_bf16(x_ref[...], g_ref[...])
        h_scr[...] = h
        a_ref[...] = _dot(h, wa_ref[...])

    acc = _dot(h_scr[...], w_ref[...])

    @pl.when(j < n_s5)
    def _():
        for c in range(tn // LANES):
            perm_scr[c] = acc[:, c * LANES:(c + 1) * LANES]
        for s in range(S5_T):
            for c in range(tn // LANES):
                rows = perm_scr[c, pl.ds(s, tm // S5_T, stride=S5_T), :]
                s5_ref[s, :, c * LANES:(c + 1) * LANES] = rows.astype(BF16)

    @pl.when(j >= n_s5)
    def _():
        y_ref[...] = acc.astype(BF16)


def _in_proj(x, g, w, w_aux, tm, tn):
    m, k = x.shape
    tm = min(tm, m)
    n_s5 = N_S5IN // tn
    return pl.pallas_call(
        _in_proj_kernel,
        grid=(m // tm, (N_S5IN + N_MAIN) // tn),
        in_specs=[pl.BlockSpec((tm, k), lambda i, j: (i, 0)),
                  pl.BlockSpec((1, k), lambda i, j: (0, 0)),
                  pl.BlockSpec((k, tn), lambda i, j: (0, j)),
                  pl.BlockSpec((k, LANES), lambda i, j: (0, 0))],
        out_specs=[pl.BlockSpec((tm, tn), lambda i, j: (i, jnp.maximum(j - n_s5, 0))),
                   pl.BlockSpec((S5_T, tm // S5_T, tn), lambda i, j: (0, i, jnp.minimum(j, n_s5 - 1))),
                   pl.BlockSpec((tm, LANES), lambda i, j: (i, 0))],
        out_shape=[jax.ShapeDtypeStruct((m, N_MAIN), BF16),
                   jax.ShapeDtypeStruct((S5_T, m // S5_T, N_S5IN), BF16),
                   jax.ShapeDtypeStruct((m, LANES), F32)],
        scratch_shapes=[pltpu.VMEM((tm, k), BF16), pltpu.VMEM((tn // LANES, tm, LANES), F32)],
        compiler_params=_params(("parallel", "arbitrary"), 56 * MIB),
        name="in_proj",
    )(x, g.reshape(1, k), w, w_aux)


def _s5_operators(lam_re, lam_im, log_dt, b_re, b_im, c_re, c_im):
    hi = lax.Precision.HIGHEST
    G, P, C, T, J, LG = S5_GROUPS, S5_STATE, S5_GROUP, S5_T, S5_NBLK, S5_LANE_GROUPS
    lam = lax.complex(jnp.minimum(lam_re.astype(F32), -1e-4), lam_im.astype(F32))
    dt = jnp.exp(log_dt.astype(F32))[:, None]
    z = lam * dt
    lam_bar = jnp.exp(z)
    b_bar = ((lam_bar - 1.0) / lam)[..., None] * lax.complex(b_re.astype(F32), b_im.astype(F32))
    c = lax.complex(c_re.astype(F32), c_im.astype(F32))
    steps = jnp.arange(T + 1, dtype=F32)
    pw = jnp.exp(z[None] * steps[:, None, None].astype(jnp.complex64))

    kern = jnp.real(jnp.einsum('gop,kgp,gpi->kgoi', c, pw[:T], b_bar, precision=hi))
    s_idx = jnp.arange(T)[:, None]
    t_idx = jnp.arange(T)[None, :]
    lag = jnp.clip(t_idx - s_idx, 0, T - 1)
    toe = kern[lag] * (t_idx >= s_idx)[:, :, None, None, None].astype(F32)
    toe = toe.reshape(T, T, J, LG, C, C).transpose(2, 0, 3, 5, 1, 4)
    a2 = toe.reshape(J, T * LANES, T * C).astype(BF16)

    bs = pw[T - 1 - jnp.arange(T)][..., None] * b_bar[None]
    bs = jnp.stack([jnp.real(bs), jnp.imag(bs)], axis=0)
    bs = bs.reshape(2, T, J, LG, P, C).transpose(2, 1, 3, 5, 0, 4)
    b2 = bs.reshape(J, T * LANES, 2 * P).astype(BF16)

    cs = c[None] * pw[1:T + 1][:, :, None, :]
    cs = jnp.stack([jnp.real(cs), -jnp.imag(cs)], axis=0)
    cs = cs.reshape(2, T, J, LG, C, P).transpose(2, 0, 3, 5, 1, 4)
    c2 = cs.reshape(J, 2 * S5_SBLK, T * C).astype(BF16)

    lam_t = pw[T].reshape(1, G * P)
    return a2, b2, c2, jnp.real(lam_t), jnp.imag(lam_t)


def _s5_expanders():
    r = jnp.arange(S5_T * S5_GROUP)[:, None]
    c = jnp.arange(S5_T * LANES)[None, :]
    e_to = ((r // S5_GROUP == c // LANES) & (r % S5_GROUP == c % S5_GROUP)).astype(BF16)
    r = jnp.arange(2 * S5_STATE)[:, None]
    c = jnp.arange(2 * S5_SBLK)[None, :]
    e_rp = ((r // S5_STATE == c // S5_SBLK) & (r % S5_STATE == c % S5_STATE)).astype(BF16)
    return e_to, e_rp


def _expand_block_diag(dst_scr, compact, exp_ref, row_group, col_group):
    rows = compact.shape[0]
    cols = exp_ref.shape[1]
    slab = 2 * LANES
    rg = (lax.broadcasted_iota(jnp.int32, (rows, slab), 0) // row_group) % S5_LANE_GROUPS
    cl = lax.broadcasted_iota(jnp.int32, (rows, slab), 1)
    for c0 in range(0, cols, slab):
        cg = ((cl + c0) // col_group) % S5_LANE_GROUPS
        blk = _dot(compact, exp_ref[:, c0:c0 + slab])
        dst_scr[:, c0:c0 + slab] = jnp.where(rg == cg, blk, 0.0).astype(BF16)


def _s5_local_kernel(u_ref, b2_ref, erp_ref, s_ref, min_scr):
    @pl.when(pl.program_id(1) == 0)
    def _():
        _expand_block_diag(min_scr, b2_ref[0], erp_ref, S5_GROUP, S5_STATE)

    ucat = jnp.concatenate([u_ref[s] for s in range(S5_T)], axis=1)
    s = _dot(ucat, min_scr[...])
    for c in range(2 * S5_SLAB):
        s_ref[c] = s[:, c * LANES:(c + 1) * LANES]


def _s5_scan_kernel(sr_ref, si_ref, ar_ref, ai_ref, xr_ref, xi_ref, *, bsz, nchunk):
    ar = [jnp.broadcast_to(ar_ref[:, c * LANES:(c + 1) * LANES], (bsz, LANES)) for c in range(S5_SLAB)]
    ai = [jnp.broadcast_to(ai_ref[:, c * LANES:(c + 1) * LANES], (bsz, LANES)) for c in range(S5_SLAB)]

    def step(k, carry):
        rows = pl.ds(k, bsz, stride=nchunk)
        new = []
        for c in range(S5_SLAB):
            xr, xi = carry[c]
            xr_ref[c, rows, :] = xr
            xi_ref[c, rows, :] = xi
            nr = ar[c] * xr - ai[c] * xi + sr_ref[c, rows, :]
            ni = ar[c] * xi + ai[c] * xr + si_ref[c, rows, :]
            new.append((nr, ni))
        return tuple(new)

    zero = jnp.zeros((bsz, LANES), F32)
    lax.fori_loop(0, nchunk, step, tuple((zero, zero) for _ in range(S5_SLAB)))


def _s5_out_kernel(u_ref, a2_ref, c2_ref, eto_ref, xr_ref, xi_ref, d_ref, out_ref, mi_scr, mo_scr):
    @pl.when(pl.program_id(1) == 0)
    def _():
        _expand_block_diag(mi_scr, a2_ref[0], eto_ref, S5_GROUP, S5_GROUP)
        _expand_block_diag(mo_scr, c2_ref[0], eto_ref, S5_STATE, S5_GROUP)

    us = [u_ref[s] for s in range(S5_T)]
    ucat = jnp.concatenate(us, axis=1)
    xcat = jnp.concatenate([xr_ref[c] for c in range(S5_SLAB)] + [xi_ref[c] for c in range(S5_SLAB)],
                           axis=1).astype(BF16)
    y = _dot(ucat, mi_scr[...]) + _dot(xcat, mo_scr[...])
    d = d_ref[...]
    for t in range(S5_T):
        yt = y[:, t * LANES:(t + 1) * LANES] + d * us[t].astype(F32)
        out_ref[t] = jax.nn.gelu(yt).astype(BF16)


def _s5_glu_kernel(g_ref, z_ref, w_ref, b_ref, out_ref):
    g = g_ref[...]
    a = _dot(g, w_ref[...]) + b_ref[...]
    out_ref[...] = (g.astype(F32) * jax.nn.sigmoid(a) * _silu(z_ref[...].astype(F32))).astype(BF16)


def _s5_branch(s5_in, bsz, seq, ops, expanders, d_skip, w_glu, b_glu, rt):
    a2, b2, c2, lam_r, lam_i = ops
    e_to, e_rp = expanders
    nrow = bsz * seq // S5_T
    nchunk = seq // S5_T
    rt = min(rt, nrow)
    u_spec = pl.BlockSpec((S5_T, rt, LANES), lambda j, r: (0, r, j))

    def whole(a):
        return pl.BlockSpec(a.shape, lambda j, r: (0,) * a.ndim)

    def per_blk(a):
        return pl.BlockSpec((1,) + a.shape[1:], lambda j, r: (j,) + (0,) * (a.ndim - 1))

    s_loc = pl.pallas_call(
        _s5_local_kernel,
        grid=(S5_NBLK, nrow // rt),
        in_specs=[u_spec, per_blk(b2), whole(e_rp)],
        out_specs=pl.BlockSpec((2 * S5_SLAB, rt, LANES), lambda j, r: (j, r, 0)),
        out_shape=jax.ShapeDtypeStruct((S5_NBLK * 2 * S5_SLAB, nrow, LANES), F32),
        scratch_shapes=[pltpu.VMEM((S5_T * LANES, 2 * S5_SBLK), BF16)],
        compiler_params=_params(("parallel", "arbitrary")),
        name="s5_local",
    )(s5_in, b2, e_rp)

    xr, xi = pl.pallas_call(
        functools.partial(_s5_scan_kernel, bsz=bsz, nchunk=nchunk),
        grid=(S5_NBLK,),
        in_specs=[pl.BlockSpec((S5_SLAB, nrow, LANES), lambda j: (2 * j, 0, 0)),
                  pl.BlockSpec((S5_SLAB, nrow, LANES), lambda j: (2 * j + 1, 0, 0)),
                  pl.BlockSpec((1, S5_SBLK), lambda j: (0, j)),
                  pl.BlockSpec((1, S5_SBLK), lambda j: (0, j))],
        out_specs=[pl.BlockSpec((S5_SLAB, nrow, LANES), lambda j: (j, 0, 0))] * 2,
        out_shape=[jax.ShapeDtypeStruct((S5_NBLK * S5_SLAB, nrow, LANES), F32)] * 2,
        compiler_params=_params(("parallel",)),
        name="s5_scan",
    )(s_loc, s_loc, lam_r, lam_i)

    gel = pl.pallas_call(
        _s5_out_kernel,
        grid=(S5_NBLK, nrow // rt),
        in_specs=[u_spec, per_blk(a2), per_blk(c2), whole(e_to),
                  pl.BlockSpec((S5_SLAB, rt, LANES), lambda j, r: (j, r, 0)),
                  pl.BlockSpec((S5_SLAB, rt, LANES), lambda j, r: (j, r, 0)),
                  pl.BlockSpec((1, LANES), lambda j, r: (0, j))],
        out_specs=pl.BlockSpec((S5_T, rt, LANES), lambda j, r: (0, r, j)),
        out_shape=jax.ShapeDtypeStruct((S5_T, nrow, D_BRANCH), BF16),
        scratch_shapes=[pltpu.VMEM((S5_T * LANES, S5_T * LANES), BF16),
                        pltpu.VMEM((2 * S5_SBLK, S5_T * LANES), BF16)],
        compiler_params=_params(("parallel", "arbitrary")),
        name="s5_out",
    )(s5_in, a2, c2, e_to, xr, xi, d_skip.reshape(1, D_BRANCH))

    row_blk = pl.BlockSpec((pl.Squeezed(), rt, D_BRANCH), lambda t, r: (t, r, 0))
    return pl.pallas_call(
        _s5_glu_kernel,
        grid=(S5_T, nrow // rt),
        in_specs=[row_blk,
                  pl.BlockSpec((pl.Squeezed(), rt, D_BRANCH), lambda t, r: (t, r, 1)),
                  pl.BlockSpec((D_BRANCH, D_BRANCH), lambda t, r: (0, 0)),
                  pl.BlockSpec((1, D_BRANCH), lambda t, r: (0, 0))],
        out_specs=row_blk,
        out_shape=jax.ShapeDtypeStruct((S5_T, nrow, D_BRANCH), BF16),
        compiler_params=_params(("parallel", "parallel")),
        name="s5_glu",
    )(gel, s5_in, w_glu.astype(BF16), b_glu.reshape(1, D_BRANCH).astype(F32))


def _split3(x):
    hi = x.astype(BF16)
    r1 = x - hi.astype(F32)
    mid = r1.astype(BF16)
    lo = (r1 - mid.astype(F32)).astype(BF16)
    return hi, mid, lo


def _mlstm_kernel(q_ref, k_ref, v_ref, o_ref, z_ref, if_ref, cw_ref, cb_ref, bif_ref, ng_ref, out_ref,
                  ext_scr, c_scr, n_scr, m_scr):
    lc, dh, nh = ML_CHUNK, ML_HEAD_DIM, ML_HEADS
    tail = 8
    cidx = pl.program_id(1)

    @pl.when(cidx == 0)
    def _():
        ext_scr[0:tail, :] = jnp.zeros((tail, 2 * D_BRANCH), F32)
        c_scr[...] = jnp.zeros_like(c_scr)
        n_scr[...] = jnp.zeros_like(n_scr)
        m_scr[...] = jnp.zeros_like(m_scr)

    @pl.when(cidx > 0)
    def _():
        ext_scr[0:tail, :] = ext_scr[lc:lc + tail, :]

    ext_scr[tail:tail + lc, 0:D_BRANCH] = q_ref[...].astype(F32)
    ext_scr[tail:tail + lc, D_BRANCH:2 * D_BRANCH] = k_ref[...].astype(F32)
    conv = jnp.broadcast_to(cb_ref[...], (lc, 2 * D_BRANCH))
    for w in range(ML_CONV):
        off = tail - (ML_CONV - 1) + w
        conv = conv + cw_ref[w:w + 1, :] * ext_scr[off:off + lc, :]
    qk = _silu(conv)

    gts = if_ref[...] + bif_ref[...]
    lf = jnp.minimum(gts, 0.0) - jnp.log(1.0 + jnp.exp(-jnp.abs(gts)))
    gts_t = jnp.transpose(gts)[0:16, :]
    lf_t = jnp.transpose(lf)[0:16, :]
    row = lax.broadcasted_iota(jnp.int32, (lc, lc), 0)
    col = lax.broadcasted_iota(jnp.int32, (lc, lc), 1)
    causal = col <= row
    tril = jnp.where(causal, 1.0, 0.0).astype(BF16)
    triu = jnp.where(row <= col, 1.0, 0.0).astype(BF16)
    acum_c = sum(_dot(tril, p) for p in _split3(lf))
    acum_r = sum(_dot(p, triu) for p in _split3(lf_t))

    for h in range(nh):
        sl = slice(h * dh, (h + 1) * dh)
        qh = qk[:, h * dh:(h + 1) * dh]
        kh = qk[:, D_BRANCH + h * dh:D_BRANCH + (h + 1) * dh] * (dh ** -0.5)
        vh = v_ref[:, sl]
        qb = qh.astype(BF16)
        a_c = acum_c[:, nh + h:nh + h + 1]
        a_r = acum_r[nh + h:nh + h + 1, :]
        i_c = gts[:, h:h + 1]
        i_r = gts_t[h:h + 1, :]
        gtot = a_c[lc - 1:lc, :]
        m_old = m_scr[h][:, 0:1]
        dmat = jnp.where(causal, a_c - a_r + i_r, -jnp.inf)
        inter = a_c + m_old
        m_row = jnp.maximum(jnp.max(dmat, axis=-1, keepdims=True), inter)
        s = _dot_nt(qb, kh.astype(BF16)) * jnp.exp(dmat - m_row)
        sc = jnp.exp(inter - m_row)
        num = _dot(s.astype(BF16), vh) + sc * _dot(qb, c_scr[h].astype(BF16))
        den = jnp.sum(s, axis=-1, keepdims=True) + sc * jnp.sum(qh * n_scr[h], axis=-1, keepdims=True)
        hout = num / jnp.maximum(jnp.abs(den), jnp.exp(-m_row))

        kw_log = gtot - a_c + i_c
        m_new = jnp.maximum(gtot + m_old, jnp.max(kw_log, axis=0, keepdims=True))
        kw = jnp.exp(kw_log - m_new)
        decay = jnp.exp(gtot + m_old - m_new)
        kk = kh * kw
        c_scr[h] = decay * c_scr[h] + _dot(jnp.transpose(kk).astype(BF16), vh)
        n_scr[h] = decay * n_scr[h] + jnp.sum(kk, axis=0, keepdims=True)
        m_scr[h] = jnp.broadcast_to(m_new, (1, LANES))

        hn = hout * lax.rsqrt(jnp.mean(hout * hout, axis=-1, keepdims=True) + EPS) * ng_ref[:, sl]
        y = jax.nn.sigmoid(o_ref[:, sl].astype(F32)) * hn * _silu(z_ref[:, sl].astype(F32))
        out_ref[:, sl] = y.astype(BF16)


def _mlstm_branch(y_main, gates, bsz, seq, conv_w, conv_b, b_i, b_f, norm_g):
    lc = ML_CHUNK
    nc = seq // lc
    bif = jnp.zeros((1, LANES), F32).at[0, 0:ML_HEADS].set(b_i.astype(F32))
    bif = bif.at[0, ML_HEADS:2 * ML_HEADS].set(b_f.astype(F32))

    def col(cb):
        return pl.BlockSpec((lc, D_BRANCH), lambda b, c: (b * nc + c, cb))

    def const(shape):
        return pl.BlockSpec(shape, lambda b, c: (0,) * len(shape))

    return pl.pallas_call(
        _mlstm_kernel,
        grid=(bsz, nc),
        in_specs=[col(ML_BLK0), col(ML_BLK0 + 1), col(ML_BLK0 + 2), col(ML_BLK0 + 3), col(ML_BLK0 + 4),
                  pl.BlockSpec((lc, LANES), lambda b, c: (b * nc + c, 0)),
                  const((ML_CONV, 2 * D_BRANCH)), const((1, 2 * D_BRANCH)), const((1, LANES)),
                  const((1, D_BRANCH))],
        out_specs=pl.BlockSpec((lc, D_BRANCH), lambda b, c: (b * nc + c, 0)),
        out_shape=jax.ShapeDtypeStruct((bsz * seq, D_BRANCH), BF16),
        scratch_shapes=[pltpu.VMEM((lc + 8, 2 * D_BRANCH), F32),
                        pltpu.VMEM((ML_HEADS, ML_HEAD_DIM, ML_HEAD_DIM), F32),
                        pltpu.VMEM((ML_HEADS, 1, ML_HEAD_DIM), F32),
                        pltpu.VMEM((ML_HEADS, 1, LANES), F32)],
        compiler_params=_params(("parallel", "arbitrary")),
        name="mlstm",
    )(y_main, y_main, y_main, y_main, y_main, gates,
      conv_w.astype(F32), conv_b.reshape(1, 2 * D_BRANCH).astype(F32), bif,
      norm_g.reshape(1, D_BRANCH).astype(F32))


def _rope_tables(seq):
    half = DA_HEAD_DIM // 2
    inv = 1.0 / (ROPE_THETA ** (jnp.arange(0, DA_HEAD_DIM, 2, dtype=F32) / DA_HEAD_DIM))
    ang = jnp.arange(seq, dtype=F32)[:, None] * inv[None, :]
    cos, sin = jnp.cos(ang), jnp.sin(ang)
    cos_t = jnp.tile(jnp.concatenate([cos, cos], axis=1), (1, LANES // DA_HEAD_DIM))
    sin_t = jnp.tile(jnp.concatenate([-sin, sin], axis=1), (1, LANES // DA_HEAD_DIM))
    del half
    return cos_t, sin_t


def _rope_kernel(q_ref, k_ref, cos_ref, sin_ref, q1_ref, q2_ref, kr_ref):
    rep = D_BRANCH // LANES
    cos = jnp.concatenate([cos_ref[...]] * rep, axis=1)
    sin = jnp.concatenate([sin_ref[...]] * rep, axis=1)
    lane = lax.broadcasted_iota(jnp.int32, (1, D_BRANCH), 1)
    first_half = (lane % DA_HEAD_DIM) < (DA_HEAD_DIM // 2)
    comp0 = (lane % DA_V_DIM) < DA_HEAD_DIM
    half = DA_HEAD_DIM // 2

    def rot(x):
        partner = jnp.where(first_half, pltpu.roll(x, D_BRANCH - half, 1), pltpu.roll(x, half, 1))
        return x * cos + partner * sin

    q = rot(q_ref[...].astype(F32)) * (DA_HEAD_DIM ** -0.5)
    q1_ref[...] = jnp.where(comp0, q, 0.0).astype(BF16)
    q2_ref[...] = jnp.where(comp0, 0.0, q).astype(BF16)
    kr_ref[...] = rot(k_ref[...].astype(F32)).astype(BF16)


def _diff_attn_kernel(q1_ref, q2_ref, k_ref, v_ref, z_ref, lq1_ref, lk1_ref, lq2_ref, lk2_ref, g_ref,
                      out_ref, m_scr, l_scr, acc_scr, *, lambda_init, tq):
    qi = pl.program_id(2)
    ki = pl.program_id(3)

    @pl.when(ki == 0)
    def _():
        m_scr[...] = jnp.full_like(m_scr, -jnp.inf)
        l_scr[...] = jnp.zeros_like(l_scr)
        acc_scr[...] = jnp.zeros_like(acc_scr)

    def update(masked):
        k = k_ref[...]
        v = v_ref[...]
        if masked:
            row = lax.broadcasted_iota(jnp.int32, (tq, tq), 0)
            col = lax.broadcasted_iota(jnp.int32, (tq, tq), 1)
            keep = col <= row
        for c, q_ref in enumerate((q1_ref, q2_ref)):
            s = _dot_nt(q_ref[...], k)
            if masked:
                s = jnp.where(keep, s, -jnp.inf)
            m_prev = m_scr[c]
            m_new = jnp.maximum(m_prev, jnp.max(s, axis=-1, keepdims=True))
            alpha = jnp.exp(m_prev - m_new)
            p = jnp.exp(s - m_new[:, 0:1])
            l_scr[c] = alpha * l_scr[c] + jnp.sum(p, axis=-1, keepdims=True)
            acc_scr[c] = alpha * acc_scr[c] + _dot(p.astype(BF16), v)
            m_scr[c] = m_new

    @pl.when(ki < qi)
    def _():
        update(False)

    @pl.when(ki == qi)
    def _():
        update(True)
        lam = (jnp.exp(jnp.sum(lq1_ref[...] * lk1_ref[...], axis=-1, keepdims=True))
               - jnp.exp(jnp.sum(lq2_ref[...] * lk2_ref[...], axis=-1, keepdims=True)) + lambda_init)
        o = acc_scr[0] / l_scr[0] - lam * (acc_scr[1] / l_scr[1])
        o = o * lax.rsqrt(jnp.mean(o * o, axis=-1, keepdims=True) + EPS) * g_ref[...]
        o = o * (1.0 - lambda_init) * _silu(z_ref[...].astype(F32))
        out_ref[...] = o.astype(BF16)


def _diff_attn_branch(y_main, bsz, seq, cos_t, sin_t, lq1, lk1, lq2, lk2, subln_g, lambda_init, tq, rope_rows):
    ntok = bsz * seq
    tr = min(rope_rows, seq)
    nrb = seq // tr
    q1, q2, kr = pl.pallas_call(
        _rope_kernel,
        grid=(bsz, nrb),
        in_specs=[pl.BlockSpec((tr, D_BRANCH), lambda b, r: (b * nrb + r, DA_BLK0)),
                  pl.BlockSpec((tr, D_BRANCH), lambda b, r: (b * nrb + r, DA_BLK0 + 1)),
                  pl.BlockSpec((tr, LANES), lambda b, r: (r, 0)),
                  pl.BlockSpec((tr, LANES), lambda b, r: (r, 0))],
        out_specs=[pl.BlockSpec((tr, D_BRANCH), lambda b, r: (b * nrb + r, 0))] * 3,
        out_shape=[jax.ShapeDtypeStruct((ntok, D_BRANCH), BF16)] * 3,
        compiler_params=_params(("parallel", "parallel")),
        name="rope",
    )(y_main, y_main, cos_t, sin_t)

    tq = min(tq, seq)
    nq = seq // tq
    vblk0 = (DA_BLK0 + 2) * D_BRANCH // DA_V_DIM
    zblk0 = (DA_BLK0 + 3) * D_BRANCH // DA_V_DIM

    def qspec():
        return pl.BlockSpec((tq, DA_V_DIM), lambda b, h, i, j: (b * nq + i, h))

    def kspec(blk0):
        return pl.BlockSpec((tq, DA_V_DIM), lambda b, h, i, j: (b * nq + jnp.minimum(i, j), blk0 + h))

    def vec(n):
        return pl.BlockSpec((1, n), lambda b, h, i, j: (0, 0))

    lvec = [a.reshape(1, DA_HEAD_DIM).astype(F32) for a in (lq1, lk1, lq2, lk2)]
    return pl.pallas_call(
        functools.partial(_diff_attn_kernel, lambda_init=lambda_init, tq=tq),
        grid=(bsz, DA_HEADS, nq, nq),
        in_specs=[qspec(), qspec(), kspec(0), kspec(vblk0),
                  pl.BlockSpec((tq, DA_V_DIM), lambda b, h, i, j: (b * nq + i, zblk0 + h)),
                  vec(DA_HEAD_DIM), vec(DA_HEAD_DIM), vec(DA_HEAD_DIM), vec(DA_HEAD_DIM), vec(DA_V_DIM)],
        out_specs=pl.BlockSpec((tq, DA_V_DIM), lambda b, h, i, j: (b * nq + i, h)),
        out_shape=jax.ShapeDtypeStruct((ntok, D_BRANCH), BF16),
        scratch_shapes=[pltpu.VMEM((2, tq, LANES), F32), pltpu.VMEM((2, tq, LANES), F32),
                        pltpu.VMEM((2, tq, DA_V_DIM), F32)],
        compiler_params=_params(("parallel", "parallel", "parallel", "arbitrary")),
        name="diff_attn",
    )(q1, q2, kr, y_main, y_main, *lvec, subln_g.reshape(1, DA_V_DIM).astype(F32))


def _mem_attn_kernel(q_ref, z_ref, k_ref, v_ref, out_ref):
    dh = XA_HEAD_DIM
    for h in range(XA_HEADS):
        sl = slice(h * dh, (h + 1) * dh)
        s = _dot_nt(q_ref[:, sl], k_ref[:, sl]) * (dh ** -0.5)
        p = jnp.exp(s - jnp.max(s, axis=-1, keepdims=True))
        p = p / jnp.sum(p, axis=-1, keepdims=True)
        o = _dot(p.astype(BF16), v_ref[:, sl])
        out_ref[:, sl] = (o * _silu(z_ref[:, sl].astype(F32))).astype(BF16)


def _mem_attn_branch(y_main, kv, bsz, seq, mem_len, tq):
    tq = min(tq, seq)
    nq = seq // tq
    return pl.pallas_call(
        _mem_attn_kernel,
        grid=(bsz, nq),
        in_specs=[pl.BlockSpec((tq, D_BRANCH), lambda b, i: (b * nq + i, XA_BLK0)),
                  pl.BlockSpec((tq, D_BRANCH), lambda b, i: (b * nq + i, XA_BLK0 + 1)),
                  pl.BlockSpec((mem_len, D_BRANCH), lambda b, i: (b, 0)),
                  pl.BlockSpec((mem_len, D_BRANCH), lambda b, i: (b, 1))],
        out_specs=pl.BlockSpec((tq, D_BRANCH), lambda b, i: (b * nq + i, 0)),
        out_shape=jax.ShapeDtypeStruct((bsz * seq, D_BRANCH), BF16),
        compiler_params=_params(("parallel", "parallel")),
        name="mem_attn",
    )(y_main, y_main, kv, kv)


def _merge_kernel(*refs):
    y_refs = refs[0:N_BRANCH]
    g_refs = refs[N_BRANCH:2 * N_BRANCH]
    w_refs = refs[2 * N_BRANCH:3 * N_BRANCH]
    out_ref, y0_scr, perm_scr = refs[3 * N_BRANCH:]

    @pl.when(pl.program_id(1) == 0)
    def _():
        rows = y_refs[0].shape[1]
        nslab = D_BRANCH // LANES
        for s in range(S5_T):
            ys = y_refs[0][s].astype(F32)
            for c in range(nslab):
                perm_scr[c, pl.ds(s, rows, stride=S5_T), :] = ys[:, c * LANES:(c + 1) * LANES]
        for c in range(nslab):
            y0_scr[:, c * LANES:(c + 1) * LANES] = perm_scr[c].astype(BF16)

    acc = jax.nn.sigmoid(g_refs[0][...].astype(F32)) * _dot(y0_scr[...], w_refs[0][0])
    for y_ref, g_ref, w_ref in zip(y_refs[1:], g_refs[1:], w_refs[1:]):
        acc = acc + jax.nn.sigmoid(g_ref[...].astype(F32)) * _dot(y_ref[...], w_ref[0])
    out_ref[...] = acc.astype(BF16)


def _merge(branches, y_main, w_branch, tm, tn):
    ntok = y_main.shape[0]
    tm = min(tm, ntok)
    ncol = D_MODEL // tn
    y_specs = ([pl.BlockSpec((S5_T, tm // S5_T, D_BRANCH), lambda i, n: (0, i, 0))]
               + [pl.BlockSpec((tm, D_BRANCH), lambda i, n: (i, 0))] * (N_BRANCH - 1))
    g_specs = [pl.BlockSpec((tm, tn), lambda i, n, b=b: (i, (GATE_COL0 + b * D_MODEL) // tn + n))
               for b in range(N_BRANCH)]
    w_specs = [pl.BlockSpec((1, D_BRANCH, tn), lambda i, n, b=b: (b, 0, n)) for b in range(N_BRANCH)]
    return pl.pallas_call(
        _merge_kernel,
        grid=(ntok // tm, ncol),
        in_specs=y_specs + g_specs + w_specs,
        out_specs=pl.BlockSpec((tm, tn), lambda i, n: (i, n)),
        out_shape=jax.ShapeDtypeStruct((ntok, D_MODEL), BF16),
        scratch_shapes=[pltpu.VMEM((tm, D_BRANCH), BF16), pltpu.VMEM((D_BRANCH // LANES, tm, LANES), F32)],
        compiler_params=_params(("parallel", "arbitrary")),
        name="merge",
    )(*branches, *([y_main] * N_BRANCH), *([w_branch] * N_BRANCH))


def _out_proj_kernel(m_ref, w_ref, g_ref, x_ref, out_ref):
    o = _dot(m_ref[...], w_ref[...])
    o = o * lax.rsqrt(jnp.mean(o * o, axis=-1, keepdims=True) + EPS) * g_ref[...]
    out_ref[...] = x_ref[...] + o


def _out_proj(merged, w_out, g_post, x, tm):
    ntok = x.shape[0]
    tm = min(tm, ntok)
    return pl.pallas_call(
        _out_proj_kernel,
        grid=(ntok // tm,),
        in_specs=[pl.BlockSpec((tm, D_MODEL), lambda i: (i, 0)),
                  pl.BlockSpec((D_MODEL, D_MODEL), lambda i: (0, 0)),
                  pl.BlockSpec((1, D_MODEL), lambda i: (0, 0)),
                  pl.BlockSpec((tm, D_MODEL), lambda i: (i, 0))],
        out_specs=pl.BlockSpec((tm, D_MODEL), lambda i: (i, 0)),
        out_shape=jax.ShapeDtypeStruct((ntok, D_MODEL), F32),
        compiler_params=_params(("parallel",)),
        name="out_proj",
    )(merged, w_out, g_post.reshape(1, D_MODEL).astype(F32), x)


def kernel(x, mem, g_pre, w_in, s5_lam_re, s5_lam_im, s5_log_dt, s5_b_re, s5_b_im, s5_c_re, s5_c_im, s5_d, s5_w_glu, s5_b_glu, ml_conv_w, ml_conv_b, ml_b_i, ml_b_f, ml_norm_g, da_lq1, da_lk1, da_lq2, da_lk2, da_subln_g, g_mem, xa_w_kv, w_branch, w_out, g_post):
    bsz, seq, _ = x.shape
    mem_len = mem.shape[1]
    depth = w_in.shape[0]
    ntok = bsz * seq
    cos_t, sin_t = _rope_tables(seq)
    xf = x.reshape(ntok, D_MODEL).astype(F32)
    memf = mem.reshape(bsz * mem_len, D_MODEL).astype(F32)
    gate0 = 7 * D_BRANCH
    expanders = _s5_expanders()
    for l in range(depth):
        lambda_init = 0.8 - 0.6 * math.exp(-0.3 * l)
        w_l = w_in[l]
        w_main = jnp.concatenate([w_l[:, :gate0], w_l[:, gate0 + 2 * ML_HEADS:]], axis=1).astype(BF16)
        w_gate = jnp.pad(w_l[:, gate0:gate0 + 2 * ML_HEADS], ((0, 0), (0, LANES - 2 * ML_HEADS))).astype(BF16)
        y_main, s5_in, gates = _in_proj(xf, g_pre[l], w_main, w_gate, tm=1024, tn=1024)
        kv = _norm_proj(memf, g_mem[l], xa_w_kv[l].astype(BF16), tm=1024, tn=1024)

        ops = _s5_operators(s5_lam_re[l], s5_lam_im[l], s5_log_dt[l], s5_b_re[l], s5_b_im[l],
                            s5_c_re[l], s5_c_im[l])
        y_s5 = _s5_branch(s5_in, bsz, seq, ops, expanders, s5_d[l].astype(F32), s5_w_glu[l], s5_b_glu[l], rt=512)
        y_ml = _mlstm_branch(y_main, gates, bsz, seq, ml_conv_w[l], ml_conv_b[l], ml_b_i[l], ml_b_f[l],
                             ml_norm_g[l])
        y_da = _diff_attn_branch(y_main, bsz, seq, cos_t, sin_t, da_lq1[l], da_lk1[l], da_lq2[l], da_lk2[l],
                                 da_subln_g[l], lambda_init, tq=512, rope_rows=512)
        y_xa = _mem_attn_branch(y_main, kv, bsz, seq, mem_len, tq=512)

        merged = _merge((y_s5, y_ml, y_da, y_xa), y_main, w_branch[l].astype(BF16), tm=512, tn=512)
        xf = _out_proj(merged, w_out[l].astype(BF16), g_post[l], xf, tm=256)
    return xf.reshape(bsz, seq, D_MODEL).astype(x.dtype)
```

```python
import functools
import math

import jax
import jax.numpy as jnp
from jax import lax
from jax.experimental import pallas as pl
from jax.experimental.pallas import tpu as pltpu

F32 = jnp.float32
BF16 = jnp.bfloat16

D_MODEL = 2048
D_BRANCH = 1024
N_BRANCH = 4
S5_GROUP = 16
S5_GROUPS = D_BRANCH // S5_GROUP
S5_STATE = 64
ML_HEADS = 4
ML_HEAD_DIM = D_BRANCH // ML_HEADS
ML_CHUNK = 128
ML_CONV = 4
DA_HEADS = 8
DA_HEAD_DIM = 64
DA_V_DIM = 2 * DA_HEAD_DIM
XA_HEADS = 4
XA_HEAD_DIM = D_BRANCH // XA_HEADS
ROPE_THETA = 10000.0
EPS = 1e-6
LOG2E = 1.4426950408889634

LANES = 128
S5_T = 16
S5_LANE_GROUPS = LANES // S5_GROUP
S5_NBLK = D_BRANCH // LANES
S5_SBLK = S5_LANE_GROUPS * S5_STATE
S5_SLAB = S5_SBLK // LANES
N_S5IN = 2 * D_BRANCH
N_MAIN = 11 * D_BRANCH + N_BRANCH * D_MODEL
ML_BLK0, DA_BLK0, XA_BLK0 = 0, 5, 9
GATE_COL0 = 11 * D_BRANCH
MIB = 1024 * 1024
VMEM_LIMIT = 48 * MIB


def _params(sem, vmem=VMEM_LIMIT):
    return pltpu.CompilerParams(dimension_semantics=sem, vmem_limit_bytes=vmem)


def _silu(x):
    return x * jax.nn.sigmoid(x)


def _dot(a, b):
    return jnp.dot(a, b, preferred_element_type=F32)


def _dot_nt(a, b):
    return lax.dot_general(a, b, (((1,), (1,)), ((), ())), preferred_element_type=F32)


def _rms_norm_bf16(x, g):
    ms = jnp.mean(x * x, axis=-1, keepdims=True)
    return (x * lax.rsqrt(ms + EPS) * g).astype(BF16)


def _norm_proj_kernel(x_ref, g_ref, w_ref, y_ref, h_scr):
    @pl.when(pl.program_id(1) == 0)
    def _():
        h_scr[...] = _rms_norm_bf16(x_ref[...], g_ref[...])

    y_ref[...] = _dot(h_scr[...], w_ref[...]).astype(y_ref.dtype)


def _norm_proj(x, g, w, tm, tn):
    m, k = x.shape
    n = w.shape[1]
    tm = min(tm, m)
    return pl.pallas_call(
        _norm_proj_kernel,
        grid=(m // tm, n // tn),
        in_specs=[pl.BlockSpec((tm, k), lambda i, j: (i, 0)),
                  pl.BlockSpec((1, k), lambda i, j: (0, 0)),
                  pl.BlockSpec((k, tn), lambda i, j: (0, j))],
        out_specs=pl.BlockSpec((tm, tn), lambda i, j: (i, j)),
        out_shape=jax.ShapeDtypeStruct((m, n), BF16),
        scratch_shapes=[pltpu.VMEM((tm, k), BF16)],
        compiler_params=_params(("parallel", "arbitrary")),
        name="norm_proj",
    )(x, g.reshape(1, k), w)


def _in_proj_kernel(x_ref, g_ref, w_ref, wa_ref, y_ref, s5_ref, a_ref, h_scr, perm_scr):
    j = pl.program_id(1)
    tm, tn = y_ref.shape
    n_s5 = N_S5IN // tn

    @pl.when(j == 0)
    def _():
        h = _rms_norm_bf16(x_ref[...], g_ref[...])
        h_scr[...] = h
        a_ref[...] = _dot(h, wa_ref[...])

    @pl.when(j < n_s5)
    def _():
        acc = _dot(h_scr[...], w_ref[...])
        for c in range(tn // LANES):
            perm_scr[c] = acc[:, c * LANES:(c + 1) * LANES]
        for s in range(S5_T):
            for c in range(tn // LANES):
                rows = perm_scr[c, pl.ds(s, tm // S5_T, stride=S5_T), :]
                s5_ref[s, :, c * LANES:(c + 1) * LANES] = rows.astype(BF16)

    @pl.when(j >= n_s5)
    def _():
        y_ref[...] = _dot(h_scr[...], w_ref[...]).astype(BF16)


def _in_proj(x, g, w, w_aux, tm, tn):
    m, k = x.shape
    tm = min(tm, m)
    n_s5 = N_S5IN // tn
    return pl.pallas_call(
        _in_proj_kernel,
        grid=(m // tm, (N_S5IN + N_MAIN) // tn),
        in_specs=[pl.BlockSpec((tm, k), lambda i, j: (i, 0)),
                  pl.BlockSpec((1, k), lambda i, j: (0, 0)),
                  pl.BlockSpec((k, tn), lambda i, j: (0, j)),
                  pl.BlockSpec((k, LANES), lambda i, j: (0, 0))],
        out_specs=[pl.BlockSpec((tm, tn), lambda i, j: (i, jnp.maximum(j - n_s5, 0))),
                   pl.BlockSpec((S5_T, tm // S5_T, tn), lambda i, j: (0, i, jnp.minimum(j, n_s5 - 1))),
                   pl.BlockSpec((tm, LANES), lambda i, j: (i, 0))],
        out_shape=[jax.ShapeDtypeStruct((m, N_MAIN), BF16),
                   jax.ShapeDtypeStruct((S5_T, m // S5_T, N_S5IN), BF16),
                   jax.ShapeDtypeStruct((m, LANES), F32)],
        scratch_shapes=[pltpu.VMEM((tm, k), BF16), pltpu.VMEM((tn // LANES, tm, LANES), F32)],
        compiler_params=_params(("parallel", "arbitrary"), 56 * MIB),
        name="in_proj",
    )(x, g.reshape(1, k), w, w_aux)


def _s5_operators(lam_re, lam_im, log_dt, b_re, b_im, c_re, c_im):
    hi = lax.Precision.HIGHEST
    G, P, C, T, J, LG = S5_GROUPS, S5_STATE, S5_GROUP, S5_T, S5_NBLK, S5_LANE_GROUPS
    lam = lax.complex(jnp.minimum(lam_re.astype(F32), -1e-4), lam_im.astype(F32))
    dt = jnp.exp(log_dt.astype(F32))[:, None]
    z = lam * dt
    lam_bar = jnp.exp(z)
    b_bar = ((lam_bar - 1.0) / lam)[..., None] * lax.complex(b_re.astype(F32), b_im.astype(F32))
    c = lax.complex(c_re.astype(F32), c_im.astype(F32))
    steps = jnp.arange(T + 1, dtype=F32)
    pw = jnp.exp(z[None] * steps[:, None, None].astype(jnp.complex64))

    kern = jnp.real(jnp.einsum('gop,kgp,gpi->kgoi', c, pw[:T], b_bar, precision=hi))
    s_idx = jnp.arange(T)[:, None]
    t_idx = jnp.arange(T)[None, :]
    lag = jnp.clip(t_idx - s_idx, 0, T - 1)
    toe = kern[lag] * (t_idx >= s_idx)[:, :, None, None, None].astype(F32)
    toe = toe.reshape(T, T, J, LG, C, C).transpose(2, 0, 3, 5, 1, 4)
    a2 = toe.reshape(J, T * LANES, T * C).astype(BF16)

    bs = pw[T - 1 - jnp.arange(T)][..., None] * b_bar[None]
    bs = jnp.stack([jnp.real(bs), jnp.imag(bs)], axis=0)
    bs = bs.reshape(2, T, J, LG, P, C).transpose(2, 1, 3, 5, 0, 4)
    b2 = bs.reshape(J, T * LANES, 2 * P).astype(BF16)

    cs = c[None] * pw[1:T + 1][:, :, None, :]
    cs = jnp.stack([jnp.real(cs), -jnp.imag(cs)], axis=0)
    cs = cs.reshape(2, T, J, LG, C, P).transpose(2, 0, 3, 5, 1, 4)
    c2 = cs.reshape(J, 2 * S5_SBLK, T * C).astype(BF16)

    lam_t = pw[T].reshape(1, G * P)
    return a2, b2, c2, jnp.real(lam_t), jnp.imag(lam_t)


def _s5_expanders():
    r = jnp.arange(S5_T * S5_GROUP)[:, None]
    c = jnp.arange(S5_T * LANES)[None, :]
    e_to = ((r // S5_GROUP == c // LANES) & (r % S5_GROUP == c % S5_GROUP)).astype(BF16)
    r = jnp.arange(2 * S5_STATE)[:, None]
    c = jnp.arange(2 * S5_SBLK)[None, :]
    e_rp = ((r // S5_STATE == c // S5_SBLK) & (r % S5_STATE == c % S5_STATE)).astype(BF16)
    return e_to, e_rp


def _expand_block_diag(dst_scr, compact, exp_ref, row_group, col_group):
    rows = compact.shape[0]
    cols = exp_ref.shape[1]
    slab = 2 * LANES
    rg = (lax.broadcasted_iota(jnp.int32, (rows, slab), 0) // row_group) % S5_LANE_GROUPS
    cl = lax.broadcasted_iota(jnp.int32, (rows, slab), 1)
    for c0 in range(0, cols, slab):
        cg = ((cl + c0) // col_group) % S5_LANE_GROUPS
        blk = _dot(compact, exp_ref[:, c0:c0 + slab])
        dst_scr[:, c0:c0 + slab] = jnp.where(rg == cg, blk, 0.0).astype(BF16)


def _s5_local_kernel(u_ref, b2_ref, erp_ref, s_ref, min_scr):
    @pl.when(pl.program_id(1) == 0)
    def _():
        _expand_block_diag(min_scr, b2_ref[0], erp_ref, S5_GROUP, S5_STATE)

    ucat = jnp.concatenate([u_ref[s] for s in range(S5_T)], axis=1)
    s = _dot(ucat, min_scr[...])
    for c in range(2 * S5_SLAB):
        s_ref[c] = s[:, c * LANES:(c + 1) * LANES]


def _s5_scan_kernel(sr_ref, si_ref, ar_ref, ai_ref, xr_ref, xi_ref, *, bsz, nchunk):
    ar = [jnp.broadcast_to(ar_ref[:, c * LANES:(c + 1) * LANES], (bsz, LANES)) for c in range(S5_SLAB)]
    ai = [jnp.broadcast_to(ai_ref[:, c * LANES:(c + 1) * LANES], (bsz, LANES)) for c in range(S5_SLAB)]

    def step(k, carry):
        rows = pl.ds(k, bsz, stride=nchunk)
        new = []
        for c in range(S5_SLAB):
            xr, xi = carry[c]
            xr_ref[c, rows, :] = xr
            xi_ref[c, rows, :] = xi
            nr = ar[c] * xr - ai[c] * xi + sr_ref[c, rows, :]
            ni = ar[c] * xi + ai[c] * xr + si_ref[c, rows, :]
            new.append((nr, ni))
        return tuple(new)

    zero = jnp.zeros((bsz, LANES), F32)
    lax.fori_loop(0, nchunk, step, tuple((zero, zero) for _ in range(S5_SLAB)))


def _s5_out_kernel(u_ref, a2_ref, c2_ref, eto_ref, xr_ref, xi_ref, d_ref, out_ref, mi_scr, mo_scr):
    @pl.when(pl.program_id(1) == 0)
    def _():
        _expand_block_diag(mi_scr, a2_ref[0], eto_ref, S5_GROUP, S5_GROUP)
        _expand_block_diag(mo_scr, c2_ref[0], eto_ref, S5_STATE, S5_GROUP)

    us = [u_ref[s] for s in range(S5_T)]
    ucat = jnp.concatenate(us, axis=1)
    xcat = jnp.concatenate([xr_ref[c] for c in range(S5_SLAB)] + [xi_ref[c] for c in range(S5_SLAB)],
                           axis=1).astype(BF16)
    y = _dot(ucat, mi_scr[...]) + _dot(xcat, mo_scr[...])
    d = d_ref[...]
    for t in range(S5_T):
        yt = y[:, t * LANES:(t + 1) * LANES] + d * us[t].astype(F32)
        out_ref[t] = jax.nn.gelu(yt).astype(BF16)


def _s5_glu_kernel(g_ref, z_ref, w_ref, b_ref, out_ref):
    g = g_ref[...]
    a = _dot(g, w_ref[...]) + b_ref[...]
    out_ref[...] = (g.astype(F32) * jax.nn.sigmoid(a) * _silu(z_ref[...].astype(F32))).astype(BF16)


def _s5_branch(s5_in, bsz, seq, ops, expanders, d_skip, w_glu, b_glu, rt):
    a2, b2, c2, lam_r, lam_i = ops
    e_to, e_rp = expanders
    nrow = bsz * seq // S5_T
    nchunk = seq // S5_T
    rt = min(rt, nrow)
    u_spec = pl.BlockSpec((S5_T, rt, LANES), lambda j, r: (0, r, j))

    def whole(a):
        return pl.BlockSpec(a.shape, lambda j, r: (0,) * a.ndim)

    def per_blk(a):
        return pl.BlockSpec((1,) + a.shape[1:], lambda j, r: (j,) + (0,) * (a.ndim - 1))

    s_loc = pl.pallas_call(
        _s5_local_kernel,
        grid=(S5_NBLK, nrow // rt),
        in_specs=[u_spec, per_blk(b2), whole(e_rp)],
        out_specs=pl.BlockSpec((2 * S5_SLAB, rt, LANES), lambda j, r: (j, r, 0)),
        out_shape=jax.ShapeDtypeStruct((S5_NBLK * 2 * S5_SLAB, nrow, LANES), F32),
        scratch_shapes=[pltpu.VMEM((S5_T * LANES, 2 * S5_SBLK), BF16)],
        compiler_params=_params(("parallel", "arbitrary")),
        name="s5_local",
    )(s5_in, b2, e_rp)

    xr, xi = pl.pallas_call(
        functools.partial(_s5_scan_kernel, bsz=bsz, nchunk=nchunk),
        grid=(S5_NBLK,),
        in_specs=[pl.BlockSpec((S5_SLAB, nrow, LANES), lambda j: (2 * j, 0, 0)),
                  pl.BlockSpec((S5_SLAB, nrow, LANES), lambda j: (2 * j + 1, 0, 0)),
                  pl.BlockSpec((1, S5_SBLK), lambda j: (0, j)),
                  pl.BlockSpec((1, S5_SBLK), lambda j: (0, j))],
        out_specs=[pl.BlockSpec((S5_SLAB, nrow, LANES), lambda j: (j, 0, 0))] * 2,
        out_shape=[jax.ShapeDtypeStruct((S5_NBLK * S5_SLAB, nrow, LANES), F32)] * 2,
        compiler_params=_params(("parallel",)),
        name="s5_scan",
    )(s_loc, s_loc, lam_r, lam_i)

    gel = pl.pallas_call(
        _s5_out_kernel,
        grid=(S5_NBLK, nrow // rt),
        in_specs=[u_spec, per_blk(a2), per_blk(c2), whole(e_to),
                  pl.BlockSpec((S5_SLAB, rt, LANES), lambda j, r: (j, r, 0)),
                  pl.BlockSpec((S5_SLAB, rt, LANES), lambda j, r: (j, r, 0)),
                  pl.BlockSpec((1, LANES), lambda j, r: (0, j))],
        out_specs=pl.BlockSpec((S5_T, rt, LANES), lambda j, r: (0, r, j)),
        out_shape=jax.ShapeDtypeStruct((S5_T, nrow, D_BRANCH), BF16),
        scratch_shapes=[pltpu.VMEM((S5_T * LANES, S5_T * LANES), BF16),
                        pltpu.VMEM((2 * S5_SBLK, S5_T * LANES), BF16)],
        compiler_params=_params(("parallel", "arbitrary")),
        name="s5_out",
    )(s5_in, a2, c2, e_to, xr, xi, d_skip.reshape(1, D_BRANCH))

    row_blk = pl.BlockSpec((pl.Squeezed(), rt, D_BRANCH), lambda t, r: (t, r, 0))
    return pl.pallas_call(
        _s5_glu_kernel,
        grid=(S5_T, nrow // rt),
        in_specs=[row_blk,
                  pl.BlockSpec((pl.Squeezed(), rt, D_BRANCH), lambda t, r: (t, r, 1)),
                  pl.BlockSpec((D_BRANCH, D_BRANCH), lambda t, r: (0, 0)),
                  pl.BlockSpec((1, D_BRANCH), lambda t, r: (0, 0))],
        out_specs=row_blk,
        out_shape=jax.ShapeDtypeStruct((S5_T, nrow, D_BRANCH), BF16),
        compiler_params=_params(("parallel", "parallel")),
        name="s5_glu",
    )(gel, s5_in, w_glu.astype(BF16), b_glu.reshape(1, D_BRANCH).astype(F32))


def _split3(x):
    hi = x.astype(BF16)
    r1 = x - hi.astype(F32)
    mid = r1.astype(BF16)
    lo = (r1 - mid.astype(F32)).astype(BF16)
    return hi, mid, lo


def _mlstm_kernel(q_ref, k_ref, v_ref, o_ref, z_ref, if_ref, cw_ref, cb_ref, bif_ref, ng_ref, out_ref,
                  ext_scr, c_scr, n_scr, m_scr):
    lc, dh, nh = ML_CHUNK, ML_HEAD_DIM, ML_HEADS
    tail = 8
    cidx = pl.program_id(1)

    @pl.when(cidx == 0)
    def _():
        ext_scr[0:tail, :] = jnp.zeros((tail, 2 * D_BRANCH), F32)
        c_scr[...] = jnp.zeros_like(c_scr)
        n_scr[...] = jnp.zeros_like(n_scr)
        m_scr[...] = jnp.zeros_like(m_scr)

    @pl.when(cidx > 0)
    def _():
        ext_scr[0:tail, :] = ext_scr[lc:lc + tail, :]

    ext_scr[tail:tail + lc, 0:D_BRANCH] = q_ref[...].astype(F32)
    ext_scr[tail:tail + lc, D_BRANCH:2 * D_BRANCH] = k_ref[...].astype(F32)
    conv = jnp.broadcast_to(cb_ref[...], (lc, 2 * D_BRANCH))
    for w in range(ML_CONV):
        off = tail - (ML_CONV - 1) + w
        conv = conv + cw_ref[w:w + 1, :] * ext_scr[off:off + lc, :]
    qk = _silu(conv)

    gts = if_ref[...] + bif_ref[...]
    lf = jnp.minimum(gts, 0.0) - jnp.log(1.0 + jnp.exp(-jnp.abs(gts)))
    gts_t = jnp.transpose(gts)[0:16, :]
    lf_t = jnp.transpose(lf)[0:16, :]
    row = lax.broadcasted_iota(jnp.int32, (lc, lc), 0)
    col = lax.broadcasted_iota(jnp.int32, (lc, lc), 1)
    causal = col <= row
    tril = jnp.where(causal, 1.0, 0.0).astype(BF16)
    triu = jnp.where(row <= col, 1.0, 0.0).astype(BF16)
    acum_c = sum(_dot(tril, p) for p in _split3(lf))
    acum_r = sum(_dot(p, triu) for p in _split3(lf_t))

    for h in range(nh):
        sl = slice(h * dh, (h + 1) * dh)
        qh = qk[:, h * dh:(h + 1) * dh]
        kh = qk[:, D_BRANCH + h * dh:D_BRANCH + (h + 1) * dh] * (dh ** -0.5)
        vh = v_ref[:, sl]
        qb = qh.astype(BF16)
        a_c = acum_c[:, nh + h:nh + h + 1]
        a_r = acum_r[nh + h:nh + h + 1, :]
        i_c = gts[:, h:h + 1]
        i_r = gts_t[h:h + 1, :]
        gtot = a_c[lc - 1:lc, :]
        m_old = m_scr[h][:, 0:1]
        dmat = jnp.where(causal, a_c - a_r + i_r, -jnp.inf)
        inter = a_c + m_old
        m_row = jnp.maximum(jnp.max(dmat, axis=-1, keepdims=True), inter)
        s = _dot_nt(qb, kh.astype(BF16)) * jnp.exp(dmat - m_row)
        sc = jnp.exp(inter - m_row)
        num = _dot(s.astype(BF16), vh) + sc * _dot(qb, c_scr[h].astype(BF16))
        den = jnp.sum(s, axis=-1, keepdims=True) + sc * jnp.sum(qh * n_scr[h], axis=-1, keepdims=True)
        hout = num / jnp.maximum(jnp.abs(den), jnp.exp(-m_row))

        kw_log = gtot - a_c + i_c
        m_new = jnp.maximum(gtot + m_old, jnp.max(kw_log, axis=0, keepdims=True))
        kw = jnp.exp(kw_log - m_new)
        decay = jnp.exp(gtot + m_old - m_new)
        kk = kh * kw
        c_scr[h] = decay * c_scr[h] + _dot(jnp.transpose(kk).astype(BF16), vh)
        n_scr[h] = decay * n_scr[h] + jnp.sum(kk, axis=0, keepdims=True)
        m_scr[h] = jnp.broadcast_to(m_new, (1, LANES))

        hn = hout * lax.rsqrt(jnp.mean(hout * hout, axis=-1, keepdims=True) + EPS) * ng_ref[:, sl]
        y = jax.nn.sigmoid(o_ref[:, sl].astype(F32)) * hn * _silu(z_ref[:, sl].astype(F32))
        out_ref[:, sl] = y.astype(BF16)


def _mlstm_branch(y_main, gates, bsz, seq, conv_w, conv_b, b_i, b_f, norm_g):
    lc = ML_CHUNK
    nc = seq // lc
    bif = jnp.zeros((1, LANES), F32).at[0, 0:ML_HEADS].set(b_i.astype(F32))
    bif = bif.at[0, ML_HEADS:2 * ML_HEADS].set(b_f.astype(F32))

    def col(cb):
        return pl.BlockSpec((lc, D_BRANCH), lambda b, c: (b * nc + c, cb))

    def const(shape):
        return pl.BlockSpec(shape, lambda b, c: (0,) * len(shape))

    return pl.pallas_call(
        _mlstm_kernel,
        grid=(bsz, nc),
        in_specs=[col(ML_BLK0), col(ML_BLK0 + 1), col(ML_BLK0 + 2), col(ML_BLK0 + 3), col(ML_BLK0 + 4),
                  pl.BlockSpec((lc, LANES), lambda b, c: (b * nc + c, 0)),
                  const((ML_CONV, 2 * D_BRANCH)), const((1, 2 * D_BRANCH)), const((1, LANES)),
                  const((1, D_BRANCH))],
        out_specs=pl.BlockSpec((lc, D_BRANCH), lambda b, c: (b * nc + c, 0)),
        out_shape=jax.ShapeDtypeStruct((bsz * seq, D_BRANCH), BF16),
        scratch_shapes=[pltpu.VMEM((lc + 8, 2 * D_BRANCH), F32),
                        pltpu.VMEM((ML_HEADS, ML_HEAD_DIM, ML_HEAD_DIM), F32),
                        pltpu.VMEM((ML_HEADS, 1, ML_HEAD_DIM), F32),
                        pltpu.VMEM((ML_HEADS, 1, LANES), F32)],
        compiler_params=_params(("parallel", "arbitrary")),
        name="mlstm",
    )(y_main, y_main, y_main, y_main, y_main, gates,
      conv_w.astype(F32), conv_b.reshape(1, 2 * D_BRANCH).astype(F32), bif,
      norm_g.reshape(1, D_BRANCH).astype(F32))


def _rope_tables(seq):
    half = DA_HEAD_DIM // 2
    inv = 1.0 / (ROPE_THETA ** (jnp.arange(0, DA_HEAD_DIM, 2, dtype=F32) / DA_HEAD_DIM))
    ang = jnp.arange(seq, dtype=F32)[:, None] * inv[None, :]
    cos, sin = jnp.cos(ang), jnp.sin(ang)
    cos_t = jnp.tile(jnp.concatenate([cos, cos], axis=1), (1, LANES // DA_HEAD_DIM))
    sin_t = jnp.tile(jnp.concatenate([-sin, sin], axis=1), (1, LANES // DA_HEAD_DIM))
    del half
    return cos_t, sin_t


def _rope_kernel(q_ref, k_ref, cos_ref, sin_ref, q1_ref, q2_ref, kr_ref):
    rep = D_BRANCH // LANES
    cos = jnp.concatenate([cos_ref[...]] * rep, axis=1)
    sin = jnp.concatenate([sin_ref[...]] * rep, axis=1)
    lane = lax.broadcasted_iota(jnp.int32, (1, D_BRANCH), 1)
    first_half = (lane % DA_HEAD_DIM) < (DA_HEAD_DIM // 2)
    comp0 = (lane % DA_V_DIM) < DA_HEAD_DIM
    half = DA_HEAD_DIM // 2

    def rot(x):
        partner = jnp.where(first_half, pltpu.roll(x, D_BRANCH - half, 1), pltpu.roll(x, half, 1))
        return x * cos + partner * sin

    q = rot(q_ref[...].astype(F32)) * (DA_HEAD_DIM ** -0.5 * LOG2E)
    q1_ref[...] = jnp.where(comp0, q, 0.0).astype(BF16)
    q2_ref[...] = jnp.where(comp0, 0.0, q).astype(BF16)
    kr_ref[...] = rot(k_ref[...].astype(F32)).astype(BF16)


def _diff_attn_kernel(q1_ref, q2_ref, k_ref, v_ref, z_ref, lq1_ref, lk1_ref, lq2_ref, lk2_ref, g_ref,
                      out_ref, vext_scr, m_scr, acc_scr, *, lambda_init, tq, rg):
    qi = pl.program_id(2)
    q_refs = (q1_ref, q2_ref)

    @pl.when(qi == 0)
    def _():
        lane = lax.broadcasted_iota(jnp.int32, (v_ref.shape[0], DA_V_DIM), 1)
        vext_scr[:, 0:DA_V_DIM] = v_ref[...]
        vext_scr[:, DA_V_DIM:2 * DA_V_DIM] = jnp.where(lane == 0, 1.0, 0.0).astype(BF16)

    m_scr[...] = jnp.full_like(m_scr, -jnp.inf)
    acc_scr[...] = jnp.zeros_like(acc_scr)

    def block(j, masked):
        keys = pl.ds(pl.multiple_of(j * tq, tq), tq)
        kb = k_ref[keys, :]
        vb = vext_scr[keys, :]
        for g in range(tq // rg):
            rows = slice(g * rg, (g + 1) * rg)
            if masked:
                row = lax.broadcasted_iota(jnp.int32, (rg, tq), 0) + g * rg
                col = lax.broadcasted_iota(jnp.int32, (rg, tq), 1)
                keep = col <= row
            for c in range(2):
                s = _dot_nt(q_refs[c][rows, :], kb)
                if masked:
                    s = jnp.where(keep, s, -jnp.inf)
                m_prev = m_scr[c, rows, :]
                m_new = jnp.maximum(m_prev, jnp.max(s, axis=-1, keepdims=True))
                alpha = jnp.exp2(m_prev - m_new)
                p = jnp.exp2(s - m_new[:, 0:1])
                pv = _dot(p.astype(BF16), vb)
                acc_scr[c, rows, 0:DA_V_DIM] = alpha * acc_scr[c, rows, 0:DA_V_DIM] + pv[:, 0:DA_V_DIM]
                acc_scr[c, rows, DA_V_DIM:2 * DA_V_DIM] = (alpha * acc_scr[c, rows, DA_V_DIM:2 * DA_V_DIM]
                                                           + pv[:, DA_V_DIM:2 * DA_V_DIM])
                m_scr[c, rows, :] = m_new

    def body(j, carry):
        block(j, False)
        return carry

    lax.fori_loop(0, qi, body, 0)
    block(qi, True)

    lam = (jnp.exp(jnp.sum(lq1_ref[...] * lk1_ref[...], axis=-1, keepdims=True))
           - jnp.exp(jnp.sum(lq2_ref[...] * lk2_ref[...], axis=-1, keepdims=True)) + lambda_init)
    o1 = acc_scr[0, :, 0:DA_V_DIM] / acc_scr[0, :, DA_V_DIM:DA_V_DIM + 1]
    o2 = acc_scr[1, :, 0:DA_V_DIM] / acc_scr[1, :, DA_V_DIM:DA_V_DIM + 1]
    o = o1 - lam * o2
    o = o * lax.rsqrt(jnp.mean(o * o, axis=-1, keepdims=True) + EPS) * g_ref[...]
    o = o * (1.0 - lambda_init) * _silu(z_ref[...].astype(F32))
    out_ref[...] = o.astype(BF16)


def _diff_attn_branch(y_main, bsz, seq, cos_t, sin_t, lq1, lk1, lq2, lk2, subln_g, lambda_init, tq, rg, rope_rows):
    ntok = bsz * seq
    tr = min(rope_rows, seq)
    nrb = seq // tr
    q1, q2, kr = pl.pallas_call(
        _rope_kernel,
        grid=(bsz, nrb),
        in_specs=[pl.BlockSpec((tr, D_BRANCH), lambda b, r: (b * nrb + r, DA_BLK0)),
                  pl.BlockSpec((tr, D_BRANCH), lambda b, r: (b * nrb + r, DA_BLK0 + 1)),
                  pl.BlockSpec((tr, LANES), lambda b, r: (r, 0)),
                  pl.BlockSpec((tr, LANES), lambda b, r: (r, 0))],
        out_specs=[pl.BlockSpec((tr, D_BRANCH), lambda b, r: (b * nrb + r, 0))] * 3,
        out_shape=[jax.ShapeDtypeStruct((ntok, D_BRANCH), BF16)] * 3,
        compiler_params=_params(("parallel", "parallel")),
        name="rope",
    )(y_main, y_main, cos_t, sin_t)

    tq = min(tq, seq)
    nq = seq // tq
    vblk0 = (DA_BLK0 + 2) * D_BRANCH // DA_V_DIM
    zblk0 = (DA_BLK0 + 3) * D_BRANCH // DA_V_DIM

    def qspec():
        return pl.BlockSpec((tq, DA_V_DIM), lambda b, h, i: (b * nq + i, h))

    def kspec(blk0):
        return pl.BlockSpec((seq, DA_V_DIM), lambda b, h, i: (b, blk0 + h))

    def vec(n):
        return pl.BlockSpec((1, n), lambda b, h, i: (0, 0))

    lvec = [a.reshape(1, DA_HEAD_DIM).astype(F32) for a in (lq1, lk1, lq2, lk2)]
    return pl.pallas_call(
        functools.partial(_diff_attn_kernel, lambda_init=lambda_init, tq=tq, rg=min(rg, tq)),
        grid=(bsz, DA_HEADS, nq),
        in_specs=[qspec(), qspec(), kspec(0), kspec(vblk0),
                  pl.BlockSpec((tq, DA_V_DIM), lambda b, h, i: (b * nq + i, zblk0 + h)),
                  vec(DA_HEAD_DIM), vec(DA_HEAD_DIM), vec(DA_HEAD_DIM), vec(DA_HEAD_DIM), vec(DA_V_DIM)],
        out_specs=pl.BlockSpec((tq, DA_V_DIM), lambda b, h, i: (b * nq + i, h)),
        out_shape=jax.ShapeDtypeStruct((ntok, D_BRANCH), BF16),
        scratch_shapes=[pltpu.VMEM((seq, 2 * DA_V_DIM), BF16),
                        pltpu.VMEM((2, tq, LANES), F32), pltpu.VMEM((2, tq, 2 * DA_V_DIM), F32)],
        compiler_params=_params(("parallel", "parallel", "arbitrary")),
        name="diff_attn",
    )(q1, q2, kr, y_main, y_main, *lvec, subln_g.reshape(1, DA_V_DIM).astype(F32))


def _mem_attn_kernel(q_ref, z_ref, k_ref, v_ref, out_ref):
    dh = XA_HEAD_DIM
    for h in range(XA_HEADS):
        sl = slice(h * dh, (h + 1) * dh)
        s = _dot_nt(q_ref[:, sl], k_ref[:, sl]) * (dh ** -0.5)
        p = jnp.exp(s - jnp.max(s, axis=-1, keepdims=True))
        p = p / jnp.sum(p, axis=-1, keepdims=True)
        o = _dot(p.astype(BF16), v_ref[:, sl])
        out_ref[:, sl] = (o * _silu(z_ref[:, sl].astype(F32))).astype(BF16)


def _mem_attn_branch(y_main, kv, bsz, seq, mem_len, tq):
    tq = min(tq, seq)
    nq = seq // tq
    return pl.pallas_call(
        _mem_attn_kernel,
        grid=(bsz, nq),
        in_specs=[pl.BlockSpec((tq, D_BRANCH), lambda b, i: (b * nq + i, XA_BLK0)),
                  pl.BlockSpec((tq, D_BRANCH), lambda b, i: (b * nq + i, XA_BLK0 + 1)),
                  pl.BlockSpec((mem_len, D_BRANCH), lambda b, i: (b, 0)),
                  pl.BlockSpec((mem_len, D_BRANCH), lambda b, i: (b, 1))],
        out_specs=pl.BlockSpec((tq, D_BRANCH), lambda b, i: (b * nq + i, 0)),
        out_shape=jax.ShapeDtypeStruct((bsz * seq, D_BRANCH), BF16),
        compiler_params=_params(("parallel", "parallel")),
        name="mem_attn",
    )(y_main, y_main, kv, kv)


def _merge_kernel(*refs):
    y_refs = refs[0:N_BRANCH]
    g_refs = refs[N_BRANCH:2 * N_BRANCH]
    w_refs = refs[2 * N_BRANCH:3 * N_BRANCH]
    out_ref, y0_scr, perm_scr = refs[3 * N_BRANCH:]

    @pl.when(pl.program_id(1) == 0)
    def _():
        rows = y_refs[0].shape[1]
        nslab = D_BRANCH // LANES
        for s in range(S5_T):
            ys = y_refs[0][s].astype(F32)
            for c in range(nslab):
                perm_scr[c, pl.ds(s, rows, stride=S5_T), :] = ys[:, c * LANES:(c + 1) * LANES]
        for c in range(nslab):
            y0_scr[:, c * LANES:(c + 1) * LANES] = perm_scr[c].astype(BF16)

    acc = jax.nn.sigmoid(g_refs[0][...].astype(F32)) * _dot(y0_scr[...], w_refs[0][0])
    for y_ref, g_ref, w_ref in zip(y_refs[1:], g_refs[1:], w_refs[1:]):
        acc = acc + jax.nn.sigmoid(g_ref[...].astype(F32)) * _dot(y_ref[...], w_ref[0])
    out_ref[...] = acc.astype(BF16)


def _merge(branches, y_main, w_branch, tm, tn):
    ntok = y_main.shape[0]
    tm = min(tm, ntok)
    ncol = D_MODEL // tn
    y_specs = ([pl.BlockSpec((S5_T, tm // S5_T, D_BRANCH), lambda i, n: (0, i, 0))]
               + [pl.BlockSpec((tm, D_BRANCH), lambda i, n: (i, 0))] * (N_BRANCH - 1))
    g_specs = [pl.BlockSpec((tm, tn), lambda i, n, b=b: (i, (GATE_COL0 + b * D_MODEL) // tn + n))
               for b in range(N_BRANCH)]
    w_specs = [pl.BlockSpec((1, D_BRANCH, tn), lambda i, n, b=b: (b, 0, n)) for b in range(N_BRANCH)]
    return pl.pallas_call(
        _merge_kernel,
        grid=(ntok // tm, ncol),
        in_specs=y_specs + g_specs + w_specs,
        out_specs=pl.BlockSpec((tm, tn), lambda i, n: (i, n)),
        out_shape=jax.ShapeDtypeStruct((ntok, D_MODEL), BF16),
        scratch_shapes=[pltpu.VMEM((tm, D_BRANCH), BF16), pltpu.VMEM((D_BRANCH // LANES, tm, LANES), F32)],
        compiler_params=_params(("parallel", "arbitrary")),
        name="merge",
    )(*branches, *([y_main] * N_BRANCH), *([w_branch] * N_BRANCH))


def _out_proj_kernel(m_ref, w_ref, g_ref, x_ref, out_ref):
    o = _dot(m_ref[...], w_ref[...])
    o = o * lax.rsqrt(jnp.mean(o * o, axis=-1, keepdims=True) + EPS) * g_ref[...]
    out_ref[...] = x_ref[...] + o


def _out_proj(merged, w_out, g_post, x, tm):
    ntok = x.shape[0]
    tm = min(tm, ntok)
    return pl.pallas_call(
        _out_proj_kernel,
        grid=(ntok // tm,),
        in_specs=[pl.BlockSpec((tm, D_MODEL), lambda i: (i, 0)),
                  pl.BlockSpec((D_MODEL, D_MODEL), lambda i: (0, 0)),
                  pl.BlockSpec((1, D_MODEL), lambda i: (0, 0)),
                  pl.BlockSpec((tm, D_MODEL), lambda i: (i, 0))],
        out_specs=pl.BlockSpec((tm, D_MODEL), lambda i: (i, 0)),
        out_shape=jax.ShapeDtypeStruct((ntok, D_MODEL), F32),
        compiler_params=_params(("parallel",)),
        name="out_proj",
    )(merged, w_out, g_post.reshape(1, D_MODEL).astype(F32), x)


def kernel(x, mem, g_pre, w_in, s5_lam_re, s5_lam_im, s5_log_dt, s5_b_re, s5_b_im, s5_c_re, s5_c_im, s5_d, s5_w_glu, s5_b_glu, ml_conv_w, ml_conv_b, ml_b_i, ml_b_f, ml_norm_g, da_lq1, da_lk1, da_lq2, da_lk2, da_subln_g, g_mem, xa_w_kv, w_branch, w_out, g_post):
    bsz, seq, _ = x.shape
    mem_len = mem.shape[1]
    depth = w_in.shape[0]
    ntok = bsz * seq
    cos_t, sin_t = _rope_tables(seq)
    xf = x.reshape(ntok, D_MODEL).astype(F32)
    memf = mem.reshape(bsz * mem_len, D_MODEL).astype(F32)
    gate0 = 7 * D_BRANCH
    expanders = _s5_expanders()
    for l in range(depth):
        lambda_init = 0.8 - 0.6 * math.exp(-0.3 * l)
        w_l = w_in[l]
        w_main = jnp.concatenate([w_l[:, :gate0], w_l[:, gate0 + 2 * ML_HEADS:]], axis=1).astype(BF16)
        w_gate = jnp.pad(w_l[:, gate0:gate0 + 2 * ML_HEADS], ((0, 0), (0, LANES - 2 * ML_HEADS))).astype(BF16)
        y_main, s5_in, gates = _in_proj(xf, g_pre[l], w_main, w_gate, tm=1024, tn=1024)
        kv = _norm_proj(memf, g_mem[l], xa_w_kv[l].astype(BF16), tm=1024, tn=1024)

        ops = _s5_operators(s5_lam_re[l], s5_lam_im[l], s5_log_dt[l], s5_b_re[l], s5_b_im[l],
                            s5_c_re[l], s5_c_im[l])
        y_s5 = _s5_branch(s5_in, bsz, seq, ops, expanders, s5_d[l].astype(F32), s5_w_glu[l], s5_b_glu[l], rt=512)
        y_ml = _mlstm_branch(y_main, gates, bsz, seq, ml_conv_w[l], ml_conv_b[l], ml_b_i[l], ml_b_f[l],
                             ml_norm_g[l])
        y_da = _diff_attn_branch(y_main, bsz, seq, cos_t, sin_t, da_lq1[l], da_lk1[l], da_lq2[l], da_lk2[l],
                                 da_subln_g[l], lambda_init, tq=1024, rg=128, rope_rows=512)
        y_xa = _mem_attn_branch(y_main, kv, bsz, seq, mem_len, tq=512)

        merged = _merge((y_s5, y_ml, y_da, y_xa), y_main, w_branch[l].astype(BF16), tm=1024, tn=512)
        xf = _out_proj(merged, w_out[l].astype(BF16), g_post[l], xf, tm=512)
    return xf.reshape(bsz, seq, D_MODEL).astype(x.dtype)
```

```python
import functools
import math

import jax
import jax.numpy as jnp
from jax import lax
from jax.experimental import pallas as pl
from jax.experimental.pallas import tpu as pltpu

F32 = jnp.float32
BF16 = jnp.bfloat16

D_MODEL = 2048
D_BRANCH = 1024
N_BRANCH = 4
S5_GROUP = 16
S5_GROUPS = D_BRANCH // S5_GROUP
S5_STATE = 64
ML_HEADS = 4
ML_HEAD_DIM = D_BRANCH // ML_HEADS
ML_CHUNK = 128
ML_CONV = 4
DA_HEADS = 8
DA_HEAD_DIM = 64
DA_V_DIM = 2 * DA_HEAD_DIM
XA_HEADS = 4
XA_HEAD_DIM = D_BRANCH // XA_HEADS
ROPE_THETA = 10000.0
EPS = 1e-6
LOG2E = 1.4426950408889634

LANES = 128
S5_T = 16
S5_LANE_GROUPS = LANES // S5_GROUP
S5_NBLK = D_BRANCH // LANES
S5_SBLK = S5_LANE_GROUPS * S5_STATE
S5_SLAB = S5_SBLK // LANES
N_S5IN = 2 * D_BRANCH
N_MAIN = 11 * D_BRANCH + N_BRANCH * D_MODEL
ML_BLK0, DA_BLK0, XA_BLK0 = 0, 5, 9
GATE_COL0 = 11 * D_BRANCH
MIB = 1024 * 1024
VMEM_LIMIT = 48 * MIB


def _params(sem, vmem=VMEM_LIMIT):
    return pltpu.CompilerParams(dimension_semantics=sem, vmem_limit_bytes=vmem)


def _silu(x):
    return x * jax.nn.sigmoid(x)


def _dot(a, b):
    return jnp.dot(a, b, preferred_element_type=F32)


def _dot_nt(a, b):
    return lax.dot_general(a, b, (((1,), (1,)), ((), ())), preferred_element_type=F32)


def _rms_norm_bf16(x, g):
    ms = jnp.mean(x * x, axis=-1, keepdims=True)
    return (x * lax.rsqrt(ms + EPS) * g).astype(BF16)


def _norm_proj_kernel(x_ref, g_ref, w_ref, y_ref, h_scr):
    @pl.when(pl.program_id(1) == 0)
    def _():
        h_scr[...] = _rms_norm_bf16(x_ref[...], g_ref[...])

    y_ref[...] = _dot(h_scr[...], w_ref[...]).astype(y_ref.dtype)


def _norm_proj(x, g, w, tm, tn):
    m, k = x.shape
    n = w.shape[1]
    tm = min(tm, m)
    return pl.pallas_call(
        _norm_proj_kernel,
        grid=(m // tm, n // tn),
        in_specs=[pl.BlockSpec((tm, k), lambda i, j: (i, 0)),
                  pl.BlockSpec((1, k), lambda i, j: (0, 0)),
                  pl.BlockSpec((k, tn), lambda i, j: (0, j))],
        out_specs=pl.BlockSpec((tm, tn), lambda i, j: (i, j)),
        out_shape=jax.ShapeDtypeStruct((m, n), BF16),
        scratch_shapes=[pltpu.VMEM((tm, k), BF16)],
        compiler_params=_params(("parallel", "arbitrary")),
        name="norm_proj",
    )(x, g.reshape(1, k), w)


def _in_proj_kernel(x_ref, g_ref, w_ref, wa_ref, y_ref, s5_ref, a_ref, h_scr, perm_scr):
    j = pl.program_id(1)
    tm, tn = y_ref.shape
    n_s5 = N_S5IN // tn

    @pl.when(j == 0)
    def _():
        h = _rms_norm_bf16(x_ref[...], g_ref[...])
        h_scr[...] = h
        a_ref[...] = _dot(h, wa_ref[...])

    @pl.when(j < n_s5)
    def _():
        acc = _dot(h_scr[...], w_ref[...])
        for c in range(tn // LANES):
            perm_scr[c] = acc[:, c * LANES:(c + 1) * LANES]
        for s in range(S5_T):
            for c in range(tn // LANES):
                rows = perm_scr[c, pl.ds(s, tm // S5_T, stride=S5_T), :]
                s5_ref[s, :, c * LANES:(c + 1) * LANES] = rows.astype(BF16)

    @pl.when(j >= n_s5)
    def _():
        y_ref[...] = _dot(h_scr[...], w_ref[...]).astype(BF16)


def _in_proj(x, g, w, w_aux, tm, tn):
    m, k = x.shape
    tm = min(tm, m)
    n_s5 = N_S5IN // tn
    return pl.pallas_call(
        _in_proj_kernel,
        grid=(m // tm, (N_S5IN + N_MAIN) // tn),
        in_specs=[pl.BlockSpec((tm, k), lambda i, j: (i, 0)),
                  pl.BlockSpec((1, k), lambda i, j: (0, 0)),
                  pl.BlockSpec((k, tn), lambda i, j: (0, j)),
                  pl.BlockSpec((k, LANES), lambda i, j: (0, 0))],
        out_specs=[pl.BlockSpec((tm, tn), lambda i, j: (i, jnp.maximum(j - n_s5, 0))),
                   pl.BlockSpec((S5_T, tm // S5_T, tn), lambda i, j: (0, i, jnp.minimum(j, n_s5 - 1))),
                   pl.BlockSpec((tm, LANES), lambda i, j: (i, 0))],
        out_shape=[jax.ShapeDtypeStruct((m, N_MAIN), BF16),
                   jax.ShapeDtypeStruct((S5_T, m // S5_T, N_S5IN), BF16),
                   jax.ShapeDtypeStruct((m, LANES), F32)],
        scratch_shapes=[pltpu.VMEM((tm, k), BF16), pltpu.VMEM((tn // LANES, tm, LANES), F32)],
        compiler_params=_params(("parallel", "arbitrary"), 56 * MIB),
        name="in_proj",
    )(x, g.reshape(1, k), w, w_aux)


def _s5_operators(lam_re, lam_im, log_dt, b_re, b_im, c_re, c_im):
    hi = lax.Precision.HIGHEST
    G, P, C, T, J, LG = S5_GROUPS, S5_STATE, S5_GROUP, S5_T, S5_NBLK, S5_LANE_GROUPS
    lam = lax.complex(jnp.minimum(lam_re.astype(F32), -1e-4), lam_im.astype(F32))
    dt = jnp.exp(log_dt.astype(F32))[:, None]
    z = lam * dt
    lam_bar = jnp.exp(z)
    b_bar = ((lam_bar - 1.0) / lam)[..., None] * lax.complex(b_re.astype(F32), b_im.astype(F32))
    c = lax.complex(c_re.astype(F32), c_im.astype(F32))
    steps = jnp.arange(T + 1, dtype=F32)
    pw = jnp.exp(z[None] * steps[:, None, None].astype(jnp.complex64))

    kern = jnp.real(jnp.einsum('gop,kgp,gpi->kgoi', c, pw[:T], b_bar, precision=hi))
    s_idx = jnp.arange(T)[:, None]
    t_idx = jnp.arange(T)[None, :]
    lag = jnp.clip(t_idx - s_idx, 0, T - 1)
    toe = kern[lag] * (t_idx >= s_idx)[:, :, None, None, None].astype(F32)
    toe = toe.reshape(T, T, J, LG, C, C).transpose(2, 0, 3, 5, 1, 4)
    a2 = toe.reshape(J, T * LANES, T * C).astype(BF16)

    bs = pw[T - 1 - jnp.arange(T)][..., None] * b_bar[None]
    bs = jnp.stack([jnp.real(bs), jnp.imag(bs)], axis=0)
    bs = bs.reshape(2, T, J, LG, P, C).transpose(2, 1, 3, 5, 0, 4)
    b2 = bs.reshape(J, T * LANES, 2 * P).astype(BF16)

    cs = c[None] * pw[1:T + 1][:, :, None, :]
    cs = jnp.stack([jnp.real(cs), -jnp.imag(cs)], axis=0)
    cs = cs.reshape(2, T, J, LG, C, P).transpose(2, 0, 3, 5, 1, 4)
    c2 = cs.reshape(J, 2 * S5_SBLK, T * C).astype(BF16)

    lam_t = pw[T].reshape(1, G * P)
    return a2, b2, c2, jnp.real(lam_t), jnp.imag(lam_t)


def _s5_expanders():
    r = jnp.arange(S5_T * S5_GROUP)[:, None]
    c = jnp.arange(S5_T * LANES)[None, :]
    e_to = ((r // S5_GROUP == c // LANES) & (r % S5_GROUP == c % S5_GROUP)).astype(BF16)
    r = jnp.arange(2 * S5_STATE)[:, None]
    c = jnp.arange(2 * S5_SBLK)[None, :]
    e_rp = ((r // S5_STATE == c // S5_SBLK) & (r % S5_STATE == c % S5_STATE)).astype(BF16)
    return e_to, e_rp


def _expand_block_diag(dst_scr, compact, exp_ref, row_group, col_group):
    rows = compact.shape[0]
    cols = exp_ref.shape[1]
    slab = 2 * LANES
    rg = (lax.broadcasted_iota(jnp.int32, (rows, slab), 0) // row_group) % S5_LANE_GROUPS
    cl = lax.broadcasted_iota(jnp.int32, (rows, slab), 1)
    for c0 in range(0, cols, slab):
        cg = ((cl + c0) // col_group) % S5_LANE_GROUPS
        blk = _dot(compact, exp_ref[:, c0:c0 + slab])
        dst_scr[:, c0:c0 + slab] = jnp.where(rg == cg, blk, 0.0).astype(BF16)


def _s5_local_kernel(u_ref, b2_ref, erp_ref, s_ref, min_scr):
    @pl.when(pl.program_id(1) == 0)
    def _():
        _expand_block_diag(min_scr, b2_ref[0], erp_ref, S5_GROUP, S5_STATE)

    ucat = jnp.concatenate([u_ref[s] for s in range(S5_T)], axis=1)
    s = _dot(ucat, min_scr[...])
    for c in range(2 * S5_SLAB):
        s_ref[c] = s[:, c * LANES:(c + 1) * LANES]


def _s5_scan_kernel(sr_ref, si_ref, ar_ref, ai_ref, xr_ref, xi_ref, *, bsz, nchunk):
    ar = [jnp.broadcast_to(ar_ref[:, c * LANES:(c + 1) * LANES], (bsz, LANES)) for c in range(S5_SLAB)]
    ai = [jnp.broadcast_to(ai_ref[:, c * LANES:(c + 1) * LANES], (bsz, LANES)) for c in range(S5_SLAB)]

    def step(k, carry):
        rows = pl.ds(k, bsz, stride=nchunk)
        new = []
        for c in range(S5_SLAB):
            xr, xi = carry[c]
            xr_ref[c, rows, :] = xr
            xi_ref[c, rows, :] = xi
            nr = ar[c] * xr - ai[c] * xi + sr_ref[c, rows, :]
            ni = ar[c] * xi + ai[c] * xr + si_ref[c, rows, :]
            new.append((nr, ni))
        return tuple(new)

    zero = jnp.zeros((bsz, LANES), F32)
    lax.fori_loop(0, nchunk, step, tuple((zero, zero) for _ in range(S5_SLAB)), unroll=4)


def _s5_out_kernel(u_ref, a2_ref, c2_ref, eto_ref, xr_ref, xi_ref, d_ref, out_ref, mi_scr, mo_scr):
    @pl.when(pl.program_id(1) == 0)
    def _():
        _expand_block_diag(mi_scr, a2_ref[0], eto_ref, S5_GROUP, S5_GROUP)
        _expand_block_diag(mo_scr, c2_ref[0], eto_ref, S5_STATE, S5_GROUP)

    us = [u_ref[s] for s in range(S5_T)]
    ucat = jnp.concatenate(us, axis=1)
    xcat = jnp.concatenate([xr_ref[c] for c in range(S5_SLAB)] + [xi_ref[c] for c in range(S5_SLAB)],
                           axis=1).astype(BF16)
    d = d_ref[...]
    tile = 2 * LANES
    for c0 in range(0, S5_T * LANES, tile):
        y = _dot(ucat[:, :c0 + tile], mi_scr[:c0 + tile, c0:c0 + tile]) + _dot(xcat, mo_scr[:, c0:c0 + tile])
        for t in range(c0 // LANES, (c0 + tile) // LANES):
            yt = y[:, t * LANES - c0:(t + 1) * LANES - c0] + d * us[t].astype(F32)
            out_ref[t] = jax.nn.gelu(yt).astype(BF16)


def _s5_glu_kernel(g_ref, z_ref, w_ref, b_ref, out_ref):
    g = g_ref[...]
    a = _dot(g, w_ref[...]) + b_ref[...]
    out_ref[...] = (g.astype(F32) * jax.nn.sigmoid(a) * _silu(z_ref[...].astype(F32))).astype(BF16)


def _s5_branch(s5_in, bsz, seq, ops, layer, expanders, d_skip, w_glu, b_glu, rt):
    a2, b2, c2, lam_r, lam_i = ops
    e_to, e_rp = expanders
    nrow = bsz * seq // S5_T
    nchunk = seq // S5_T
    rt = min(rt, nrow)
    u_spec = pl.BlockSpec((S5_T, rt, LANES), lambda j, r: (0, r, j))

    def whole(a):
        return pl.BlockSpec(a.shape, lambda j, r: (0,) * a.ndim)

    def per_blk(a):
        return pl.BlockSpec((pl.Squeezed(), 1) + a.shape[2:], lambda j, r: (layer, j, 0, 0))

    s_loc = pl.pallas_call(
        _s5_local_kernel,
        grid=(S5_NBLK, nrow // rt),
        in_specs=[u_spec, per_blk(b2), whole(e_rp)],
        out_specs=pl.BlockSpec((2 * S5_SLAB, rt, LANES), lambda j, r: (j, r, 0)),
        out_shape=jax.ShapeDtypeStruct((S5_NBLK * 2 * S5_SLAB, nrow, LANES), F32),
        scratch_shapes=[pltpu.VMEM((S5_T * LANES, 2 * S5_SBLK), BF16)],
        compiler_params=_params(("parallel", "arbitrary")),
        name="s5_local",
    )(s5_in, b2, e_rp)

    xr, xi = pl.pallas_call(
        functools.partial(_s5_scan_kernel, bsz=bsz, nchunk=nchunk),
        grid=(S5_NBLK,),
        in_specs=[pl.BlockSpec((S5_SLAB, nrow, LANES), lambda j: (2 * j, 0, 0)),
                  pl.BlockSpec((S5_SLAB, nrow, LANES), lambda j: (2 * j + 1, 0, 0)),
                  pl.BlockSpec((pl.Squeezed(), 1, S5_SBLK), lambda j: (layer, 0, j)),
                  pl.BlockSpec((pl.Squeezed(), 1, S5_SBLK), lambda j: (layer, 0, j))],
        out_specs=[pl.BlockSpec((S5_SLAB, nrow, LANES), lambda j: (j, 0, 0))] * 2,
        out_shape=[jax.ShapeDtypeStruct((S5_NBLK * S5_SLAB, nrow, LANES), F32)] * 2,
        compiler_params=_params(("parallel",)),
        name="s5_scan",
    )(s_loc, s_loc, lam_r, lam_i)

    gel = pl.pallas_call(
        _s5_out_kernel,
        grid=(S5_NBLK, nrow // rt),
        in_specs=[u_spec, per_blk(a2), per_blk(c2), whole(e_to),
                  pl.BlockSpec((S5_SLAB, rt, LANES), lambda j, r: (j, r, 0)),
                  pl.BlockSpec((S5_SLAB, rt, LANES), lambda j, r: (j, r, 0)),
                  pl.BlockSpec((1, LANES), lambda j, r: (0, j))],
        out_specs=pl.BlockSpec((S5_T, rt, LANES), lambda j, r: (0, r, j)),
        out_shape=jax.ShapeDtypeStruct((S5_T, nrow, D_BRANCH), BF16),
        scratch_shapes=[pltpu.VMEM((S5_T * LANES, S5_T * LANES), BF16),
                        pltpu.VMEM((2 * S5_SBLK, S5_T * LANES), BF16)],
        compiler_params=_params(("parallel", "arbitrary")),
        name="s5_out",
    )(s5_in, a2, c2, e_to, xr, xi, d_skip.reshape(1, D_BRANCH))

    row_blk = pl.BlockSpec((pl.Squeezed(), rt, D_BRANCH), lambda t, r: (t, r, 0))
    return pl.pallas_call(
        _s5_glu_kernel,
        grid=(S5_T, nrow // rt),
        in_specs=[row_blk,
                  pl.BlockSpec((pl.Squeezed(), rt, D_BRANCH), lambda t, r: (t, r, 1)),
                  pl.BlockSpec((D_BRANCH, D_BRANCH), lambda t, r: (0, 0)),
                  pl.BlockSpec((1, D_BRANCH), lambda t, r: (0, 0))],
        out_specs=row_blk,
        out_shape=jax.ShapeDtypeStruct((S5_T, nrow, D_BRANCH), BF16),
        compiler_params=_params(("parallel", "parallel")),
        name="s5_glu",
    )(gel, s5_in, w_glu.astype(BF16), b_glu.reshape(1, D_BRANCH).astype(F32))


def _split3(x):
    hi = x.astype(BF16)
    r1 = x - hi.astype(F32)
    mid = r1.astype(BF16)
    lo = (r1 - mid.astype(F32)).astype(BF16)
    return hi, mid, lo


def _mlstm_kernel(q_ref, k_ref, v_ref, o_ref, z_ref, if_ref, cw_ref, cb_ref, bif_ref, ng_ref, out_ref,
                  ext_scr, c_scr, n_scr, m_scr):
    lc, dh, nh = ML_CHUNK, ML_HEAD_DIM, ML_HEADS
    tail = 8
    cidx = pl.program_id(1)

    @pl.when(cidx == 0)
    def _():
        ext_scr[0:tail, :] = jnp.zeros((tail, 2 * D_BRANCH), F32)
        c_scr[...] = jnp.zeros_like(c_scr)
        n_scr[...] = jnp.zeros_like(n_scr)
        m_scr[...] = jnp.zeros_like(m_scr)

    @pl.when(cidx > 0)
    def _():
        ext_scr[0:tail, :] = ext_scr[lc:lc + tail, :]

    ext_scr[tail:tail + lc, 0:D_BRANCH] = q_ref[...].astype(F32)
    ext_scr[tail:tail + lc, D_BRANCH:2 * D_BRANCH] = k_ref[...].astype(F32)
    conv = jnp.broadcast_to(cb_ref[...], (lc, 2 * D_BRANCH))
    for w in range(ML_CONV):
        off = tail - (ML_CONV - 1) + w
        conv = conv + cw_ref[w:w + 1, :] * ext_scr[off:off + lc, :]
    qk = _silu(conv)

    gts = if_ref[...] + bif_ref[...]
    lf = jnp.minimum(gts, 0.0) - jnp.log(1.0 + jnp.exp(-jnp.abs(gts)))
    gts_t = jnp.transpose(gts)[0:16, :]
    lf_t = jnp.transpose(lf)[0:16, :]
    row = lax.broadcasted_iota(jnp.int32, (lc, lc), 0)
    col = lax.broadcasted_iota(jnp.int32, (lc, lc), 1)
    causal = col <= row
    tril = jnp.where(causal, 1.0, 0.0).astype(BF16)
    triu = jnp.where(row <= col, 1.0, 0.0).astype(BF16)
    acum_c = sum(_dot(tril, p) for p in _split3(lf))
    acum_r = sum(_dot(p, triu) for p in _split3(lf_t))

    for h in range(nh):
        sl = slice(h * dh, (h + 1) * dh)
        qh = qk[:, h * dh:(h + 1) * dh]
        kh = qk[:, D_BRANCH + h * dh:D_BRANCH + (h + 1) * dh] * (dh ** -0.5)
        vh = v_ref[:, sl]
        qb = qh.astype(BF16)
        a_c = acum_c[:, nh + h:nh + h + 1]
        a_r = acum_r[nh + h:nh + h + 1, :]
        i_c = gts[:, h:h + 1]
        i_r = gts_t[h:h + 1, :]
        gtot = a_c[lc - 1:lc, :]
        m_old = m_scr[h][:, 0:1]
        dmat = jnp.where(causal, a_c - a_r + i_r, -jnp.inf)
        inter = a_c + m_old
        m_row = jnp.maximum(jnp.max(dmat, axis=-1, keepdims=True), inter)
        s = _dot_nt(qb, kh.astype(BF16)) * jnp.exp(dmat - m_row)
        sc = jnp.exp(inter - m_row)
        num = _dot(s.astype(BF16), vh) + sc * _dot(qb, c_scr[h].astype(BF16))
        den = jnp.sum(s, axis=-1, keepdims=True) + sc * jnp.sum(qh * n_scr[h], axis=-1, keepdims=True)
        hout = num / jnp.maximum(jnp.abs(den), jnp.exp(-m_row))

        kw_log = gtot - a_c + i_c
        m_new = jnp.maximum(gtot + m_old, jnp.max(kw_log, axis=0, keepdims=True))
        kw = jnp.exp(kw_log - m_new)
        decay = jnp.exp(gtot + m_old - m_new)
        kk = kh * kw
        c_scr[h] = decay * c_scr[h] + _dot(jnp.transpose(kk).astype(BF16), vh)
        n_scr[h] = decay * n_scr[h] + jnp.sum(kk, axis=0, keepdims=True)
        m_scr[h] = jnp.broadcast_to(m_new, (1, LANES))

        hn = hout * lax.rsqrt(jnp.mean(hout * hout, axis=-1, keepdims=True) + EPS) * ng_ref[:, sl]
        y = jax.nn.sigmoid(o_ref[:, sl].astype(F32)) * hn * _silu(z_ref[:, sl].astype(F32))
        out_ref[:, sl] = y.astype(BF16)


def _mlstm_branch(y_main, gates, bsz, seq, conv_w, conv_b, b_i, b_f, norm_g):
    lc = ML_CHUNK
    nc = seq // lc
    bif = jnp.zeros((1, LANES), F32).at[0, 0:ML_HEADS].set(b_i.astype(F32))
    bif = bif.at[0, ML_HEADS:2 * ML_HEADS].set(b_f.astype(F32))

    def col(cb):
        return pl.BlockSpec((lc, D_BRANCH), lambda b, c: (b * nc + c, cb))

    def const(shape):
        return pl.BlockSpec(shape, lambda b, c: (0,) * len(shape))

    return pl.pallas_call(
        _mlstm_kernel,
        grid=(bsz, nc),
        in_specs=[col(ML_BLK0), col(ML_BLK0 + 1), col(ML_BLK0 + 2), col(ML_BLK0 + 3), col(ML_BLK0 + 4),
                  pl.BlockSpec((lc, LANES), lambda b, c: (b * nc + c, 0)),
                  const((ML_CONV, 2 * D_BRANCH)), const((1, 2 * D_BRANCH)), const((1, LANES)),
                  const((1, D_BRANCH))],
        out_specs=pl.BlockSpec((lc, D_BRANCH), lambda b, c: (b * nc + c, 0)),
        out_shape=jax.ShapeDtypeStruct((bsz * seq, D_BRANCH), BF16),
        scratch_shapes=[pltpu.VMEM((lc + 8, 2 * D_BRANCH), F32),
                        pltpu.VMEM((ML_HEADS, ML_HEAD_DIM, ML_HEAD_DIM), F32),
                        pltpu.VMEM((ML_HEADS, 1, ML_HEAD_DIM), F32),
                        pltpu.VMEM((ML_HEADS, 1, LANES), F32)],
        compiler_params=_params(("parallel", "arbitrary")),
        name="mlstm",
    )(y_main, y_main, y_main, y_main, y_main, gates,
      conv_w.astype(F32), conv_b.reshape(1, 2 * D_BRANCH).astype(F32), bif,
      norm_g.reshape(1, D_BRANCH).astype(F32))


def _rope_tables(seq):
    half = DA_HEAD_DIM // 2
    inv = 1.0 / (ROPE_THETA ** (jnp.arange(0, DA_HEAD_DIM, 2, dtype=F32) / DA_HEAD_DIM))
    ang = jnp.arange(seq, dtype=F32)[:, None] * inv[None, :]
    cos, sin = jnp.cos(ang), jnp.sin(ang)
    cos_t = jnp.tile(jnp.concatenate([cos, cos], axis=1), (1, LANES // DA_HEAD_DIM))
    sin_t = jnp.tile(jnp.concatenate([-sin, sin], axis=1), (1, LANES // DA_HEAD_DIM))
    del half
    return cos_t, sin_t


def _rope(x, cos, sin):
    half = DA_HEAD_DIM // 2
    lane = lax.broadcasted_iota(jnp.int32, (1, LANES), 1)
    first_half = (lane % DA_HEAD_DIM) < half
    partner = jnp.where(first_half, pltpu.roll(x, LANES - half, 1), pltpu.roll(x, half, 1))
    return x * cos + partner * sin


def _diff_attn_kernel(q_ref, k_ref, v_ref, z_ref, cosq_ref, sinq_ref, cosk_ref, sink_ref,
                      lq1_ref, lk1_ref, lq2_ref, lk2_ref, g_ref,
                      out_ref, vext_scr, kr_scr, q_scr, m_scr, acc_scr, *, lambda_init, tq, rg):
    qi = pl.program_id(2)

    @pl.when(qi == 0)
    def _():
        lane = lax.broadcasted_iota(jnp.int32, (v_ref.shape[0], DA_V_DIM), 1)
        vext_scr[:, 0:DA_V_DIM] = v_ref[...]
        vext_scr[:, DA_V_DIM:2 * DA_V_DIM] = jnp.where(lane == 0, 1.0, 0.0).astype(BF16)
        kr_scr[...] = _rope(k_ref[...].astype(F32), cosk_ref[...], sink_ref[...]).astype(BF16)

    q = _rope(q_ref[...].astype(F32), cosq_ref[...], sinq_ref[...]) * (DA_HEAD_DIM ** -0.5 * LOG2E)
    comp0 = lax.broadcasted_iota(jnp.int32, (1, DA_V_DIM), 1) < DA_HEAD_DIM
    q_scr[0] = jnp.where(comp0, q, 0.0).astype(BF16)
    q_scr[1] = jnp.where(comp0, 0.0, q).astype(BF16)

    m_scr[...] = jnp.full_like(m_scr, -jnp.inf)
    acc_scr[...] = jnp.zeros_like(acc_scr)

    def block(j, masked):
        keys = pl.ds(pl.multiple_of(j * tq, tq), tq)
        kb = kr_scr[keys, :]
        vb = vext_scr[keys, :]
        for g in range(tq // rg):
            rows = slice(g * rg, (g + 1) * rg)
            if masked:
                row = lax.broadcasted_iota(jnp.int32, (rg, tq), 0) + g * rg
                col = lax.broadcasted_iota(jnp.int32, (rg, tq), 1)
                keep = col <= row
            for c in range(2):
                s = _dot_nt(q_scr[c, rows, :], kb)
                if masked:
                    s = jnp.where(keep, s, -jnp.inf)
                m_prev = m_scr[c, rows, :]
                m_new = jnp.maximum(m_prev, jnp.max(s, axis=-1, keepdims=True))
                alpha = jnp.exp2(m_prev - m_new)
                p = jnp.exp2(s - m_new[:, 0:1])
                pv = _dot(p.astype(BF16), vb)
                acc_scr[c, rows, 0:DA_V_DIM] = alpha * acc_scr[c, rows, 0:DA_V_DIM] + pv[:, 0:DA_V_DIM]
                acc_scr[c, rows, DA_V_DIM:2 * DA_V_DIM] = (alpha * acc_scr[c, rows, DA_V_DIM:2 * DA_V_DIM]
                                                           + pv[:, DA_V_DIM:2 * DA_V_DIM])
                m_scr[c, rows, :] = m_new

    def body(j, carry):
        block(j, False)
        return carry

    lax.fori_loop(0, qi, body, 0)
    block(qi, True)

    lam = (jnp.exp(jnp.sum(lq1_ref[...] * lk1_ref[...], axis=-1, keepdims=True))
           - jnp.exp(jnp.sum(lq2_ref[...] * lk2_ref[...], axis=-1, keepdims=True)) + lambda_init)
    o1 = acc_scr[0, :, 0:DA_V_DIM] / acc_scr[0, :, DA_V_DIM:DA_V_DIM + 1]
    o2 = acc_scr[1, :, 0:DA_V_DIM] / acc_scr[1, :, DA_V_DIM:DA_V_DIM + 1]
    o = o1 - lam * o2
    o = o * lax.rsqrt(jnp.mean(o * o, axis=-1, keepdims=True) + EPS) * g_ref[...]
    o = o * (1.0 - lambda_init) * _silu(z_ref[...].astype(F32))
    out_ref[...] = o.astype(BF16)


def _diff_attn_branch(y_main, bsz, seq, cos_t, sin_t, lq1, lk1, lq2, lk2, subln_g, lambda_init, tq, rg):
    ntok = bsz * seq
    tq = min(tq, seq)
    nq = seq // tq
    per_blk = D_BRANCH // DA_V_DIM

    def tile(blk):
        return pl.BlockSpec((tq, DA_V_DIM), lambda b, h, i: (b * nq + i, blk * per_blk + h))

    def full(blk):
        return pl.BlockSpec((seq, DA_V_DIM), lambda b, h, i: (b, blk * per_blk + h))

    def vec(n):
        return pl.BlockSpec((1, n), lambda b, h, i: (0, 0))

    q_tab = pl.BlockSpec((tq, LANES), lambda b, h, i: (i, 0))
    k_tab = pl.BlockSpec((seq, LANES), lambda b, h, i: (0, 0))
    lvec = [a.reshape(1, DA_HEAD_DIM).astype(F32) for a in (lq1, lk1, lq2, lk2)]
    return pl.pallas_call(
        functools.partial(_diff_attn_kernel, lambda_init=lambda_init, tq=tq, rg=min(rg, tq)),
        grid=(bsz, DA_HEADS, nq),
        in_specs=[tile(DA_BLK0), full(DA_BLK0 + 1), full(DA_BLK0 + 2), tile(DA_BLK0 + 3),
                  q_tab, q_tab, k_tab, k_tab,
                  vec(DA_HEAD_DIM), vec(DA_HEAD_DIM), vec(DA_HEAD_DIM), vec(DA_HEAD_DIM), vec(DA_V_DIM)],
        out_specs=pl.BlockSpec((tq, DA_V_DIM), lambda b, h, i: (b * nq + i, h)),
        out_shape=jax.ShapeDtypeStruct((ntok, D_BRANCH), BF16),
        scratch_shapes=[pltpu.VMEM((seq, 2 * DA_V_DIM), BF16), pltpu.VMEM((seq, DA_V_DIM), BF16),
                        pltpu.VMEM((2, tq, DA_V_DIM), BF16),
                        pltpu.VMEM((2, tq, LANES), F32), pltpu.VMEM((2, tq, 2 * DA_V_DIM), F32)],
        compiler_params=_params(("parallel", "parallel", "arbitrary")),
        name="diff_attn",
    )(y_main, y_main, y_main, y_main, cos_t, sin_t, cos_t, sin_t, *lvec,
      subln_g.reshape(1, DA_V_DIM).astype(F32))


def _mem_attn_kernel(q_ref, z_ref, k_ref, v_ref, out_ref):
    dh = XA_HEAD_DIM
    for h in range(XA_HEADS):
        sl = slice(h * dh, (h + 1) * dh)
        s = _dot_nt(q_ref[:, sl], k_ref[:, sl]) * (dh ** -0.5)
        p = jnp.exp(s - jnp.max(s, axis=-1, keepdims=True))
        p = p / jnp.sum(p, axis=-1, keepdims=True)
        o = _dot(p.astype(BF16), v_ref[:, sl])
        out_ref[:, sl] = (o * _silu(z_ref[:, sl].astype(F32))).astype(BF16)


def _mem_attn_branch(y_main, kv, bsz, seq, mem_len, tq):
    tq = min(tq, seq)
    nq = seq // tq
    return pl.pallas_call(
        _mem_attn_kernel,
        grid=(bsz, nq),
        in_specs=[pl.BlockSpec((tq, D_BRANCH), lambda b, i: (b * nq + i, XA_BLK0)),
                  pl.BlockSpec((tq, D_BRANCH), lambda b, i: (b * nq + i, XA_BLK0 + 1)),
                  pl.BlockSpec((mem_len, D_BRANCH), lambda b, i: (b, 0)),
                  pl.BlockSpec((mem_len, D_BRANCH), lambda b, i: (b, 1))],
        out_specs=pl.BlockSpec((tq, D_BRANCH), lambda b, i: (b * nq + i, 0)),
        out_shape=jax.ShapeDtypeStruct((bsz * seq, D_BRANCH), BF16),
        compiler_params=_params(("parallel", "parallel")),
        name="mem_attn",
    )(y_main, y_main, kv, kv)


def _merge_kernel(*refs):
    y_refs = refs[0:N_BRANCH]
    g_refs = refs[N_BRANCH:2 * N_BRANCH]
    w_refs = refs[2 * N_BRANCH:3 * N_BRANCH]
    out_ref, y0_scr, perm_scr = refs[3 * N_BRANCH:]

    @pl.when(pl.program_id(1) == 0)
    def _():
        rows = y_refs[0].shape[1]
        nslab = D_BRANCH // LANES
        for s in range(S5_T):
            ys = y_refs[0][s].astype(F32)
            for c in range(nslab):
                perm_scr[c, pl.ds(s, rows, stride=S5_T), :] = ys[:, c * LANES:(c + 1) * LANES]
        for c in range(nslab):
            y0_scr[:, c * LANES:(c + 1) * LANES] = perm_scr[c].astype(BF16)

    acc = jax.nn.sigmoid(g_refs[0][...].astype(F32)) * _dot(y0_scr[...], w_refs[0][0])
    for y_ref, g_ref, w_ref in zip(y_refs[1:], g_refs[1:], w_refs[1:]):
        acc = acc + jax.nn.sigmoid(g_ref[...].astype(F32)) * _dot(y_ref[...], w_ref[0])
    out_ref[...] = acc.astype(BF16)


def _merge(branches, y_main, w_branch, tm, tn):
    ntok = y_main.shape[0]
    tm = min(tm, ntok)
    ncol = D_MODEL // tn
    y_specs = ([pl.BlockSpec((S5_T, tm // S5_T, D_BRANCH), lambda i, n: (0, i, 0))]
               + [pl.BlockSpec((tm, D_BRANCH), lambda i, n: (i, 0))] * (N_BRANCH - 1))
    g_specs = [pl.BlockSpec((tm, tn), lambda i, n, b=b: (i, (GATE_COL0 + b * D_MODEL) // tn + n))
               for b in range(N_BRANCH)]
    w_specs = [pl.BlockSpec((1, D_BRANCH, tn), lambda i, n, b=b: (b, 0, n)) for b in range(N_BRANCH)]
    return pl.pallas_call(
        _merge_kernel,
        grid=(ntok // tm, ncol),
        in_specs=y_specs + g_specs + w_specs,
        out_specs=pl.BlockSpec((tm, tn), lambda i, n: (i, n)),
        out_shape=jax.ShapeDtypeStruct((ntok, D_MODEL), BF16),
        scratch_shapes=[pltpu.VMEM((tm, D_BRANCH), BF16), pltpu.VMEM((D_BRANCH // LANES, tm, LANES), F32)],
        compiler_params=_params(("parallel", "arbitrary")),
        name="merge",
    )(*branches, *([y_main] * N_BRANCH), *([w_branch] * N_BRANCH))


def _out_proj_kernel(m_ref, w_ref, g_ref, x_ref, out_ref):
    o = _dot(m_ref[...], w_ref[...])
    o = o * lax.rsqrt(jnp.mean(o * o, axis=-1, keepdims=True) + EPS) * g_ref[...]
    out_ref[...] = x_ref[...] + o


def _out_proj(merged, w_out, g_post, x, tm):
    ntok = x.shape[0]
    tm = min(tm, ntok)
    return pl.pallas_call(
        _out_proj_kernel,
        grid=(ntok // tm,),
        in_specs=[pl.BlockSpec((tm, D_MODEL), lambda i: (i, 0)),
                  pl.BlockSpec((D_MODEL, D_MODEL), lambda i: (0, 0)),
                  pl.BlockSpec((1, D_MODEL), lambda i: (0, 0)),
                  pl.BlockSpec((tm, D_MODEL), lambda i: (i, 0))],
        out_specs=pl.BlockSpec((tm, D_MODEL), lambda i: (i, 0)),
        out_shape=jax.ShapeDtypeStruct((ntok, D_MODEL), F32),
        compiler_params=_params(("parallel",)),
        name="out_proj",
    )(merged, w_out, g_post.reshape(1, D_MODEL).astype(F32), x)


def kernel(x, mem, g_pre, w_in, s5_lam_re, s5_lam_im, s5_log_dt, s5_b_re, s5_b_im, s5_c_re, s5_c_im, s5_d, s5_w_glu, s5_b_glu, ml_conv_w, ml_conv_b, ml_b_i, ml_b_f, ml_norm_g, da_lq1, da_lk1, da_lq2, da_lk2, da_subln_g, g_mem, xa_w_kv, w_branch, w_out, g_post):
    bsz, seq, _ = x.shape
    mem_len = mem.shape[1]
    depth = w_in.shape[0]
    ntok = bsz * seq
    cos_t, sin_t = _rope_tables(seq)
    xf = x.reshape(ntok, D_MODEL).astype(F32)
    memf = mem.reshape(bsz * mem_len, D_MODEL).astype(F32)
    gate0 = 7 * D_BRANCH
    expanders = _s5_expanders()
    s5_ops = jax.vmap(_s5_operators)(s5_lam_re, s5_lam_im, s5_log_dt, s5_b_re, s5_b_im, s5_c_re, s5_c_im)
    for l in range(depth):
        lambda_init = 0.8 - 0.6 * math.exp(-0.3 * l)
        w_l = w_in[l]
        w_main = jnp.concatenate([w_l[:, :gate0], w_l[:, gate0 + 2 * ML_HEADS:]], axis=1).astype(BF16)
        w_gate = jnp.pad(w_l[:, gate0:gate0 + 2 * ML_HEADS], ((0, 0), (0, LANES - 2 * ML_HEADS))).astype(BF16)
        y_main, s5_in, gates = _in_proj(xf, g_pre[l], w_main, w_gate, tm=1024, tn=1024)
        kv = _norm_proj(memf, g_mem[l], xa_w_kv[l].astype(BF16), tm=1024, tn=1024)

        y_s5 = _s5_branch(s5_in, bsz, seq, s5_ops, l, expanders, s5_d[l].astype(F32), s5_w_glu[l], s5_b_glu[l],
                          rt=512)
        y_ml = _mlstm_branch(y_main, gates, bsz, seq, ml_conv_w[l], ml_conv_b[l], ml_b_i[l], ml_b_f[l],
                             ml_norm_g[l])
        y_da = _diff_attn_branch(y_main, bsz, seq, cos_t, sin_t, da_lq1[l], da_lk1[l], da_lq2[l], da_lk2[l],
                                 da_subln_g[l], lambda_init, tq=1024, rg=128)
        y_xa = _mem_attn_branch(y_main, kv, bsz, seq, mem_len, tq=512)

        merged = _merge((y_s5, y_ml, y_da, y_xa), y_main, w_branch[l].astype(BF16), tm=1024, tn=512)
        xf = _out_proj(merged, w_out[l].astype(BF16), g_post[l], xf, tm=512)
    return xf.reshape(bsz, seq, D_MODEL).astype(x.dtype)
```

```python
import functools
import math

import jax
import jax.numpy as jnp
from jax import lax
from jax.experimental import pallas as pl
from jax.experimental.pallas import tpu as pltpu

F32 = jnp.float32
BF16 = jnp.bfloat16

D_MODEL = 2048
D_BRANCH = 1024
N_BRANCH = 4
S5_GROUP = 16
S5_GROUPS = D_BRANCH // S5_GROUP
S5_STATE = 64
ML_HEADS = 4
ML_HEAD_DIM = D_BRANCH // ML_HEADS
ML_CHUNK = 128
ML_CONV = 4
DA_HEADS = 8
DA_HEAD_DIM = 64
DA_V_DIM = 2 * DA_HEAD_DIM
XA_HEADS = 4
XA_HEAD_DIM = D_BRANCH // XA_HEADS
ROPE_THETA = 10000.0
EPS = 1e-6
LOG2E = 1.4426950408889634

LANES = 128
VT_ROWS = DA_V_DIM + 16
S5_T = 16
S5_LANE_GROUPS = LANES // S5_GROUP
S5_NBLK = D_BRANCH // LANES
S5_SBLK = S5_LANE_GROUPS * S5_STATE
S5_SLAB = S5_SBLK // LANES
N_S5IN = 2 * D_BRANCH
N_MAIN = 11 * D_BRANCH + N_BRANCH * D_MODEL
ML_BLK0, DA_BLK0, XA_BLK0 = 0, 5, 9
GATE_COL0 = 11 * D_BRANCH
MIB = 1024 * 1024
VMEM_LIMIT = 48 * MIB


def _params(sem, vmem=VMEM_LIMIT):
    return pltpu.CompilerParams(dimension_semantics=sem, vmem_limit_bytes=vmem)


def _silu(x):
    return x * jax.nn.sigmoid(x)


def _dot(a, b):
    return jnp.dot(a, b, preferred_element_type=F32)


def _dot_nt(a, b):
    return lax.dot_general(a, b, (((1,), (1,)), ((), ())), preferred_element_type=F32)


def _rms_norm_bf16(x, g):
    ms = jnp.mean(x * x, axis=-1, keepdims=True)
    return (x * lax.rsqrt(ms + EPS) * g).astype(BF16)


def _norm_proj_kernel(x_ref, g_ref, w_ref, y_ref, h_scr):
    @pl.when(pl.program_id(1) == 0)
    def _():
        h_scr[...] = _rms_norm_bf16(x_ref[...], g_ref[...])

    y_ref[...] = _dot(h_scr[...], w_ref[...]).astype(y_ref.dtype)


def _norm_proj(x, g, w, tm, tn):
    m, k = x.shape
    n = w.shape[1]
    tm = min(tm, m)
    return pl.pallas_call(
        _norm_proj_kernel,
        grid=(m // tm, n // tn),
        in_specs=[pl.BlockSpec((tm, k), lambda i, j: (i, 0)),
                  pl.BlockSpec((1, k), lambda i, j: (0, 0)),
                  pl.BlockSpec((k, tn), lambda i, j: (0, j))],
        out_specs=pl.BlockSpec((tm, tn), lambda i, j: (i, j)),
        out_shape=jax.ShapeDtypeStruct((m, n), BF16),
        scratch_shapes=[pltpu.VMEM((tm, k), BF16)],
        compiler_params=_params(("parallel", "arbitrary")),
        name="norm_proj",
    )(x, g.reshape(1, k), w)


def _in_proj_kernel(x_ref, g_ref, w_ref, wa_ref, y_ref, s5_ref, a_ref, h_scr, perm_scr):
    j = pl.program_id(1)
    tm, tn = y_ref.shape
    n_s5 = N_S5IN // tn

    @pl.when(j == 0)
    def _():
        h = _rms_norm_bf16(x_ref[...], g_ref[...])
        h_scr[...] = h
        a_ref[...] = _dot(h, wa_ref[...])

    @pl.when(j < n_s5)
    def _():
        acc = _dot(h_scr[...], w_ref[...])
        for c in range(tn // LANES):
            perm_scr[c] = acc[:, c * LANES:(c + 1) * LANES]
        for s in range(S5_T):
            for c in range(tn // LANES):
                rows = perm_scr[c, pl.ds(s, tm // S5_T, stride=S5_T), :]
                s5_ref[s, :, c * LANES:(c + 1) * LANES] = rows.astype(BF16)

    @pl.when(j >= n_s5)
    def _():
        y_ref[...] = _dot(h_scr[...], w_ref[...]).astype(BF16)


def _in_proj(x, g, w, w_aux, tm, tn):
    m, k = x.shape
    tm = min(tm, m)
    n_s5 = N_S5IN // tn
    return pl.pallas_call(
        _in_proj_kernel,
        grid=(m // tm, (N_S5IN + N_MAIN) // tn),
        in_specs=[pl.BlockSpec((tm, k), lambda i, j: (i, 0)),
                  pl.BlockSpec((1, k), lambda i, j: (0, 0)),
                  pl.BlockSpec((k, tn), lambda i, j: (0, j)),
                  pl.BlockSpec((k, LANES), lambda i, j: (0, 0))],
        out_specs=[pl.BlockSpec((tm, tn), lambda i, j: (i, jnp.maximum(j - n_s5, 0))),
                   pl.BlockSpec((S5_T, tm // S5_T, tn), lambda i, j: (0, i, jnp.minimum(j, n_s5 - 1))),
                   pl.BlockSpec((tm, LANES), lambda i, j: (i, 0))],
        out_shape=[jax.ShapeDtypeStruct((m, N_MAIN), BF16),
                   jax.ShapeDtypeStruct((S5_T, m // S5_T, N_S5IN), BF16),
                   jax.ShapeDtypeStruct((m, LANES), F32)],
        scratch_shapes=[pltpu.VMEM((tm, k), BF16), pltpu.VMEM((tn // LANES, tm, LANES), F32)],
        compiler_params=_params(("parallel", "arbitrary"), 56 * MIB),
        name="in_proj",
    )(x, g.reshape(1, k), w, w_aux)


def _s5_operators(lam_re, lam_im, log_dt, b_re, b_im, c_re, c_im):
    hi = lax.Precision.HIGHEST
    G, P, C, T, J, LG = S5_GROUPS, S5_STATE, S5_GROUP, S5_T, S5_NBLK, S5_LANE_GROUPS
    lam = lax.complex(jnp.minimum(lam_re.astype(F32), -1e-4), lam_im.astype(F32))
    dt = jnp.exp(log_dt.astype(F32))[:, None]
    z = lam * dt
    lam_bar = jnp.exp(z)
    b_bar = ((lam_bar - 1.0) / lam)[..., None] * lax.complex(b_re.astype(F32), b_im.astype(F32))
    c = lax.complex(c_re.astype(F32), c_im.astype(F32))
    steps = jnp.arange(T + 1, dtype=F32)
    pw = jnp.exp(z[None] * steps[:, None, None].astype(jnp.complex64))

    kern = jnp.real(jnp.einsum('gop,kgp,gpi->kgoi', c, pw[:T], b_bar, precision=hi))
    s_idx = jnp.arange(T)[:, None]
    t_idx = jnp.arange(T)[None, :]
    lag = jnp.clip(t_idx - s_idx, 0, T - 1)
    toe = kern[lag] * (t_idx >= s_idx)[:, :, None, None, None].astype(F32)
    toe = toe.reshape(T, T, J, LG, C, C).transpose(2, 0, 3, 5, 1, 4)
    a2 = toe.reshape(J, T * LANES, T * C).astype(BF16)

    bs = pw[T - 1 - jnp.arange(T)][..., None] * b_bar[None]
    bs = jnp.stack([jnp.real(bs), jnp.imag(bs)], axis=0)
    bs = bs.reshape(2, T, J, LG, P, C).transpose(2, 1, 3, 5, 0, 4)
    b2 = bs.reshape(J, T * LANES, 2 * P).astype(BF16)

    cs = c[None] * pw[1:T + 1][:, :, None, :]
    cs = jnp.stack([jnp.real(cs), -jnp.imag(cs)], axis=0)
    cs = cs.reshape(2, T, J, LG, C, P).transpose(2, 0, 3, 5, 1, 4)
    c2 = cs.reshape(J, 2 * S5_SBLK, T * C).astype(BF16)

    lam_t = pw[T].reshape(1, G * P)
    return a2, b2, c2, jnp.real(lam_t), jnp.imag(lam_t)


def _s5_expanders():
    r = jnp.arange(S5_T * S5_GROUP)[:, None]
    c = jnp.arange(S5_T * LANES)[None, :]
    e_to = ((r // S5_GROUP == c // LANES) & (r % S5_GROUP == c % S5_GROUP)).astype(BF16)
    r = jnp.arange(2 * S5_STATE)[:, None]
    c = jnp.arange(2 * S5_SBLK)[None, :]
    e_rp = ((r // S5_STATE == c // S5_SBLK) & (r % S5_STATE == c % S5_STATE)).astype(BF16)
    return e_to, e_rp


def _expand_block_diag(dst_scr, compact, exp_ref, row_group, col_group):
    rows = compact.shape[0]
    cols = exp_ref.shape[1]
    slab = 2 * LANES
    rg = (lax.broadcasted_iota(jnp.int32, (rows, slab), 0) // row_group) % S5_LANE_GROUPS
    cl = lax.broadcasted_iota(jnp.int32, (rows, slab), 1)
    for c0 in range(0, cols, slab):
        cg = ((cl + c0) // col_group) % S5_LANE_GROUPS
        blk = _dot(compact, exp_ref[:, c0:c0 + slab])
        dst_scr[:, c0:c0 + slab] = jnp.where(rg == cg, blk, 0.0).astype(BF16)


def _s5_local_kernel(u_ref, b2_ref, erp_ref, s_ref, min_scr):
    @pl.when(pl.program_id(1) == 0)
    def _():
        _expand_block_diag(min_scr, b2_ref[0], erp_ref, S5_GROUP, S5_STATE)

    ucat = jnp.concatenate([u_ref[s] for s in range(S5_T)], axis=1)
    s = _dot(ucat, min_scr[...])
    for c in range(2 * S5_SLAB):
        s_ref[c] = s[:, c * LANES:(c + 1) * LANES]


def _s5_scan_kernel(sr_ref, si_ref, ar_ref, ai_ref, xr_ref, xi_ref, *, bsz, nchunk):
    ar = [jnp.broadcast_to(ar_ref[:, c * LANES:(c + 1) * LANES], (bsz, LANES)) for c in range(S5_SLAB)]
    ai = [jnp.broadcast_to(ai_ref[:, c * LANES:(c + 1) * LANES], (bsz, LANES)) for c in range(S5_SLAB)]

    def step(k, carry):
        rows = pl.ds(k, bsz, stride=nchunk)
        new = []
        for c in range(S5_SLAB):
            xr, xi = carry[c]
            xr_ref[c, rows, :] = xr
            xi_ref[c, rows, :] = xi
            nr = ar[c] * xr - ai[c] * xi + sr_ref[c, rows, :]
            ni = ar[c] * xi + ai[c] * xr + si_ref[c, rows, :]
            new.append((nr, ni))
        return tuple(new)

    zero = jnp.zeros((bsz, LANES), F32)
    lax.fori_loop(0, nchunk, step, tuple((zero, zero) for _ in range(S5_SLAB)), unroll=4)


def _s5_out_kernel(u_ref, a2_ref, c2_ref, eto_ref, xr_ref, xi_ref, d_ref, out_ref, mi_scr, mo_scr):
    @pl.when(pl.program_id(1) == 0)
    def _():
        _expand_block_diag(mi_scr, a2_ref[0], eto_ref, S5_GROUP, S5_GROUP)
        _expand_block_diag(mo_scr, c2_ref[0], eto_ref, S5_STATE, S5_GROUP)

    us = [u_ref[s] for s in range(S5_T)]
    ucat = jnp.concatenate(us, axis=1)
    xcat = jnp.concatenate([xr_ref[c] for c in range(S5_SLAB)] + [xi_ref[c] for c in range(S5_SLAB)],
                           axis=1).astype(BF16)
    d = d_ref[...]
    tile = 2 * LANES
    for c0 in range(0, S5_T * LANES, tile):
        y = _dot(ucat[:, :c0 + tile], mi_scr[:c0 + tile, c0:c0 + tile]) + _dot(xcat, mo_scr[:, c0:c0 + tile])
        for t in range(c0 // LANES, (c0 + tile) // LANES):
            yt = y[:, t * LANES - c0:(t + 1) * LANES - c0] + d * us[t].astype(F32)
            out_ref[t] = jax.nn.gelu(yt).astype(BF16)


def _s5_glu_kernel(g_ref, z_ref, w_ref, b_ref, out_ref):
    g = g_ref[...]
    a = _dot(g, w_ref[...]) + b_ref[...]
    out_ref[...] = (g.astype(F32) * jax.nn.sigmoid(a) * _silu(z_ref[...].astype(F32))).astype(BF16)


def _s5_branch(s5_in, bsz, seq, ops, layer, expanders, d_skip, w_glu, b_glu, rt):
    a2, b2, c2, lam_r, lam_i = ops
    e_to, e_rp = expanders
    nrow = bsz * seq // S5_T
    nchunk = seq // S5_T
    rt = min(rt, nrow)
    u_spec = pl.BlockSpec((S5_T, rt, LANES), lambda j, r: (0, r, j))

    def whole(a):
        return pl.BlockSpec(a.shape, lambda j, r: (0,) * a.ndim)

    def per_blk(a):
        return pl.BlockSpec((pl.Squeezed(), 1) + a.shape[2:], lambda j, r: (layer, j, 0, 0))

    s_loc = pl.pallas_call(
        _s5_local_kernel,
        grid=(S5_NBLK, nrow // rt),
        in_specs=[u_spec, per_blk(b2), whole(e_rp)],
        out_specs=pl.BlockSpec((2 * S5_SLAB, rt, LANES), lambda j, r: (j, r, 0)),
        out_shape=jax.ShapeDtypeStruct((S5_NBLK * 2 * S5_SLAB, nrow, LANES), F32),
        scratch_shapes=[pltpu.VMEM((S5_T * LANES, 2 * S5_SBLK), BF16)],
        compiler_params=_params(("parallel", "arbitrary")),
        name="s5_local",
    )(s5_in, b2, e_rp)

    xr, xi = pl.pallas_call(
        functools.partial(_s5_scan_kernel, bsz=bsz, nchunk=nchunk),
        grid=(S5_NBLK,),
        in_specs=[pl.BlockSpec((S5_SLAB, nrow, LANES), lambda j: (2 * j, 0, 0)),
                  pl.BlockSpec((S5_SLAB, nrow, LANES), lambda j: (2 * j + 1, 0, 0)),
                  pl.BlockSpec((pl.Squeezed(), 1, S5_SBLK), lambda j: (layer, 0, j)),
                  pl.BlockSpec((pl.Squeezed(), 1, S5_SBLK), lambda j: (layer, 0, j))],
        out_specs=[pl.BlockSpec((S5_SLAB, nrow, LANES), lambda j: (j, 0, 0))] * 2,
        out_shape=[jax.ShapeDtypeStruct((S5_NBLK * S5_SLAB, nrow, LANES), F32)] * 2,
        compiler_params=_params(("parallel",)),
        name="s5_scan",
    )(s_loc, s_loc, lam_r, lam_i)

    gel = pl.pallas_call(
        _s5_out_kernel,
        grid=(S5_NBLK, nrow // rt),
        in_specs=[u_spec, per_blk(a2), per_blk(c2), whole(e_to),
                  pl.BlockSpec((S5_SLAB, rt, LANES), lambda j, r: (j, r, 0)),
                  pl.BlockSpec((S5_SLAB, rt, LANES), lambda j, r: (j, r, 0)),
                  pl.BlockSpec((1, LANES), lambda j, r: (0, j))],
        out_specs=pl.BlockSpec((S5_T, rt, LANES), lambda j, r: (0, r, j)),
        out_shape=jax.ShapeDtypeStruct((S5_T, nrow, D_BRANCH), BF16),
        scratch_shapes=[pltpu.VMEM((S5_T * LANES, S5_T * LANES), BF16),
                        pltpu.VMEM((2 * S5_SBLK, S5_T * LANES), BF16)],
        compiler_params=_params(("parallel", "arbitrary")),
        name="s5_out",
    )(s5_in, a2, c2, e_to, xr, xi, d_skip.reshape(1, D_BRANCH))

    row_blk = pl.BlockSpec((pl.Squeezed(), rt, D_BRANCH), lambda t, r: (t, r, 0))
    return pl.pallas_call(
        _s5_glu_kernel,
        grid=(S5_T, nrow // rt),
        in_specs=[row_blk,
                  pl.BlockSpec((pl.Squeezed(), rt, D_BRANCH), lambda t, r: (t, r, 1)),
                  pl.BlockSpec((D_BRANCH, D_BRANCH), lambda t, r: (0, 0)),
                  pl.BlockSpec((1, D_BRANCH), lambda t, r: (0, 0))],
        out_specs=row_blk,
        out_shape=jax.ShapeDtypeStruct((S5_T, nrow, D_BRANCH), BF16),
        compiler_params=_params(("parallel", "parallel")),
        name="s5_glu",
    )(gel, s5_in, w_glu.astype(BF16), b_glu.reshape(1, D_BRANCH).astype(F32))


def _split3(x):
    hi = x.astype(BF16)
    r1 = x - hi.astype(F32)
    mid = r1.astype(BF16)
    lo = (r1 - mid.astype(F32)).astype(BF16)
    return hi, mid, lo


def _mlstm_kernel(q_ref, k_ref, v_ref, o_ref, z_ref, if_ref, cw_ref, cb_ref, bif_ref, ng_ref, out_ref,
                  ext_scr, c_scr, n_scr, m_scr):
    lc, dh, nh = ML_CHUNK, ML_HEAD_DIM, ML_HEADS
    tail = 8
    cidx = pl.program_id(1)

    @pl.when(cidx == 0)
    def _():
        ext_scr[0:tail, :] = jnp.zeros((tail, 2 * D_BRANCH), F32)
        c_scr[...] = jnp.zeros_like(c_scr)
        n_scr[...] = jnp.zeros_like(n_scr)
        m_scr[...] = jnp.zeros_like(m_scr)

    @pl.when(cidx > 0)
    def _():
        ext_scr[0:tail, :] = ext_scr[lc:lc + tail, :]

    ext_scr[tail:tail + lc, 0:D_BRANCH] = q_ref[...].astype(F32)
    ext_scr[tail:tail + lc, D_BRANCH:2 * D_BRANCH] = k_ref[...].astype(F32)
    conv = jnp.broadcast_to(cb_ref[...], (lc, 2 * D_BRANCH))
    for w in range(ML_CONV):
        off = tail - (ML_CONV - 1) + w
        conv = conv + cw_ref[w:w + 1, :] * ext_scr[off:off + lc, :]
    qk = _silu(conv)

    gts = if_ref[...] + bif_ref[...]
    lf = jnp.minimum(gts, 0.0) - jnp.log(1.0 + jnp.exp(-jnp.abs(gts)))
    gts_t = jnp.transpose(gts)[0:16, :]
    lf_t = jnp.transpose(lf)[0:16, :]
    row = lax.broadcasted_iota(jnp.int32, (lc, lc), 0)
    col = lax.broadcasted_iota(jnp.int32, (lc, lc), 1)
    causal = col <= row
    tril = jnp.where(causal, 1.0, 0.0).astype(BF16)
    triu = jnp.where(row <= col, 1.0, 0.0).astype(BF16)
    acum_c = sum(_dot(tril, p) for p in _split3(lf))
    acum_r = sum(_dot(p, triu) for p in _split3(lf_t))

    for h in range(nh):
        sl = slice(h * dh, (h + 1) * dh)
        qh = qk[:, h * dh:(h + 1) * dh]
        kh = qk[:, D_BRANCH + h * dh:D_BRANCH + (h + 1) * dh] * (dh ** -0.5)
        vh = v_ref[:, sl]
        qb = qh.astype(BF16)
        a_c = acum_c[:, nh + h:nh + h + 1]
        a_r = acum_r[nh + h:nh + h + 1, :]
        i_c = gts[:, h:h + 1]
        i_r = gts_t[h:h + 1, :]
        gtot = a_c[lc - 1:lc, :]
        m_old = m_scr[h][:, 0:1]
        dmat = jnp.where(causal, a_c - a_r + i_r, -jnp.inf)
        inter = a_c + m_old
        m_row = jnp.maximum(jnp.max(dmat, axis=-1, keepdims=True), inter)
        s = _dot_nt(qb, kh.astype(BF16)) * jnp.exp(dmat - m_row)
        sc = jnp.exp(inter - m_row)
        num = _dot(s.astype(BF16), vh) + sc * _dot(qb, c_scr[h].astype(BF16))
        den = jnp.sum(s, axis=-1, keepdims=True) + sc * jnp.sum(qh * n_scr[h], axis=-1, keepdims=True)
        hout = num / jnp.maximum(jnp.abs(den), jnp.exp(-m_row))

        kw_log = gtot - a_c + i_c
        m_new = jnp.maximum(gtot + m_old, jnp.max(kw_log, axis=0, keepdims=True))
        kw = jnp.exp(kw_log - m_new)
        decay = jnp.exp(gtot + m_old - m_new)
        kk = kh * kw
        c_scr[h] = decay * c_scr[h] + _dot(jnp.transpose(kk).astype(BF16), vh)
        n_scr[h] = decay * n_scr[h] + jnp.sum(kk, axis=0, keepdims=True)
        m_scr[h] = jnp.broadcast_to(m_new, (1, LANES))

        hn = hout * lax.rsqrt(jnp.mean(hout * hout, axis=-1, keepdims=True) + EPS) * ng_ref[:, sl]
        y = jax.nn.sigmoid(o_ref[:, sl].astype(F32)) * hn * _silu(z_ref[:, sl].astype(F32))
        out_ref[:, sl] = y.astype(BF16)


def _mlstm_branch(y_main, gates, bsz, seq, conv_w, conv_b, b_i, b_f, norm_g):
    lc = ML_CHUNK
    nc = seq // lc
    bif = jnp.zeros((1, LANES), F32).at[0, 0:ML_HEADS].set(b_i.astype(F32))
    bif = bif.at[0, ML_HEADS:2 * ML_HEADS].set(b_f.astype(F32))

    def col(cb):
        return pl.BlockSpec((lc, D_BRANCH), lambda b, c: (b * nc + c, cb))

    def const(shape):
        return pl.BlockSpec(shape, lambda b, c: (0,) * len(shape))

    return pl.pallas_call(
        _mlstm_kernel,
        grid=(bsz, nc),
        in_specs=[col(ML_BLK0), col(ML_BLK0 + 1), col(ML_BLK0 + 2), col(ML_BLK0 + 3), col(ML_BLK0 + 4),
                  pl.BlockSpec((lc, LANES), lambda b, c: (b * nc + c, 0)),
                  const((ML_CONV, 2 * D_BRANCH)), const((1, 2 * D_BRANCH)), const((1, LANES)),
                  const((1, D_BRANCH))],
        out_specs=pl.BlockSpec((lc, D_BRANCH), lambda b, c: (b * nc + c, 0)),
        out_shape=jax.ShapeDtypeStruct((bsz * seq, D_BRANCH), BF16),
        scratch_shapes=[pltpu.VMEM((lc + 8, 2 * D_BRANCH), F32),
                        pltpu.VMEM((ML_HEADS, ML_HEAD_DIM, ML_HEAD_DIM), F32),
                        pltpu.VMEM((ML_HEADS, 1, ML_HEAD_DIM), F32),
                        pltpu.VMEM((ML_HEADS, 1, LANES), F32)],
        compiler_params=_params(("parallel", "arbitrary")),
        name="mlstm",
    )(y_main, y_main, y_main, y_main, y_main, gates,
      conv_w.astype(F32), conv_b.reshape(1, 2 * D_BRANCH).astype(F32), bif,
      norm_g.reshape(1, D_BRANCH).astype(F32))


def _rope_tables(seq):
    half = DA_HEAD_DIM // 2
    inv = 1.0 / (ROPE_THETA ** (jnp.arange(0, DA_HEAD_DIM, 2, dtype=F32) / DA_HEAD_DIM))
    ang = jnp.arange(seq, dtype=F32)[:, None] * inv[None, :]
    cos, sin = jnp.cos(ang), jnp.sin(ang)
    cos_t = jnp.tile(jnp.concatenate([cos, cos], axis=1), (1, LANES // DA_HEAD_DIM))
    sin_t = jnp.tile(jnp.concatenate([-sin, sin], axis=1), (1, LANES // DA_HEAD_DIM))
    del half
    return cos_t, sin_t


def _rope(x, cos, sin):
    half = DA_HEAD_DIM // 2
    lane = lax.broadcasted_iota(jnp.int32, (1, LANES), 1)
    first_half = (lane % DA_HEAD_DIM) < half
    partner = jnp.where(first_half, pltpu.roll(x, LANES - half, 1), pltpu.roll(x, half, 1))
    return x * cos + partner * sin


def _diff_attn_kernel(q_ref, k_ref, v_ref, z_ref, cosq_ref, sinq_ref, cosk_ref, sink_ref,
                      lq1_ref, lk1_ref, lq2_ref, lk2_ref, g_ref,
                      out_ref, vt_scr, kr_scr, qt_scr, m_scr, acc_scr, *, lambda_init, tq, cg):
    qi = pl.program_id(2)
    nkb = vt_scr.shape[0]

    @pl.when(qi == 0)
    def _():
        kr_scr[...] = _rope(k_ref[...].astype(F32), cosk_ref[...], sink_ref[...]).astype(BF16)
        ones_row = lax.broadcasted_iota(jnp.int32, (VT_ROWS - DA_V_DIM, tq), 0) == 0
        for jb in range(nkb):
            vt_scr[jb, 0:DA_V_DIM, :] = jnp.transpose(v_ref[jb * tq:(jb + 1) * tq, :].astype(F32)).astype(BF16)
            vt_scr[jb, DA_V_DIM:VT_ROWS, :] = jnp.where(ones_row, 1.0, 0.0).astype(BF16)

    q = _rope(q_ref[...].astype(F32), cosq_ref[...], sinq_ref[...]) * (DA_HEAD_DIM ** -0.5 * LOG2E)
    comp0 = lax.broadcasted_iota(jnp.int32, (1, DA_V_DIM), 1) < DA_HEAD_DIM
    qt_scr[0] = jnp.transpose(jnp.where(comp0, q, 0.0)).astype(BF16)
    qt_scr[1] = jnp.transpose(jnp.where(comp0, 0.0, q)).astype(BF16)

    m_scr[...] = jnp.full_like(m_scr, -jnp.inf)
    acc_scr[...] = jnp.zeros_like(acc_scr)

    def block(j, masked):
        kb = kr_scr[pl.ds(pl.multiple_of(j * tq, tq), tq), :]
        vtb = vt_scr[j]
        chains = [(g, c) for g in range(tq // cg) for c in range(2)]

        def nkeys(g):
            return (g + 1) * cg if masked else tq

        def scores(g, c):
            return _dot(kb[:nkeys(g)], qt_scr[c, :, g * cg:(g + 1) * cg])

        def softmax(g, c, s):
            cols = slice(g * cg, (g + 1) * cg)
            if masked:
                kpos = lax.broadcasted_iota(jnp.int32, s.shape, 0)
                qpos = lax.broadcasted_iota(jnp.int32, s.shape, 1) + g * cg
                s = jnp.where(kpos <= qpos, s, -jnp.inf)
            m_prev = m_scr[c, :, cols]
            m_new = jnp.maximum(m_prev, jnp.max(s, axis=0, keepdims=True))
            m_scr[c, :, cols] = m_new
            return jnp.exp2(m_prev - m_new), jnp.exp2(s - m_new).astype(BF16)

        def accumulate(g, c, alpha, p):
            cols = slice(g * cg, (g + 1) * cg)
            acc_scr[c, :, cols] = alpha * acc_scr[c, :, cols] + _dot(vtb[:, :nkeys(g)], p)

        ahead, behind = 3, 2
        s_q = [scores(*chains[i]) for i in range(ahead)]
        pending = []
        for i, ch in enumerate(chains):
            s_cur = s_q.pop(0)
            if i + ahead < len(chains):
                s_q.append(scores(*chains[i + ahead]))
            alpha, p = softmax(*ch, s_cur)
            pending.append((*ch, alpha, p))
            if len(pending) > behind:
                accumulate(*pending.pop(0))
        for item in pending:
            accumulate(*item)

    def body(j, carry):
        block(j, False)
        return carry

    lax.fori_loop(0, qi, body, 0)
    block(qi, True)

    lam = (jnp.exp(jnp.sum(lq1_ref[...] * lk1_ref[...], axis=-1, keepdims=True))
           - jnp.exp(jnp.sum(lq2_ref[...] * lk2_ref[...], axis=-1, keepdims=True)) + lambda_init)
    o1 = acc_scr[0, 0:DA_V_DIM, :] / acc_scr[0, DA_V_DIM:DA_V_DIM + 1, :]
    o2 = acc_scr[1, 0:DA_V_DIM, :] / acc_scr[1, DA_V_DIM:DA_V_DIM + 1, :]
    o = jnp.transpose(o1 - lam * o2)
    o = o * lax.rsqrt(jnp.mean(o * o, axis=-1, keepdims=True) + EPS) * g_ref[...]
    o = o * (1.0 - lambda_init) * _silu(z_ref[...].astype(F32))
    out_ref[...] = o.astype(BF16)


def _diff_attn_branch(y_main, bsz, seq, cos_t, sin_t, lq1, lk1, lq2, lk2, subln_g, lambda_init, tq, cg):
    ntok = bsz * seq
    tq = min(tq, seq)
    nq = seq // tq
    per_blk = D_BRANCH // DA_V_DIM

    def tile(blk):
        return pl.BlockSpec((tq, DA_V_DIM), lambda b, h, i: (b * nq + i, blk * per_blk + h))

    def full(blk):
        return pl.BlockSpec((seq, DA_V_DIM), lambda b, h, i: (b, blk * per_blk + h))

    def vec(n):
        return pl.BlockSpec((1, n), lambda b, h, i: (0, 0))

    q_tab = pl.BlockSpec((tq, LANES), lambda b, h, i: (i, 0))
    k_tab = pl.BlockSpec((seq, LANES), lambda b, h, i: (0, 0))
    lvec = [a.reshape(1, DA_HEAD_DIM).astype(F32) for a in (lq1, lk1, lq2, lk2)]
    return pl.pallas_call(
        functools.partial(_diff_attn_kernel, lambda_init=lambda_init, tq=tq, cg=min(cg, tq)),
        grid=(bsz, DA_HEADS, nq),
        in_specs=[tile(DA_BLK0), full(DA_BLK0 + 1), full(DA_BLK0 + 2), tile(DA_BLK0 + 3),
                  q_tab, q_tab, k_tab, k_tab,
                  vec(DA_HEAD_DIM), vec(DA_HEAD_DIM), vec(DA_HEAD_DIM), vec(DA_HEAD_DIM), vec(DA_V_DIM)],
        out_specs=pl.BlockSpec((tq, DA_V_DIM), lambda b, h, i: (b * nq + i, h)),
        out_shape=jax.ShapeDtypeStruct((ntok, D_BRANCH), BF16),
        scratch_shapes=[pltpu.VMEM((nq, VT_ROWS, tq), BF16), pltpu.VMEM((seq, DA_V_DIM), BF16),
                        pltpu.VMEM((2, DA_V_DIM, tq), BF16),
                        pltpu.VMEM((2, 1, tq), F32), pltpu.VMEM((2, VT_ROWS, tq), F32)],
        compiler_params=_params(("parallel", "parallel", "arbitrary")),
        name="diff_attn",
    )(y_main, y_main, y_main, y_main, cos_t, sin_t, cos_t, sin_t, *lvec,
      subln_g.reshape(1, DA_V_DIM).astype(F32))


def _mem_attn_kernel(q_ref, z_ref, k_ref, v_ref, out_ref):
    dh = XA_HEAD_DIM
    for h in range(XA_HEADS):
        sl = slice(h * dh, (h + 1) * dh)
        s = _dot_nt(q_ref[:, sl], k_ref[:, sl]) * (dh ** -0.5)
        p = jnp.exp(s - jnp.max(s, axis=-1, keepdims=True))
        p = p / jnp.sum(p, axis=-1, keepdims=True)
        o = _dot(p.astype(BF16), v_ref[:, sl])
        out_ref[:, sl] = (o * _silu(z_ref[:, sl].astype(F32))).astype(BF16)


def _mem_attn_branch(y_main, kv, bsz, seq, mem_len, tq):
    tq = min(tq, seq)
    nq = seq // tq
    return pl.pallas_call(
        _mem_attn_kernel,
        grid=(bsz, nq),
        in_specs=[pl.BlockSpec((tq, D_BRANCH), lambda b, i: (b * nq + i, XA_BLK0)),
                  pl.BlockSpec((tq, D_BRANCH), lambda b, i: (b * nq + i, XA_BLK0 + 1)),
                  pl.BlockSpec((mem_len, D_BRANCH), lambda b, i: (b, 0)),
                  pl.BlockSpec((mem_len, D_BRANCH), lambda b, i: (b, 1))],
        out_specs=pl.BlockSpec((tq, D_BRANCH), lambda b, i: (b * nq + i, 0)),
        out_shape=jax.ShapeDtypeStruct((bsz * seq, D_BRANCH), BF16),
        compiler_params=_params(("parallel", "parallel")),
        name="mem_attn",
    )(y_main, y_main, kv, kv)


def _merge_kernel(*refs):
    y_refs = refs[0:N_BRANCH]
    g_refs = refs[N_BRANCH:2 * N_BRANCH]
    w_refs = refs[2 * N_BRANCH:3 * N_BRANCH]
    out_ref, y0_scr, perm_scr = refs[3 * N_BRANCH:]

    @pl.when(pl.program_id(1) == 0)
    def _():
        rows = y_refs[0].shape[1]
        nslab = D_BRANCH // LANES
        for s in range(S5_T):
            ys = y_refs[0][s].astype(F32)
            for c in range(nslab):
                perm_scr[c, pl.ds(s, rows, stride=S5_T), :] = ys[:, c * LANES:(c + 1) * LANES]
        for c in range(nslab):
            y0_scr[:, c * LANES:(c + 1) * LANES] = perm_scr[c].astype(BF16)

    acc = jax.nn.sigmoid(g_refs[0][...].astype(F32)) * _dot(y0_scr[...], w_refs[0][0])
    for y_ref, g_ref, w_ref in zip(y_refs[1:], g_refs[1:], w_refs[1:]):
        acc = acc + jax.nn.sigmoid(g_ref[...].astype(F32)) * _dot(y_ref[...], w_ref[0])
    out_ref[...] = acc.astype(BF16)


def _merge(branches, y_main, w_branch, tm, tn):
    ntok = y_main.shape[0]
    tm = min(tm, ntok)
    ncol = D_MODEL // tn
    y_specs = ([pl.BlockSpec((S5_T, tm // S5_T, D_BRANCH), lambda i, n: (0, i, 0))]
               + [pl.BlockSpec((tm, D_BRANCH), lambda i, n: (i, 0))] * (N_BRANCH - 1))
    g_specs = [pl.BlockSpec((tm, tn), lambda i, n, b=b: (i, (GATE_COL0 + b * D_MODEL) // tn + n))
               for b in range(N_BRANCH)]
    w_specs = [pl.BlockSpec((1, D_BRANCH, tn), lambda i, n, b=b: (b, 0, n)) for b in range(N_BRANCH)]
    return pl.pallas_call(
        _merge_kernel,
        grid=(ntok // tm, ncol),
        in_specs=y_specs + g_specs + w_specs,
        out_specs=pl.BlockSpec((tm, tn), lambda i, n: (i, n)),
        out_shape=jax.ShapeDtypeStruct((ntok, D_MODEL), BF16),
        scratch_shapes=[pltpu.VMEM((tm, D_BRANCH), BF16), pltpu.VMEM((D_BRANCH // LANES, tm, LANES), F32)],
        compiler_params=_params(("parallel", "arbitrary")),
        name="merge",
    )(*branches, *([y_main] * N_BRANCH), *([w_branch] * N_BRANCH))


def _out_proj_kernel(m_ref, w_ref, g_ref, x_ref, out_ref):
    o = _dot(m_ref[...], w_ref[...])
    o = o * lax.rsqrt(jnp.mean(o * o, axis=-1, keepdims=True) + EPS) * g_ref[...]
    out_ref[...] = x_ref[...] + o


def _out_proj(merged, w_out, g_post, x, tm):
    ntok = x.shape[0]
    tm = min(tm, ntok)
    return pl.pallas_call(
        _out_proj_kernel,
        grid=(ntok // tm,),
        in_specs=[pl.BlockSpec((tm, D_MODEL), lambda i: (i, 0)),
                  pl.BlockSpec((D_MODEL, D_MODEL), lambda i: (0, 0)),
                  pl.BlockSpec((1, D_MODEL), lambda i: (0, 0)),
                  pl.BlockSpec((tm, D_MODEL), lambda i: (i, 0))],
        out_specs=pl.BlockSpec((tm, D_MODEL), lambda i: (i, 0)),
        out_shape=jax.ShapeDtypeStruct((ntok, D_MODEL), F32),
        compiler_params=_params(("parallel",)),
        name="out_proj",
    )(merged, w_out, g_post.reshape(1, D_MODEL).astype(F32), x)


def kernel(x, mem, g_pre, w_in, s5_lam_re, s5_lam_im, s5_log_dt, s5_b_re, s5_b_im, s5_c_re, s5_c_im, s5_d, s5_w_glu, s5_b_glu, ml_conv_w, ml_conv_b, ml_b_i, ml_b_f, ml_norm_g, da_lq1, da_lk1, da_lq2, da_lk2, da_subln_g, g_mem, xa_w_kv, w_branch, w_out, g_post):
    bsz, seq, _ = x.shape
    mem_len = mem.shape[1]
    depth = w_in.shape[0]
    ntok = bsz * seq
    cos_t, sin_t = _rope_tables(seq)
    xf = x.reshape(ntok, D_MODEL).astype(F32)
    memf = mem.reshape(bsz * mem_len, D_MODEL).astype(F32)
    gate0 = 7 * D_BRANCH
    expanders = _s5_expanders()
    s5_ops = jax.vmap(_s5_operators)(s5_lam_re, s5_lam_im, s5_log_dt, s5_b_re, s5_b_im, s5_c_re, s5_c_im)
    for l in range(depth):
        lambda_init = 0.8 - 0.6 * math.exp(-0.3 * l)
        w_l = w_in[l]
        w_main = jnp.concatenate([w_l[:, :gate0], w_l[:, gate0 + 2 * ML_HEADS:]], axis=1).astype(BF16)
        w_gate = jnp.pad(w_l[:, gate0:gate0 + 2 * ML_HEADS], ((0, 0), (0, LANES - 2 * ML_HEADS))).astype(BF16)
        y_main, s5_in, gates = _in_proj(xf, g_pre[l], w_main, w_gate, tm=1024, tn=1024)
        kv = _norm_proj(memf, g_mem[l], xa_w_kv[l].astype(BF16), tm=1024, tn=1024)

        y_s5 = _s5_branch(s5_in, bsz, seq, s5_ops, l, expanders, s5_d[l].astype(F32), s5_w_glu[l], s5_b_glu[l],
                          rt=512)
        y_ml = _mlstm_branch(y_main, gates, bsz, seq, ml_conv_w[l], ml_conv_b[l], ml_b_i[l], ml_b_f[l],
                             ml_norm_g[l])
        y_da = _diff_attn_branch(y_main, bsz, seq, cos_t, sin_t, da_lq1[l], da_lk1[l], da_lq2[l], da_lk2[l],
                                 da_subln_g[l], lambda_init, tq=1024, cg=256)
        y_xa = _mem_attn_branch(y_main, kv, bsz, seq, mem_len, tq=512)

        merged = _merge((y_s5, y_ml, y_da, y_xa), y_main, w_branch[l].astype(BF16), tm=1024, tn=512)
        xf = _out_proj(merged, w_out[l].astype(BF16), g_post[l], xf, tm=512)
    return xf.reshape(bsz, seq, D_MODEL).astype(x.dtype)
```

```python
import functools
import math

import jax
import jax.numpy as jnp
from jax import lax
from jax.experimental import pallas as pl
from jax.experimental.pallas import tpu as pltpu

F32 = jnp.float32
BF16 = jnp.bfloat16

D_MODEL = 2048
D_BRANCH = 1024
N_BRANCH = 4
S5_GROUP = 16
S5_GROUPS = D_BRANCH // S5_GROUP
S5_STATE = 64
ML_HEADS = 4
ML_HEAD_DIM = D_BRANCH // ML_HEADS
ML_CHUNK = 128
ML_CONV = 4
ML_TAIL = 16
DA_HEADS = 8
DA_HEAD_DIM = 64
DA_V_DIM = 2 * DA_HEAD_DIM
XA_HEADS = 4
XA_HEAD_DIM = D_BRANCH // XA_HEADS
ROPE_THETA = 10000.0
EPS = 1e-6
LOG2E = 1.4426950408889634

LANES = 128
VT_ROWS = DA_V_DIM + 16
S5_T = 16
S5_LANE_GROUPS = LANES // S5_GROUP
S5_NBLK = D_BRANCH // LANES
S5_SBLK = S5_LANE_GROUPS * S5_STATE
S5_SLAB = S5_SBLK // LANES
N_S5IN = 2 * D_BRANCH
N_MAIN = 11 * D_BRANCH + N_BRANCH * D_MODEL
ML_BLK0, DA_BLK0, XA_BLK0 = 0, 5, 9
GATE_COL0 = 11 * D_BRANCH
MIB = 1024 * 1024
VMEM_LIMIT = 48 * MIB


def _params(sem, vmem=VMEM_LIMIT):
    return pltpu.CompilerParams(dimension_semantics=sem, vmem_limit_bytes=vmem)


def _silu(x):
    return x * jax.nn.sigmoid(x)


def _dot(a, b):
    return jnp.dot(a, b, preferred_element_type=F32)


def _dot_nt(a, b):
    return lax.dot_general(a, b, (((1,), (1,)), ((), ())), preferred_element_type=F32)


def _rms_norm_bf16(x, g):
    ms = jnp.mean(x * x, axis=-1, keepdims=True)
    return (x * lax.rsqrt(ms + EPS) * g).astype(BF16)


def _norm_proj_kernel(x_ref, g_ref, w_ref, y_ref, h_scr):
    @pl.when(pl.program_id(1) == 0)
    def _():
        h_scr[...] = _rms_norm_bf16(x_ref[...], g_ref[...])

    y_ref[...] = _dot(h_scr[...], w_ref[...]).astype(y_ref.dtype)


def _norm_proj(x, g, w, tm, tn):
    m, k = x.shape
    n = w.shape[1]
    tm = min(tm, m)
    return pl.pallas_call(
        _norm_proj_kernel,
        grid=(m // tm, n // tn),
        in_specs=[pl.BlockSpec((tm, k), lambda i, j: (i, 0)),
                  pl.BlockSpec((1, k), lambda i, j: (0, 0)),
                  pl.BlockSpec((k, tn), lambda i, j: (0, j))],
        out_specs=pl.BlockSpec((tm, tn), lambda i, j: (i, j)),
        out_shape=jax.ShapeDtypeStruct((m, n), BF16),
        scratch_shapes=[pltpu.VMEM((tm, k), BF16)],
        compiler_params=_params(("parallel", "arbitrary")),
        name="norm_proj",
    )(x, g.reshape(1, k), w)


def _in_proj_kernel(x_ref, g_ref, w_ref, wa_ref, y_ref, s5_ref, a_ref, h_scr, perm_scr):
    j = pl.program_id(1)
    tm, tn = y_ref.shape
    n_s5 = N_S5IN // tn

    @pl.when(j == 0)
    def _():
        h = _rms_norm_bf16(x_ref[...], g_ref[...])
        h_scr[...] = h
        a_ref[...] = _dot(h, wa_ref[...])

    @pl.when(j < n_s5)
    def _():
        acc = _dot(h_scr[...], w_ref[...])
        for c in range(tn // LANES):
            perm_scr[c] = acc[:, c * LANES:(c + 1) * LANES]
        for s in range(S5_T):
            for c in range(tn // LANES):
                rows = perm_scr[c, pl.ds(s, tm // S5_T, stride=S5_T), :]
                s5_ref[s, :, c * LANES:(c + 1) * LANES] = rows.astype(BF16)

    @pl.when(j >= n_s5)
    def _():
        y_ref[...] = _dot(h_scr[...], w_ref[...]).astype(BF16)


def _in_proj(x, g, w, w_aux, tm, tn):
    m, k = x.shape
    tm = min(tm, m)
    n_s5 = N_S5IN // tn
    return pl.pallas_call(
        _in_proj_kernel,
        grid=(m // tm, (N_S5IN + N_MAIN) // tn),
        in_specs=[pl.BlockSpec((tm, k), lambda i, j: (i, 0)),
                  pl.BlockSpec((1, k), lambda i, j: (0, 0)),
                  pl.BlockSpec((k, tn), lambda i, j: (0, j)),
                  pl.BlockSpec((k, LANES), lambda i, j: (0, 0))],
        out_specs=[pl.BlockSpec((tm, tn), lambda i, j: (i, jnp.maximum(j - n_s5, 0))),
                   pl.BlockSpec((S5_T, tm // S5_T, tn), lambda i, j: (0, i, jnp.minimum(j, n_s5 - 1))),
                   pl.BlockSpec((tm, LANES), lambda i, j: (i, 0))],
        out_shape=[jax.ShapeDtypeStruct((m, N_MAIN), BF16),
                   jax.ShapeDtypeStruct((S5_T, m // S5_T, N_S5IN), BF16),
                   jax.ShapeDtypeStruct((m, LANES), F32)],
        scratch_shapes=[pltpu.VMEM((tm, k), BF16), pltpu.VMEM((tn // LANES, tm, LANES), F32)],
        compiler_params=_params(("parallel", "arbitrary"), 56 * MIB),
        name="in_proj",
    )(x, g.reshape(1, k), w, w_aux)


def _s5_operators(lam_re, lam_im, log_dt, b_re, b_im, c_re, c_im):
    hi = lax.Precision.HIGHEST
    G, P, C, T, J, LG = S5_GROUPS, S5_STATE, S5_GROUP, S5_T, S5_NBLK, S5_LANE_GROUPS
    lam = lax.complex(jnp.minimum(lam_re.astype(F32), -1e-4), lam_im.astype(F32))
    dt = jnp.exp(log_dt.astype(F32))[:, None]
    z = lam * dt
    lam_bar = jnp.exp(z)
    b_bar = ((lam_bar - 1.0) / lam)[..., None] * lax.complex(b_re.astype(F32), b_im.astype(F32))
    c = lax.complex(c_re.astype(F32), c_im.astype(F32))
    steps = jnp.arange(T + 1, dtype=F32)
    pw = jnp.exp(z[None] * steps[:, None, None].astype(jnp.complex64))

    kern = jnp.real(jnp.einsum('gop,kgp,gpi->kgoi', c, pw[:T], b_bar, precision=hi))
    kc = kern.reshape(T, J, LG, C, C).transpose(1, 2, 4, 0, 3).reshape(J, LANES, T * C)

    bs = pw[T - 1 - jnp.arange(T)][..., None] * b_bar[None]
    bs = jnp.stack([jnp.real(bs), jnp.imag(bs)], axis=0)
    bs = bs.reshape(2, T, J, LG, P, C).transpose(2, 1, 3, 5, 0, 4)
    b2 = bs.reshape(J, T * LANES, 2 * P).astype(BF16)

    cs = c[None] * pw[1:T + 1][:, :, None, :]
    cs = jnp.stack([jnp.real(cs), -jnp.imag(cs)], axis=0)
    cs = cs.reshape(2, T, J, LG, C, P).transpose(2, 0, 3, 5, 1, 4)
    c2 = cs.reshape(J, 2 * S5_SBLK, T * C).astype(BF16)

    lam_t = pw[T].reshape(1, G * P)
    return kc, b2, c2, jnp.real(lam_t), jnp.imag(lam_t)


def _s5_expanders():
    r = jnp.arange(S5_T * S5_GROUP)[:, None]
    c = jnp.arange(S5_T * LANES)[None, :]
    e_to = ((r // S5_GROUP == c // LANES) & (r % S5_GROUP == c % S5_GROUP)).astype(BF16)
    r = jnp.arange(2 * S5_STATE)[:, None]
    c = jnp.arange(2 * S5_SBLK)[None, :]
    e_rp = ((r // S5_STATE == c // S5_SBLK) & (r % S5_STATE == c % S5_STATE)).astype(BF16)
    return e_to, e_rp


def _expand_block_diag(dst_scr, compact, exp_ref, row_group, col_group):
    rows = compact.shape[0]
    cols = exp_ref.shape[1]
    slab = 2 * LANES
    rg = (lax.broadcasted_iota(jnp.int32, (rows, slab), 0) // row_group) % S5_LANE_GROUPS
    cl = lax.broadcasted_iota(jnp.int32, (rows, slab), 1)
    for c0 in range(0, cols, slab):
        cg = ((cl + c0) // col_group) % S5_LANE_GROUPS
        blk = _dot(compact, exp_ref[:, c0:c0 + slab])
        dst_scr[:, c0:c0 + slab] = jnp.where(rg == cg, blk, 0.0).astype(BF16)


def _s5_local_kernel(u_ref, b2_ref, erp_ref, s_ref, min_scr):
    @pl.when(pl.program_id(1) == 0)
    def _():
        _expand_block_diag(min_scr, b2_ref[0], erp_ref, S5_GROUP, S5_STATE)

    ucat = jnp.concatenate([u_ref[s] for s in range(S5_T)], axis=1)
    s = _dot(ucat, min_scr[...])
    for c in range(2 * S5_SLAB):
        s_ref[c] = s[:, c * LANES:(c + 1) * LANES]


def _s5_scan_kernel(sr_ref, si_ref, ar_ref, ai_ref, xr_ref, xi_ref, *, bsz, nchunk):
    ar = [jnp.broadcast_to(ar_ref[:, c * LANES:(c + 1) * LANES], (bsz, LANES)) for c in range(S5_SLAB)]
    ai = [jnp.broadcast_to(ai_ref[:, c * LANES:(c + 1) * LANES], (bsz, LANES)) for c in range(S5_SLAB)]

    def step(k, carry):
        rows = pl.ds(k, bsz, stride=nchunk)
        new = []
        for c in range(S5_SLAB):
            xr, xi = carry[c]
            xr_ref[c, rows, :] = xr
            xi_ref[c, rows, :] = xi
            nr = ar[c] * xr - ai[c] * xi + sr_ref[c, rows, :]
            ni = ar[c] * xi + ai[c] * xr + si_ref[c, rows, :]
            new.append((nr, ni))
        return tuple(new)

    zero = jnp.zeros((bsz, LANES), F32)
    lax.fori_loop(0, nchunk, step, tuple((zero, zero) for _ in range(S5_SLAB)), unroll=4)


def _s5_out_kernel(u_ref, kc_ref, c2_ref, eto_ref, xr_ref, xi_ref, d_ref, out_ref, mi_scr, mo_scr, a2_scr):
    @pl.when(pl.program_id(1) == 0)
    def _():
        kc = kc_ref[0]
        lane = lax.broadcasted_iota(jnp.int32, kc.shape, 1)
        for s in range(S5_T):
            shifted = kc if s == 0 else jnp.where(lane >= s * S5_GROUP, pltpu.roll(kc, s * S5_GROUP, 1), 0.0)
            a2_scr[s * LANES:(s + 1) * LANES, :] = shifted.astype(BF16)
        _expand_block_diag(mi_scr, a2_scr[...], eto_ref, S5_GROUP, S5_GROUP)
        _expand_block_diag(mo_scr, c2_ref[0], eto_ref, S5_STATE, S5_GROUP)

    us = [u_ref[s] for s in range(S5_T)]
    ucat = jnp.concatenate(us, axis=1)
    xcat = jnp.concatenate([xr_ref[c] for c in range(S5_SLAB)] + [xi_ref[c] for c in range(S5_SLAB)],
                           axis=1).astype(BF16)
    d = d_ref[...]
    tile = 2 * LANES
    for c0 in range(0, S5_T * LANES, tile):
        y = _dot(ucat[:, :c0 + tile], mi_scr[:c0 + tile, c0:c0 + tile]) + _dot(xcat, mo_scr[:, c0:c0 + tile])
        for t in range(c0 // LANES, (c0 + tile) // LANES):
            yt = y[:, t * LANES - c0:(t + 1) * LANES - c0] + d * us[t].astype(F32)
            out_ref[t] = jax.nn.gelu(yt).astype(BF16)


def _s5_glu_kernel(g_ref, z_ref, w_ref, b_ref, out_ref):
    g = g_ref[...]
    a = _dot(g, w_ref[...]) + b_ref[...]
    out_ref[...] = (g.astype(F32) * jax.nn.sigmoid(a) * _silu(z_ref[...].astype(F32))).astype(BF16)


def _s5_branch(s5_in, bsz, seq, ops, layer, expanders, d_skip, w_glu, b_glu, rt):
    kc, b2, c2, lam_r, lam_i = ops
    e_to, e_rp = expanders
    nrow = bsz * seq // S5_T
    nchunk = seq // S5_T
    rt = min(rt, nrow)
    u_spec = pl.BlockSpec((S5_T, rt, LANES), lambda j, r: (0, r, j))

    def whole(a):
        return pl.BlockSpec(a.shape, lambda j, r: (0,) * a.ndim)

    def per_blk(a):
        return pl.BlockSpec((pl.Squeezed(), 1) + a.shape[2:], lambda j, r: (layer, j, 0, 0))

    s_loc = pl.pallas_call(
        _s5_local_kernel,
        grid=(S5_NBLK, nrow // rt),
        in_specs=[u_spec, per_blk(b2), whole(e_rp)],
        out_specs=pl.BlockSpec((2 * S5_SLAB, rt, LANES), lambda j, r: (j, r, 0)),
        out_shape=jax.ShapeDtypeStruct((S5_NBLK * 2 * S5_SLAB, nrow, LANES), F32),
        scratch_shapes=[pltpu.VMEM((S5_T * LANES, 2 * S5_SBLK), BF16)],
        compiler_params=_params(("parallel", "arbitrary")),
        name="s5_local",
    )(s5_in, b2, e_rp)

    xr, xi = pl.pallas_call(
        functools.partial(_s5_scan_kernel, bsz=bsz, nchunk=nchunk),
        grid=(S5_NBLK,),
        in_specs=[pl.BlockSpec((S5_SLAB, nrow, LANES), lambda j: (2 * j, 0, 0)),
                  pl.BlockSpec((S5_SLAB, nrow, LANES), lambda j: (2 * j + 1, 0, 0)),
                  pl.BlockSpec((pl.Squeezed(), 1, S5_SBLK), lambda j: (layer, 0, j)),
                  pl.BlockSpec((pl.Squeezed(), 1, S5_SBLK), lambda j: (layer, 0, j))],
        out_specs=[pl.BlockSpec((S5_SLAB, nrow, LANES), lambda j: (j, 0, 0))] * 2,
        out_shape=[jax.ShapeDtypeStruct((S5_NBLK * S5_SLAB, nrow, LANES), F32)] * 2,
        compiler_params=_params(("parallel",)),
        name="s5_scan",
    )(s_loc, s_loc, lam_r, lam_i)

    gel = pl.pallas_call(
        _s5_out_kernel,
        grid=(S5_NBLK, nrow // rt),
        in_specs=[u_spec, per_blk(kc), per_blk(c2), whole(e_to),
                  pl.BlockSpec((S5_SLAB, rt, LANES), lambda j, r: (j, r, 0)),
                  pl.BlockSpec((S5_SLAB, rt, LANES), lambda j, r: (j, r, 0)),
                  pl.BlockSpec((1, LANES), lambda j, r: (0, j))],
        out_specs=pl.BlockSpec((S5_T, rt, LANES), lambda j, r: (0, r, j)),
        out_shape=jax.ShapeDtypeStruct((S5_T, nrow, D_BRANCH), BF16),
        scratch_shapes=[pltpu.VMEM((S5_T * LANES, S5_T * LANES), BF16),
                        pltpu.VMEM((2 * S5_SBLK, S5_T * LANES), BF16),
                        pltpu.VMEM((S5_T * LANES, S5_T * S5_GROUP), BF16)],
        compiler_params=_params(("parallel", "arbitrary")),
        name="s5_out",
    )(s5_in, kc, c2, e_to, xr, xi, d_skip.reshape(1, D_BRANCH))

    row_blk = pl.BlockSpec((pl.Squeezed(), rt, D_BRANCH), lambda t, r: (t, r, 0))
    return pl.pallas_call(
        _s5_glu_kernel,
        grid=(S5_T, nrow // rt),
        in_specs=[row_blk,
                  pl.BlockSpec((pl.Squeezed(), rt, D_BRANCH), lambda t, r: (t, r, 1)),
                  pl.BlockSpec((D_BRANCH, D_BRANCH), lambda t, r: (0, 0)),
                  pl.BlockSpec((1, D_BRANCH), lambda t, r: (0, 0))],
        out_specs=row_blk,
        out_shape=jax.ShapeDtypeStruct((S5_T, nrow, D_BRANCH), BF16),
        compiler_params=_params(("parallel", "parallel")),
        name="s5_glu",
    )(gel, s5_in, w_glu.astype(BF16), b_glu.reshape(1, D_BRANCH).astype(F32))


def _split3(x):
    hi = x.astype(BF16)
    r1 = x - hi.astype(F32)
    mid = r1.astype(BF16)
    lo = (r1 - mid.astype(F32)).astype(BF16)
    return hi, mid, lo


def _mlstm_kernel(q_ref, k_ref, v_ref, o_ref, z_ref, if_ref, cw_ref, cb_ref, bif_ref, ng_ref, out_ref,
                  ext_scr, c_scr, n_scr, m_scr):
    lc, dh, nh = ML_CHUNK, ML_HEAD_DIM, ML_HEADS
    tail = ML_TAIL
    cidx = pl.program_id(1)

    @pl.when(cidx == 0)
    def _():
        ext_scr[0:tail, :] = jnp.zeros((tail, 2 * D_BRANCH), BF16)
        c_scr[...] = jnp.zeros_like(c_scr)
        n_scr[...] = jnp.zeros_like(n_scr)
        m_scr[...] = jnp.zeros_like(m_scr)

    @pl.when(cidx > 0)
    def _():
        ext_scr[0:tail, :] = ext_scr[lc:lc + tail, :]

    ext_scr[tail:tail + lc, 0:D_BRANCH] = q_ref[...]
    ext_scr[tail:tail + lc, D_BRANCH:2 * D_BRANCH] = k_ref[...]
    ext = ext_scr[...]
    srow = lax.broadcasted_iota(jnp.int32, (lc, tail + lc), 0)
    scol = lax.broadcasted_iota(jnp.int32, (lc, tail + lc), 1)
    conv = cb_ref[...] + cw_ref[ML_CONV - 1:ML_CONV, :] * ext[tail:tail + lc].astype(F32)
    for d in range(1, ML_CONV):
        shift = jnp.where(scol == srow + (tail - d), 1.0, 0.0).astype(BF16)
        conv = conv + cw_ref[ML_CONV - 1 - d:ML_CONV - d, :] * _dot(shift, ext)
    qk = _silu(conv)

    gts = if_ref[...] + bif_ref[...]
    lf = jnp.minimum(gts, 0.0) - jnp.log(1.0 + jnp.exp(-jnp.abs(gts)))
    gts_t = jnp.transpose(gts)[0:16, :]
    lf_t = jnp.transpose(lf)[0:16, :]
    row = lax.broadcasted_iota(jnp.int32, (lc, lc), 0)
    col = lax.broadcasted_iota(jnp.int32, (lc, lc), 1)
    causal = col <= row
    tril = jnp.where(causal, 1.0, 0.0).astype(BF16)
    triu = jnp.where(row <= col, 1.0, 0.0).astype(BF16)
    acum_c = sum(_dot(tril, p) for p in _split3(lf))
    acum_r = sum(_dot(p, triu) for p in _split3(lf_t))

    for h in range(nh):
        sl = slice(h * dh, (h + 1) * dh)
        qh = qk[:, h * dh:(h + 1) * dh]
        kh = qk[:, D_BRANCH + h * dh:D_BRANCH + (h + 1) * dh] * (dh ** -0.5)
        vh = v_ref[:, sl]
        qb = qh.astype(BF16)
        a_c = acum_c[:, nh + h:nh + h + 1]
        a_r = acum_r[nh + h:nh + h + 1, :]
        i_c = gts[:, h:h + 1]
        i_r = gts_t[h:h + 1, :]
        gtot = a_c[lc - 1:lc, :]
        m_old = m_scr[h][:, 0:1]
        dmat = jnp.where(causal, a_c - a_r + i_r, -jnp.inf)
        inter = a_c + m_old
        m_row = jnp.maximum(jnp.max(dmat, axis=-1, keepdims=True), inter)
        s = _dot_nt(qb, kh.astype(BF16)) * jnp.exp(dmat - m_row)
        sc = jnp.exp(inter - m_row)
        num = _dot(s.astype(BF16), vh) + sc * _dot(qb, c_scr[h].astype(BF16))
        den = jnp.sum(s, axis=-1, keepdims=True) + sc * jnp.sum(qh * n_scr[h], axis=-1, keepdims=True)
        hout = num / jnp.maximum(jnp.abs(den), jnp.exp(-m_row))

        kw_log = gtot - a_c + i_c
        m_new = jnp.maximum(gtot + m_old, jnp.max(kw_log, axis=0, keepdims=True))
        kw = jnp.exp(kw_log - m_new)
        decay = jnp.exp(gtot + m_old - m_new)
        kk = kh * kw
        c_scr[h] = decay * c_scr[h] + _dot(jnp.transpose(kk).astype(BF16), vh)
        n_scr[h] = decay * n_scr[h] + jnp.sum(kk, axis=0, keepdims=True)
        m_scr[h] = jnp.broadcast_to(m_new, (1, LANES))

        hn = hout * lax.rsqrt(jnp.mean(hout * hout, axis=-1, keepdims=True) + EPS) * ng_ref[:, sl]
        y = jax.nn.sigmoid(o_ref[:, sl].astype(F32)) * hn * _silu(z_ref[:, sl].astype(F32))
        out_ref[:, sl] = y.astype(BF16)


def _mlstm_branch(y_main, gates, bsz, seq, conv_w, conv_b, b_i, b_f, norm_g):
    lc = ML_CHUNK
    nc = seq // lc
    bif = jnp.zeros((1, LANES), F32).at[0, 0:ML_HEADS].set(b_i.astype(F32))
    bif = bif.at[0, ML_HEADS:2 * ML_HEADS].set(b_f.astype(F32))

    def col(cb):
        return pl.BlockSpec((lc, D_BRANCH), lambda b, c: (b * nc + c, cb))

    def const(shape):
        return pl.BlockSpec(shape, lambda b, c: (0,) * len(shape))

    return pl.pallas_call(
        _mlstm_kernel,
        grid=(bsz, nc),
        in_specs=[col(ML_BLK0), col(ML_BLK0 + 1), col(ML_BLK0 + 2), col(ML_BLK0 + 3), col(ML_BLK0 + 4),
                  pl.BlockSpec((lc, LANES), lambda b, c: (b * nc + c, 0)),
                  const((ML_CONV, 2 * D_BRANCH)), const((1, 2 * D_BRANCH)), const((1, LANES)),
                  const((1, D_BRANCH))],
        out_specs=pl.BlockSpec((lc, D_BRANCH), lambda b, c: (b * nc + c, 0)),
        out_shape=jax.ShapeDtypeStruct((bsz * seq, D_BRANCH), BF16),
        scratch_shapes=[pltpu.VMEM((lc + ML_TAIL, 2 * D_BRANCH), BF16),
                        pltpu.VMEM((ML_HEADS, ML_HEAD_DIM, ML_HEAD_DIM), F32),
                        pltpu.VMEM((ML_HEADS, 1, ML_HEAD_DIM), F32),
                        pltpu.VMEM((ML_HEADS, 1, LANES), F32)],
        compiler_params=_params(("parallel", "arbitrary")),
        name="mlstm",
    )(y_main, y_main, y_main, y_main, y_main, gates,
      conv_w.astype(F32), conv_b.reshape(1, 2 * D_BRANCH).astype(F32), bif,
      norm_g.reshape(1, D_BRANCH).astype(F32))


def _rope_tables(seq):
    half = DA_HEAD_DIM // 2
    inv = 1.0 / (ROPE_THETA ** (jnp.arange(0, DA_HEAD_DIM, 2, dtype=F32) / DA_HEAD_DIM))
    ang = jnp.arange(seq, dtype=F32)[:, None] * inv[None, :]
    cos, sin = jnp.cos(ang), jnp.sin(ang)
    cos_t = jnp.tile(jnp.concatenate([cos, cos], axis=1), (1, LANES // DA_HEAD_DIM))
    sin_t = jnp.tile(jnp.concatenate([-sin, sin], axis=1), (1, LANES // DA_HEAD_DIM))
    del half
    return cos_t, sin_t


def _rope(x, cos, sin):
    half = DA_HEAD_DIM // 2
    lane = lax.broadcasted_iota(jnp.int32, (1, LANES), 1)
    first_half = (lane % DA_HEAD_DIM) < half
    partner = jnp.where(first_half, pltpu.roll(x, LANES - half, 1), pltpu.roll(x, half, 1))
    return x * cos + partner * sin


def _diff_attn_kernel(q_ref, k_ref, v_ref, z_ref, cosq_ref, sinq_ref, cosk_ref, sink_ref,
                      lq1_ref, lk1_ref, lq2_ref, lk2_ref, g_ref,
                      out_ref, vt_scr, kr_scr, qt_scr, m_scr, acc_scr, *, lambda_init, tq, cg):
    qi = pl.program_id(2)
    nkb = vt_scr.shape[0]

    @pl.when(qi == 0)
    def _():
        kr_scr[...] = _rope(k_ref[...].astype(F32), cosk_ref[...], sink_ref[...]).astype(BF16)
        ones_row = lax.broadcasted_iota(jnp.int32, (VT_ROWS - DA_V_DIM, tq), 0) == 0
        for jb in range(nkb):
            vt_scr[jb, 0:DA_V_DIM, :] = jnp.transpose(v_ref[jb * tq:(jb + 1) * tq, :].astype(F32)).astype(BF16)
            vt_scr[jb, DA_V_DIM:VT_ROWS, :] = jnp.where(ones_row, 1.0, 0.0).astype(BF16)

    q = _rope(q_ref[...].astype(F32), cosq_ref[...], sinq_ref[...]) * (DA_HEAD_DIM ** -0.5 * LOG2E)
    comp0 = lax.broadcasted_iota(jnp.int32, (1, DA_V_DIM), 1) < DA_HEAD_DIM
    qt_scr[0] = jnp.transpose(jnp.where(comp0, q, 0.0)).astype(BF16)
    qt_scr[1] = jnp.transpose(jnp.where(comp0, 0.0, q)).astype(BF16)

    m_scr[...] = jnp.full_like(m_scr, -jnp.inf)
    acc_scr[...] = jnp.zeros_like(acc_scr)

    def stream(blocks):
        kbs = [kr_scr[pl.ds(pl.multiple_of(j * tq, tq), tq), :] for j, _ in blocks]
        vtbs = [vt_scr[j] for j, _ in blocks]
        chains = [(b, g, c) for b in range(len(blocks)) for g in range(tq // cg) for c in range(2)]

        def nkeys(b, g):
            return (g + 1) * cg if blocks[b][1] else tq

        def scores(b, g, c):
            return _dot(kbs[b][:nkeys(b, g)], qt_scr[c, :, g * cg:(g + 1) * cg])

        def softmax(b, g, c, s):
            cols = slice(g * cg, (g + 1) * cg)
            if blocks[b][1]:
                kpos = lax.broadcasted_iota(jnp.int32, s.shape, 0)
                qpos = lax.broadcasted_iota(jnp.int32, s.shape, 1) + g * cg
                s = jnp.where(kpos <= qpos, s, -jnp.inf)
            m_prev = m_scr[c, :, cols]
            m_new = jnp.maximum(m_prev, jnp.max(s, axis=0, keepdims=True))
            m_scr[c, :, cols] = m_new
            return jnp.exp2(m_prev - m_new), jnp.exp2(s - m_new).astype(BF16)

        def accumulate(b, g, c, alpha, p):
            cols = slice(g * cg, (g + 1) * cg)
            acc_scr[c, :, cols] = alpha * acc_scr[c, :, cols] + _dot(vtbs[b][:, :nkeys(b, g)], p)

        ahead, behind = 3, 2
        s_q = [scores(*chains[i]) for i in range(ahead)]
        pending = []
        for i, ch in enumerate(chains):
            s_cur = s_q.pop(0)
            if i + ahead < len(chains):
                s_q.append(scores(*chains[i + ahead]))
            alpha, p = softmax(*ch, s_cur)
            pending.append((*ch, alpha, p))
            if len(pending) > behind:
                accumulate(*pending.pop(0))
        for item in pending:
            accumulate(*item)

    def body(j, carry):
        stream([(j, False)])
        return carry

    lax.fori_loop(0, qi, body, 0)
    stream([(qi, True)])

    lam = (jnp.exp(jnp.sum(lq1_ref[...] * lk1_ref[...], axis=-1, keepdims=True))
           - jnp.exp(jnp.sum(lq2_ref[...] * lk2_ref[...], axis=-1, keepdims=True)) + lambda_init)
    o1 = acc_scr[0, 0:DA_V_DIM, :] / acc_scr[0, DA_V_DIM:DA_V_DIM + 1, :]
    o2 = acc_scr[1, 0:DA_V_DIM, :] / acc_scr[1, DA_V_DIM:DA_V_DIM + 1, :]
    o = jnp.transpose(o1 - lam * o2)
    o = o * lax.rsqrt(jnp.mean(o * o, axis=-1, keepdims=True) + EPS) * g_ref[...]
    o = o * (1.0 - lambda_init) * _silu(z_ref[...].astype(F32))
    out_ref[...] = o.astype(BF16)


def _diff_attn_branch(y_main, bsz, seq, cos_t, sin_t, lq1, lk1, lq2, lk2, subln_g, lambda_init, tq, cg):
    ntok = bsz * seq
    tq = min(tq, seq)
    nq = seq // tq
    per_blk = D_BRANCH // DA_V_DIM

    def tile(blk):
        return pl.BlockSpec((tq, DA_V_DIM), lambda b, h, i: (b * nq + i, blk * per_blk + h))

    def full(blk):
        return pl.BlockSpec((seq, DA_V_DIM), lambda b, h, i: (b, blk * per_blk + h))

    def vec(n):
        return pl.BlockSpec((1, n), lambda b, h, i: (0, 0))

    q_tab = pl.BlockSpec((tq, LANES), lambda b, h, i: (i, 0))
    k_tab = pl.BlockSpec((seq, LANES), lambda b, h, i: (0, 0))
    lvec = [a.reshape(1, DA_HEAD_DIM).astype(F32) for a in (lq1, lk1, lq2, lk2)]
    return pl.pallas_call(
        functools.partial(_diff_attn_kernel, lambda_init=lambda_init, tq=tq, cg=min(cg, tq)),
        grid=(bsz, DA_HEADS, nq),
        in_specs=[tile(DA_BLK0), full(DA_BLK0 + 1), full(DA_BLK0 + 2), tile(DA_BLK0 + 3),
                  q_tab, q_tab, k_tab, k_tab,
                  vec(DA_HEAD_DIM), vec(DA_HEAD_DIM), vec(DA_HEAD_DIM), vec(DA_HEAD_DIM), vec(DA_V_DIM)],
        out_specs=pl.BlockSpec((tq, DA_V_DIM), lambda b, h, i: (b * nq + i, h)),
        out_shape=jax.ShapeDtypeStruct((ntok, D_BRANCH), BF16),
        scratch_shapes=[pltpu.VMEM((nq, VT_ROWS, tq), BF16), pltpu.VMEM((seq, DA_V_DIM), BF16),
                        pltpu.VMEM((2, DA_V_DIM, tq), BF16),
                        pltpu.VMEM((2, 1, tq), F32), pltpu.VMEM((2, VT_ROWS, tq), F32)],
        compiler_params=_params(("parallel", "parallel", "arbitrary")),
        name="diff_attn",
    )(y_main, y_main, y_main, y_main, cos_t, sin_t, cos_t, sin_t, *lvec,
      subln_g.reshape(1, DA_V_DIM).astype(F32))


def _mem_attn_kernel(q_ref, z_ref, k_ref, v_ref, out_ref):
    dh = XA_HEAD_DIM
    for h in range(XA_HEADS):
        sl = slice(h * dh, (h + 1) * dh)
        s = _dot_nt(q_ref[:, sl], k_ref[:, sl]) * (dh ** -0.5)
        p = jnp.exp(s - jnp.max(s, axis=-1, keepdims=True))
        p = p / jnp.sum(p, axis=-1, keepdims=True)
        o = _dot(p.astype(BF16), v_ref[:, sl])
        out_ref[:, sl] = (o * _silu(z_ref[:, sl].astype(F32))).astype(BF16)


def _mem_attn_branch(y_main, kv, bsz, seq, mem_len, tq):
    tq = min(tq, seq)
    nq = seq // tq
    return pl.pallas_call(
        _mem_attn_kernel,
        grid=(bsz, nq),
        in_specs=[pl.BlockSpec((tq, D_BRANCH), lambda b, i: (b * nq + i, XA_BLK0)),
                  pl.BlockSpec((tq, D_BRANCH), lambda b, i: (b * nq + i, XA_BLK0 + 1)),
                  pl.BlockSpec((mem_len, D_BRANCH), lambda b, i: (b, 0)),
                  pl.BlockSpec((mem_len, D_BRANCH), lambda b, i: (b, 1))],
        out_specs=pl.BlockSpec((tq, D_BRANCH), lambda b, i: (b * nq + i, 0)),
        out_shape=jax.ShapeDtypeStruct((bsz * seq, D_BRANCH), BF16),
        compiler_params=_params(("parallel", "parallel")),
        name="mem_attn",
    )(y_main, y_main, kv, kv)


def _merge_kernel(*refs):
    y_refs = refs[0:N_BRANCH]
    g_refs = refs[N_BRANCH:2 * N_BRANCH]
    w_refs = refs[2 * N_BRANCH:3 * N_BRANCH]
    out_ref, y0_scr, perm_scr = refs[3 * N_BRANCH:]

    @pl.when(pl.program_id(1) == 0)
    def _():
        rows = y_refs[0].shape[1]
        nslab = D_BRANCH // LANES
        for s in range(S5_T):
            ys = y_refs[0][s].astype(F32)
            for c in range(nslab):
                perm_scr[c, pl.ds(s, rows, stride=S5_T), :] = ys[:, c * LANES:(c + 1) * LANES]
        for c in range(nslab):
            y0_scr[:, c * LANES:(c + 1) * LANES] = perm_scr[c].astype(BF16)

    acc = jax.nn.sigmoid(g_refs[0][...].astype(F32)) * _dot(y0_scr[...], w_refs[0][0])
    for y_ref, g_ref, w_ref in zip(y_refs[1:], g_refs[1:], w_refs[1:]):
        acc = acc + jax.nn.sigmoid(g_ref[...].astype(F32)) * _dot(y_ref[...], w_ref[0])
    out_ref[...] = acc.astype(BF16)


def _merge(branches, y_main, w_branch, tm, tn):
    ntok = y_main.shape[0]
    tm = min(tm, ntok)
    ncol = D_MODEL // tn
    y_specs = ([pl.BlockSpec((S5_T, tm // S5_T, D_BRANCH), lambda i, n: (0, i, 0))]
               + [pl.BlockSpec((tm, D_BRANCH), lambda i, n: (i, 0))] * (N_BRANCH - 1))
    g_specs = [pl.BlockSpec((tm, tn), lambda i, n, b=b: (i, (GATE_COL0 + b * D_MODEL) // tn + n))
               for b in range(N_BRANCH)]
    w_specs = [pl.BlockSpec((1, D_BRANCH, tn), lambda i, n, b=b: (b, 0, n)) for b in range(N_BRANCH)]
    return pl.pallas_call(
        _merge_kernel,
        grid=(ntok // tm, ncol),
        in_specs=y_specs + g_specs + w_specs,
        out_specs=pl.BlockSpec((tm, tn), lambda i, n: (i, n)),
        out_shape=jax.ShapeDtypeStruct((ntok, D_MODEL), BF16),
        scratch_shapes=[pltpu.VMEM((tm, D_BRANCH), BF16), pltpu.VMEM((D_BRANCH // LANES, tm, LANES), F32)],
        compiler_params=_params(("parallel", "arbitrary")),
        name="merge",
    )(*branches, *([y_main] * N_BRANCH), *([w_branch] * N_BRANCH))


def _out_proj_kernel(m_ref, w_ref, g_ref, x_ref, out_ref):
    o = _dot(m_ref[...], w_ref[...])
    o = o * lax.rsqrt(jnp.mean(o * o, axis=-1, keepdims=True) + EPS) * g_ref[...]
    out_ref[...] = x_ref[...] + o


def _out_proj(merged, w_out, g_post, x, tm):
    ntok = x.shape[0]
    tm = min(tm, ntok)
    return pl.pallas_call(
        _out_proj_kernel,
        grid=(ntok // tm,),
        in_specs=[pl.BlockSpec((tm, D_MODEL), lambda i: (i, 0)),
                  pl.BlockSpec((D_MODEL, D_MODEL), lambda i: (0, 0)),
                  pl.BlockSpec((1, D_MODEL), lambda i: (0, 0)),
                  pl.BlockSpec((tm, D_MODEL), lambda i: (i, 0))],
        out_specs=pl.BlockSpec((tm, D_MODEL), lambda i: (i, 0)),
        out_shape=jax.ShapeDtypeStruct((ntok, D_MODEL), F32),
        compiler_params=_params(("parallel",)),
        name="out_proj",
    )(merged, w_out, g_post.reshape(1, D_MODEL).astype(F32), x)


def kernel(x, mem, g_pre, w_in, s5_lam_re, s5_lam_im, s5_log_dt, s5_b_re, s5_b_im, s5_c_re, s5_c_im, s5_d, s5_w_glu, s5_b_glu, ml_conv_w, ml_conv_b, ml_b_i, ml_b_f, ml_norm_g, da_lq1, da_lk1, da_lq2, da_lk2, da_subln_g, g_mem, xa_w_kv, w_branch, w_out, g_post):
    bsz, seq, _ = x.shape
    mem_len = mem.shape[1]
    depth = w_in.shape[0]
    ntok = bsz * seq
    cos_t, sin_t = _rope_tables(seq)
    xf = x.reshape(ntok, D_MODEL).astype(F32)
    memf = mem.reshape(bsz * mem_len, D_MODEL).astype(F32)
    gate0 = 7 * D_BRANCH
    expanders = _s5_expanders()
    s5_ops = jax.vmap(_s5_operators)(s5_lam_re, s5_lam_im, s5_log_dt, s5_b_re, s5_b_im, s5_c_re, s5_c_im)
    for l in range(depth):
        lambda_init = 0.8 - 0.6 * math.exp(-0.3 * l)
        w_l = w_in[l]
        w_main = jnp.concatenate([w_l[:, :gate0], w_l[:, gate0 + 2 * ML_HEADS:]], axis=1).astype(BF16)
        w_gate = jnp.pad(w_l[:, gate0:gate0 + 2 * ML_HEADS], ((0, 0), (0, LANES - 2 * ML_HEADS))).astype(BF16)
        y_main, s5_in, gates = _in_proj(xf, g_pre[l], w_main, w_gate, tm=1024, tn=1024)
        kv = _norm_proj(memf, g_mem[l], xa_w_kv[l].astype(BF16), tm=1024, tn=1024)

        y_s5 = _s5_branch(s5_in, bsz, seq, s5_ops, l, expanders, s5_d[l].astype(F32), s5_w_glu[l], s5_b_glu[l],
                          rt=512)
        y_ml = _mlstm_branch(y_main, gates, bsz, seq, ml_conv_w[l], ml_conv_b[l], ml_b_i[l], ml_b_f[l],
                             ml_norm_g[l])
        y_da = _diff_attn_branch(y_main, bsz, seq, cos_t, sin_t, da_lq1[l], da_lk1[l], da_lq2[l], da_lk2[l],
                                 da_subln_g[l], lambda_init, tq=1024, cg=256)
        y_xa = _mem_attn_branch(y_main, kv, bsz, seq, mem_len, tq=512)

        merged = _merge((y_s5, y_ml, y_da, y_xa), y_main, w_branch[l].astype(BF16), tm=1024, tn=512)
        xf = _out_proj(merged, w_out[l].astype(BF16), g_post[l], xf, tm=512)
    return xf.reshape(bsz, seq, D_MODEL).astype(x.dtype)
```

```python
import functools
import math

import jax
import jax.numpy as jnp
from jax import lax
from jax.experimental import pallas as pl
from jax.experimental.pallas import tpu as pltpu

F32 = jnp.float32
BF16 = jnp.bfloat16

D_MODEL = 2048
D_BRANCH = 1024
N_BRANCH = 4
S5_GROUP = 16
S5_GROUPS = D_BRANCH // S5_GROUP
S5_STATE = 64
ML_HEADS = 4
ML_HEAD_DIM = D_BRANCH // ML_HEADS
ML_CHUNK = 128
ML_CONV = 4
ML_TAIL = 16
DA_HEADS = 8
DA_HEAD_DIM = 64
DA_V_DIM = 2 * DA_HEAD_DIM
XA_HEADS = 4
XA_HEAD_DIM = D_BRANCH // XA_HEADS
ROPE_THETA = 10000.0
EPS = 1e-6
LOG2E = 1.4426950408889634

LANES = 128
VT_ROWS = DA_V_DIM + 16
S5_T = 16
S5_LANE_GROUPS = LANES // S5_GROUP
S5_NBLK = D_BRANCH // LANES
S5_SBLK = S5_LANE_GROUPS * S5_STATE
S5_SLAB = S5_SBLK // LANES
N_S5IN = 2 * D_BRANCH
N_MAIN = 11 * D_BRANCH + N_BRANCH * D_MODEL
ML_BLK0, DA_BLK0, XA_BLK0 = 0, 5, 9
GATE_COL0 = 11 * D_BRANCH
MIB = 1024 * 1024
VMEM_LIMIT = 48 * MIB


def _params(sem, vmem=VMEM_LIMIT):
    return pltpu.CompilerParams(dimension_semantics=sem, vmem_limit_bytes=vmem)


def _silu(x):
    return x * jax.nn.sigmoid(x)


def _dot(a, b):
    return jnp.dot(a, b, preferred_element_type=F32)


def _dot_nt(a, b):
    return lax.dot_general(a, b, (((1,), (1,)), ((), ())), preferred_element_type=F32)


def _rms_norm_bf16(x, g):
    ms = jnp.mean(x * x, axis=-1, keepdims=True)
    return (x * lax.rsqrt(ms + EPS) * g).astype(BF16)


def _norm_proj_kernel(x_ref, g_ref, w_ref, y_ref, h_scr):
    @pl.when(pl.program_id(1) == 0)
    def _():
        h_scr[...] = _rms_norm_bf16(x_ref[...], g_ref[...])

    y_ref[...] = _dot(h_scr[...], w_ref[...]).astype(y_ref.dtype)


def _norm_proj(x, g, w, layer, tm, tn):
    m, k = x.shape
    n = w.shape[2]
    tm = min(tm, m)
    return pl.pallas_call(
        _norm_proj_kernel,
        grid=(m // tm, n // tn),
        in_specs=[pl.BlockSpec((tm, k), lambda i, j: (i, 0)),
                  pl.BlockSpec((1, k), lambda i, j: (0, 0)),
                  pl.BlockSpec((pl.Squeezed(), k, tn), lambda i, j: (layer, 0, j))],
        out_specs=pl.BlockSpec((tm, tn), lambda i, j: (i, j)),
        out_shape=jax.ShapeDtypeStruct((m, n), BF16),
        scratch_shapes=[pltpu.VMEM((tm, k), BF16)],
        compiler_params=_params(("parallel", "arbitrary")),
        name="norm_proj",
    )(x, g.reshape(1, k), w)


def _in_proj_kernel(x_ref, g_ref, w_ref, wa_ref, y_ref, s5_ref, a_ref, h_scr, perm_scr):
    j = pl.program_id(1)
    tm, tn = y_ref.shape
    n_s5 = N_S5IN // tn

    @pl.when(j == 0)
    def _():
        h = _rms_norm_bf16(x_ref[...], g_ref[...])
        h_scr[...] = h
        a_ref[...] = _dot(h, wa_ref[...])

    @pl.when(j < n_s5)
    def _():
        acc = _dot(h_scr[...], w_ref[...])
        for c in range(tn // LANES):
            perm_scr[c] = acc[:, c * LANES:(c + 1) * LANES]
        for s in range(S5_T):
            for c in range(tn // LANES):
                rows = perm_scr[c, pl.ds(s, tm // S5_T, stride=S5_T), :]
                s5_ref[s, :, c * LANES:(c + 1) * LANES] = rows.astype(BF16)

    @pl.when(j >= n_s5)
    def _():
        y_ref[...] = _dot(h_scr[...], w_ref[...]).astype(BF16)


def _in_proj(x, g, w, w_aux, tm, tn):
    m, k = x.shape
    tm = min(tm, m)
    n_s5 = N_S5IN // tn
    return pl.pallas_call(
        _in_proj_kernel,
        grid=(m // tm, (N_S5IN + N_MAIN) // tn),
        in_specs=[pl.BlockSpec((tm, k), lambda i, j: (i, 0)),
                  pl.BlockSpec((1, k), lambda i, j: (0, 0)),
                  pl.BlockSpec((k, tn), lambda i, j: (0, j)),
                  pl.BlockSpec((k, LANES), lambda i, j: (0, 0))],
        out_specs=[pl.BlockSpec((tm, tn), lambda i, j: (i, jnp.maximum(j - n_s5, 0))),
                   pl.BlockSpec((S5_T, tm // S5_T, tn), lambda i, j: (0, i, jnp.minimum(j, n_s5 - 1))),
                   pl.BlockSpec((tm, LANES), lambda i, j: (i, 0))],
        out_shape=[jax.ShapeDtypeStruct((m, N_MAIN), BF16),
                   jax.ShapeDtypeStruct((S5_T, m // S5_T, N_S5IN), BF16),
                   jax.ShapeDtypeStruct((m, LANES), F32)],
        scratch_shapes=[pltpu.VMEM((tm, k), BF16), pltpu.VMEM((tn // LANES, tm, LANES), F32)],
        compiler_params=_params(("parallel", "arbitrary"), 56 * MIB),
        name="in_proj",
    )(x, g.reshape(1, k), w, w_aux)


def _s5_operators(lam_re, lam_im, log_dt, b_re, b_im, c_re, c_im):
    hi = lax.Precision.HIGHEST
    G, P, C, T, J, LG = S5_GROUPS, S5_STATE, S5_GROUP, S5_T, S5_NBLK, S5_LANE_GROUPS
    lam = lax.complex(jnp.minimum(lam_re.astype(F32), -1e-4), lam_im.astype(F32))
    dt = jnp.exp(log_dt.astype(F32))[:, None]
    z = lam * dt
    lam_bar = jnp.exp(z)
    b_bar = ((lam_bar - 1.0) / lam)[..., None] * lax.complex(b_re.astype(F32), b_im.astype(F32))
    c = lax.complex(c_re.astype(F32), c_im.astype(F32))
    steps = jnp.arange(T + 1, dtype=F32)
    pw = jnp.exp(z[None] * steps[:, None, None].astype(jnp.complex64))

    kern = jnp.real(jnp.einsum('gop,kgp,gpi->kgoi', c, pw[:T], b_bar, precision=hi))
    kc = kern.reshape(T, J, LG, C, C).transpose(1, 2, 4, 0, 3).reshape(J, LANES, T * C)

    bs = pw[T - 1 - jnp.arange(T)][..., None] * b_bar[None]
    bs = jnp.stack([jnp.real(bs), jnp.imag(bs)], axis=0)
    bs = bs.reshape(2, T, J, LG, P, C).transpose(2, 1, 3, 5, 0, 4)
    b2 = bs.reshape(J, T * LANES, 2 * P).astype(BF16)

    cs = c[None] * pw[1:T + 1][:, :, None, :]
    cs = jnp.stack([jnp.real(cs), -jnp.imag(cs)], axis=0)
    cs = cs.reshape(2, T, J, LG, C, P).transpose(2, 0, 3, 5, 1, 4)
    c2 = cs.reshape(J, 2 * S5_SBLK, T * C).astype(BF16)

    lam_t = pw[T].reshape(1, G * P)
    return kc, b2, c2, jnp.real(lam_t), jnp.imag(lam_t)


def _s5_expanders():
    r = jnp.arange(S5_T * S5_GROUP)[:, None]
    c = jnp.arange(S5_T * LANES)[None, :]
    e_to = ((r // S5_GROUP == c // LANES) & (r % S5_GROUP == c % S5_GROUP)).astype(BF16)
    r = jnp.arange(2 * S5_STATE)[:, None]
    c = jnp.arange(2 * S5_SBLK)[None, :]
    e_rp = ((r // S5_STATE == c // S5_SBLK) & (r % S5_STATE == c % S5_STATE)).astype(BF16)
    return e_to, e_rp


def _expand_block_diag(dst_scr, compact, exp_ref, row_group, col_group):
    rows = compact.shape[0]
    cols = exp_ref.shape[1]
    slab = 2 * LANES
    rg = (lax.broadcasted_iota(jnp.int32, (rows, slab), 0) // row_group) % S5_LANE_GROUPS
    cl = lax.broadcasted_iota(jnp.int32, (rows, slab), 1)
    for c0 in range(0, cols, slab):
        cg = ((cl + c0) // col_group) % S5_LANE_GROUPS
        blk = _dot(compact, exp_ref[:, c0:c0 + slab])
        dst_scr[:, c0:c0 + slab] = jnp.where(rg == cg, blk, 0.0).astype(BF16)


def _s5_local_kernel(u_ref, b2_ref, erp_ref, s_ref, min_scr):
    @pl.when(pl.program_id(1) == 0)
    def _():
        _expand_block_diag(min_scr, b2_ref[0], erp_ref, S5_GROUP, S5_STATE)

    ucat = jnp.concatenate([u_ref[s] for s in range(S5_T)], axis=1)
    s = _dot(ucat, min_scr[...])
    for c in range(2 * S5_SLAB):
        s_ref[c] = s[:, c * LANES:(c + 1) * LANES]


def _s5_scan_kernel(sr_ref, si_ref, ar_ref, ai_ref, xr_ref, xi_ref, *, bsz, nchunk):
    ar = [jnp.broadcast_to(ar_ref[:, c * LANES:(c + 1) * LANES], (bsz, LANES)) for c in range(S5_SLAB)]
    ai = [jnp.broadcast_to(ai_ref[:, c * LANES:(c + 1) * LANES], (bsz, LANES)) for c in range(S5_SLAB)]

    def step(k, carry):
        rows = pl.ds(k, bsz, stride=nchunk)
        new = []
        for c in range(S5_SLAB):
            xr, xi = carry[c]
            xr_ref[c, rows, :] = xr
            xi_ref[c, rows, :] = xi
            nr = ar[c] * xr - ai[c] * xi + sr_ref[c, rows, :]
            ni = ar[c] * xi + ai[c] * xr + si_ref[c, rows, :]
            new.append((nr, ni))
        return tuple(new)

    zero = jnp.zeros((bsz, LANES), F32)
    lax.fori_loop(0, nchunk, step, tuple((zero, zero) for _ in range(S5_SLAB)), unroll=4)


def _s5_out_kernel(u_ref, kc_ref, c2_ref, eto_ref, xr_ref, xi_ref, d_ref, out_ref, mi_scr, mo_scr, a2_scr):
    @pl.when(pl.program_id(1) == 0)
    def _():
        kc = kc_ref[0]
        lane = lax.broadcasted_iota(jnp.int32, kc.shape, 1)
        for s in range(S5_T):
            shifted = kc if s == 0 else jnp.where(lane >= s * S5_GROUP, pltpu.roll(kc, s * S5_GROUP, 1), 0.0)
            a2_scr[s * LANES:(s + 1) * LANES, :] = shifted.astype(BF16)
        _expand_block_diag(mi_scr, a2_scr[...], eto_ref, S5_GROUP, S5_GROUP)
        _expand_block_diag(mo_scr, c2_ref[0], eto_ref, S5_STATE, S5_GROUP)

    us = [u_ref[s] for s in range(S5_T)]
    ucat = jnp.concatenate(us, axis=1)
    xcat = jnp.concatenate([xr_ref[c] for c in range(S5_SLAB)] + [xi_ref[c] for c in range(S5_SLAB)],
                           axis=1).astype(BF16)
    d = d_ref[...]
    tile = 2 * LANES
    for c0 in range(0, S5_T * LANES, tile):
        y = _dot(ucat[:, :c0 + tile], mi_scr[:c0 + tile, c0:c0 + tile]) + _dot(xcat, mo_scr[:, c0:c0 + tile])
        for t in range(c0 // LANES, (c0 + tile) // LANES):
            yt = y[:, t * LANES - c0:(t + 1) * LANES - c0] + d * us[t].astype(F32)
            out_ref[t] = jax.nn.gelu(yt).astype(BF16)


def _s5_glu_kernel(g_ref, z_ref, w_ref, b_ref, out_ref):
    g = g_ref[...]
    a = _dot(g, w_ref[...]) + b_ref[...]
    out_ref[...] = (g.astype(F32) * jax.nn.sigmoid(a) * _silu(z_ref[...].astype(F32))).astype(BF16)


def _s5_branch(s5_in, bsz, seq, ops, layer, expanders, d_skip, w_glu, b_glu, rt):
    kc, b2, c2, lam_r, lam_i = ops
    e_to, e_rp = expanders
    nrow = bsz * seq // S5_T
    nchunk = seq // S5_T
    rt = min(rt, nrow)
    u_spec = pl.BlockSpec((S5_T, rt, LANES), lambda j, r: (0, r, j))

    def whole(a):
        return pl.BlockSpec(a.shape, lambda j, r: (0,) * a.ndim)

    def per_blk(a):
        return pl.BlockSpec((pl.Squeezed(), 1) + a.shape[2:], lambda j, r: (layer, j, 0, 0))

    s_loc = pl.pallas_call(
        _s5_local_kernel,
        grid=(S5_NBLK, nrow // rt),
        in_specs=[u_spec, per_blk(b2), whole(e_rp)],
        out_specs=pl.BlockSpec((2 * S5_SLAB, rt, LANES), lambda j, r: (j, r, 0)),
        out_shape=jax.ShapeDtypeStruct((S5_NBLK * 2 * S5_SLAB, nrow, LANES), F32),
        scratch_shapes=[pltpu.VMEM((S5_T * LANES, 2 * S5_SBLK), BF16)],
        compiler_params=_params(("parallel", "arbitrary")),
        name="s5_local",
    )(s5_in, b2, e_rp)

    xr, xi = pl.pallas_call(
        functools.partial(_s5_scan_kernel, bsz=bsz, nchunk=nchunk),
        grid=(S5_NBLK,),
        in_specs=[pl.BlockSpec((S5_SLAB, nrow, LANES), lambda j: (2 * j, 0, 0)),
                  pl.BlockSpec((S5_SLAB, nrow, LANES), lambda j: (2 * j + 1, 0, 0)),
                  pl.BlockSpec((pl.Squeezed(), 1, S5_SBLK), lambda j: (layer, 0, j)),
                  pl.BlockSpec((pl.Squeezed(), 1, S5_SBLK), lambda j: (layer, 0, j))],
        out_specs=[pl.BlockSpec((S5_SLAB, nrow, LANES), lambda j: (j, 0, 0))] * 2,
        out_shape=[jax.ShapeDtypeStruct((S5_NBLK * S5_SLAB, nrow, LANES), F32)] * 2,
        compiler_params=_params(("parallel",)),
        name="s5_scan",
    )(s_loc, s_loc, lam_r, lam_i)

    gel = pl.pallas_call(
        _s5_out_kernel,
        grid=(S5_NBLK, nrow // rt),
        in_specs=[u_spec, per_blk(kc), per_blk(c2), whole(e_to),
                  pl.BlockSpec((S5_SLAB, rt, LANES), lambda j, r: (j, r, 0)),
                  pl.BlockSpec((S5_SLAB, rt, LANES), lambda j, r: (j, r, 0)),
                  pl.BlockSpec((1, LANES), lambda j, r: (0, j))],
        out_specs=pl.BlockSpec((S5_T, rt, LANES), lambda j, r: (0, r, j)),
        out_shape=jax.ShapeDtypeStruct((S5_T, nrow, D_BRANCH), BF16),
        scratch_shapes=[pltpu.VMEM((S5_T * LANES, S5_T * LANES), BF16),
                        pltpu.VMEM((2 * S5_SBLK, S5_T * LANES), BF16),
                        pltpu.VMEM((S5_T * LANES, S5_T * S5_GROUP), BF16)],
        compiler_params=_params(("parallel", "arbitrary")),
        name="s5_out",
    )(s5_in, kc, c2, e_to, xr, xi, d_skip.reshape(1, D_BRANCH))

    row_blk = pl.BlockSpec((pl.Squeezed(), rt, D_BRANCH), lambda t, r: (t, r, 0))
    return pl.pallas_call(
        _s5_glu_kernel,
        grid=(S5_T, nrow // rt),
        in_specs=[row_blk,
                  pl.BlockSpec((pl.Squeezed(), rt, D_BRANCH), lambda t, r: (t, r, 1)),
                  pl.BlockSpec((pl.Squeezed(), D_BRANCH, D_BRANCH), lambda t, r: (layer, 0, 0)),
                  pl.BlockSpec((1, D_BRANCH), lambda t, r: (0, 0))],
        out_specs=row_blk,
        out_shape=jax.ShapeDtypeStruct((S5_T, nrow, D_BRANCH), BF16),
        compiler_params=_params(("parallel", "parallel")),
        name="s5_glu",
    )(gel, s5_in, w_glu, b_glu.reshape(1, D_BRANCH).astype(F32))


def _split3(x):
    hi = x.astype(BF16)
    r1 = x - hi.astype(F32)
    mid = r1.astype(BF16)
    lo = (r1 - mid.astype(F32)).astype(BF16)
    return hi, mid, lo


def _mlstm_kernel(q_ref, k_ref, v_ref, o_ref, z_ref, if_ref, cw_ref, cb_ref, bif_ref, ng_ref, out_ref,
                  ext_scr, c_scr, n_scr, m_scr):
    lc, dh, nh = ML_CHUNK, ML_HEAD_DIM, ML_HEADS
    tail = ML_TAIL
    cidx = pl.program_id(1)

    @pl.when(cidx == 0)
    def _():
        ext_scr[0:tail, :] = jnp.zeros((tail, 2 * D_BRANCH), BF16)
        c_scr[...] = jnp.zeros_like(c_scr)
        n_scr[...] = jnp.zeros_like(n_scr)
        m_scr[...] = jnp.zeros_like(m_scr)

    @pl.when(cidx > 0)
    def _():
        ext_scr[0:tail, :] = ext_scr[lc:lc + tail, :]

    ext_scr[tail:tail + lc, 0:D_BRANCH] = q_ref[...]
    ext_scr[tail:tail + lc, D_BRANCH:2 * D_BRANCH] = k_ref[...]
    ext = ext_scr[...]
    srow = lax.broadcasted_iota(jnp.int32, (lc, tail + lc), 0)
    scol = lax.broadcasted_iota(jnp.int32, (lc, tail + lc), 1)
    conv = cb_ref[...] + cw_ref[ML_CONV - 1:ML_CONV, :] * ext[tail:tail + lc].astype(F32)
    for d in range(1, ML_CONV):
        shift = jnp.where(scol == srow + (tail - d), 1.0, 0.0).astype(BF16)
        conv = conv + cw_ref[ML_CONV - 1 - d:ML_CONV - d, :] * _dot(shift, ext)
    qk = _silu(conv)

    gts = if_ref[...] + bif_ref[...]
    lf = jnp.minimum(gts, 0.0) - jnp.log(1.0 + jnp.exp(-jnp.abs(gts)))
    gts_t = jnp.transpose(gts)[0:16, :]
    lf_t = jnp.transpose(lf)[0:16, :]
    row = lax.broadcasted_iota(jnp.int32, (lc, lc), 0)
    col = lax.broadcasted_iota(jnp.int32, (lc, lc), 1)
    causal = col <= row
    tril = jnp.where(causal, 1.0, 0.0).astype(BF16)
    triu = jnp.where(row <= col, 1.0, 0.0).astype(BF16)
    acum_c = sum(_dot(tril, p) for p in _split3(lf))
    acum_r = sum(_dot(p, triu) for p in _split3(lf_t))

    for h in range(nh):
        sl = slice(h * dh, (h + 1) * dh)
        qh = qk[:, h * dh:(h + 1) * dh]
        kh = qk[:, D_BRANCH + h * dh:D_BRANCH + (h + 1) * dh] * (dh ** -0.5)
        vh = v_ref[:, sl]
        qb = qh.astype(BF16)
        a_c = acum_c[:, nh + h:nh + h + 1]
        a_r = acum_r[nh + h:nh + h + 1, :]
        i_c = gts[:, h:h + 1]
        i_r = gts_t[h:h + 1, :]
        gtot = a_c[lc - 1:lc, :]
        m_old = m_scr[h][:, 0:1]
        dmat = jnp.where(causal, a_c - a_r + i_r, -jnp.inf)
        inter = a_c + m_old
        m_row = jnp.maximum(jnp.max(dmat, axis=-1, keepdims=True), inter)
        s = _dot_nt(qb, kh.astype(BF16)) * jnp.exp(dmat - m_row)
        sc = jnp.exp(inter - m_row)
        num = _dot(s.astype(BF16), vh) + sc * _dot(qb, c_scr[h].astype(BF16))
        den = jnp.sum(s, axis=-1, keepdims=True) + sc * jnp.sum(qh * n_scr[h], axis=-1, keepdims=True)
        hout = num / jnp.maximum(jnp.abs(den), jnp.exp(-m_row))

        kw_log = gtot - a_c + i_c
        m_new = jnp.maximum(gtot + m_old, jnp.max(kw_log, axis=0, keepdims=True))
        kw = jnp.exp(kw_log - m_new)
        decay = jnp.exp(gtot + m_old - m_new)
        kk = kh * kw
        c_scr[h] = decay * c_scr[h] + _dot(jnp.transpose(kk).astype(BF16), vh)
        n_scr[h] = decay * n_scr[h] + jnp.sum(kk, axis=0, keepdims=True)
        m_scr[h] = jnp.broadcast_to(m_new, (1, LANES))

        hn = hout * lax.rsqrt(jnp.mean(hout * hout, axis=-1, keepdims=True) + EPS) * ng_ref[:, sl]
        y = jax.nn.sigmoid(o_ref[:, sl].astype(F32)) * hn * _silu(z_ref[:, sl].astype(F32))
        out_ref[:, sl] = y.astype(BF16)


def _mlstm_branch(y_main, gates, bsz, seq, conv_w, conv_b, b_i, b_f, norm_g):
    lc = ML_CHUNK
    nc = seq // lc
    bif = jnp.zeros((1, LANES), F32).at[0, 0:ML_HEADS].set(b_i.astype(F32))
    bif = bif.at[0, ML_HEADS:2 * ML_HEADS].set(b_f.astype(F32))

    def col(cb):
        return pl.BlockSpec((lc, D_BRANCH), lambda b, c: (b * nc + c, cb))

    def const(shape):
        return pl.BlockSpec(shape, lambda b, c: (0,) * len(shape))

    return pl.pallas_call(
        _mlstm_kernel,
        grid=(bsz, nc),
        in_specs=[col(ML_BLK0), col(ML_BLK0 + 1), col(ML_BLK0 + 2), col(ML_BLK0 + 3), col(ML_BLK0 + 4),
                  pl.BlockSpec((lc, LANES), lambda b, c: (b * nc + c, 0)),
                  const((ML_CONV, 2 * D_BRANCH)), const((1, 2 * D_BRANCH)), const((1, LANES)),
                  const((1, D_BRANCH))],
        out_specs=pl.BlockSpec((lc, D_BRANCH), lambda b, c: (b * nc + c, 0)),
        out_shape=jax.ShapeDtypeStruct((bsz * seq, D_BRANCH), BF16),
        scratch_shapes=[pltpu.VMEM((lc + ML_TAIL, 2 * D_BRANCH), BF16),
                        pltpu.VMEM((ML_HEADS, ML_HEAD_DIM, ML_HEAD_DIM), F32),
                        pltpu.VMEM((ML_HEADS, 1, ML_HEAD_DIM), F32),
                        pltpu.VMEM((ML_HEADS, 1, LANES), F32)],
        compiler_params=_params(("parallel", "arbitrary")),
        name="mlstm",
    )(y_main, y_main, y_main, y_main, y_main, gates,
      conv_w.astype(F32), conv_b.reshape(1, 2 * D_BRANCH).astype(F32), bif,
      norm_g.reshape(1, D_BRANCH).astype(F32))


def _rope_tables(seq):
    half = DA_HEAD_DIM // 2
    inv = 1.0 / (ROPE_THETA ** (jnp.arange(0, DA_HEAD_DIM, 2, dtype=F32) / DA_HEAD_DIM))
    ang = jnp.arange(seq, dtype=F32)[:, None] * inv[None, :]
    cos, sin = jnp.cos(ang), jnp.sin(ang)
    cos_t = jnp.tile(jnp.concatenate([cos, cos], axis=1), (1, LANES // DA_HEAD_DIM))
    sin_t = jnp.tile(jnp.concatenate([-sin, sin], axis=1), (1, LANES // DA_HEAD_DIM))
    del half
    return cos_t, sin_t


def _rope(x, cos, sin):
    half = DA_HEAD_DIM // 2
    lane = lax.broadcasted_iota(jnp.int32, (1, LANES), 1)
    first_half = (lane % DA_HEAD_DIM) < half
    partner = jnp.where(first_half, pltpu.roll(x, LANES - half, 1), pltpu.roll(x, half, 1))
    return x * cos + partner * sin


def _diff_attn_kernel(q_ref, k_ref, v_ref, z_ref, cosq_ref, sinq_ref, cosk_ref, sink_ref,
                      lq1_ref, lk1_ref, lq2_ref, lk2_ref, g_ref,
                      out_ref, vt_scr, kr_scr, qt_scr, m_scr, acc_scr, *, lambda_init, tq, cg):
    qi = pl.program_id(2)
    nkb = vt_scr.shape[0]

    @pl.when(qi == 0)
    def _():
        kr_scr[...] = _rope(k_ref[...].astype(F32), cosk_ref[...], sink_ref[...]).astype(BF16)
        ones_row = lax.broadcasted_iota(jnp.int32, (VT_ROWS - DA_V_DIM, tq), 0) == 0
        for jb in range(nkb):
            vt_scr[jb, 0:DA_V_DIM, :] = jnp.transpose(v_ref[jb * tq:(jb + 1) * tq, :].astype(F32)).astype(BF16)
            vt_scr[jb, DA_V_DIM:VT_ROWS, :] = jnp.where(ones_row, 1.0, 0.0).astype(BF16)

    q = _rope(q_ref[...].astype(F32), cosq_ref[...], sinq_ref[...]) * (DA_HEAD_DIM ** -0.5 * LOG2E)
    comp0 = lax.broadcasted_iota(jnp.int32, (1, DA_V_DIM), 1) < DA_HEAD_DIM
    qt_scr[0] = jnp.transpose(jnp.where(comp0, q, 0.0)).astype(BF16)
    qt_scr[1] = jnp.transpose(jnp.where(comp0, 0.0, q)).astype(BF16)

    m_scr[...] = jnp.full_like(m_scr, -jnp.inf)
    acc_scr[...] = jnp.zeros_like(acc_scr)

    def stream(blocks):
        kbs = [kr_scr[pl.ds(pl.multiple_of(j * tq, tq), tq), :] for j, _ in blocks]
        vtbs = [vt_scr[j] for j, _ in blocks]
        chains = [(b, g, c) for b in range(len(blocks)) for g in range(tq // cg) for c in range(2)]

        def nkeys(b, g):
            return (g + 1) * cg if blocks[b][1] else tq

        def scores(b, g, c):
            return _dot(kbs[b][:nkeys(b, g)], qt_scr[c, :, g * cg:(g + 1) * cg])

        def softmax(b, g, c, s):
            cols = slice(g * cg, (g + 1) * cg)
            if blocks[b][1]:
                kpos = lax.broadcasted_iota(jnp.int32, s.shape, 0)
                qpos = lax.broadcasted_iota(jnp.int32, s.shape, 1) + g * cg
                s = jnp.where(kpos <= qpos, s, -jnp.inf)
            m_prev = m_scr[c, :, cols]
            m_new = jnp.maximum(m_prev, jnp.max(s, axis=0, keepdims=True))
            m_scr[c, :, cols] = m_new
            return jnp.exp2(m_prev - m_new), jnp.exp2(s - m_new).astype(BF16)

        def accumulate(b, g, c, alpha, p):
            cols = slice(g * cg, (g + 1) * cg)
            acc_scr[c, :, cols] = alpha * acc_scr[c, :, cols] + _dot(vtbs[b][:, :nkeys(b, g)], p)

        ahead, behind = min(3, len(chains)), 2
        s_q = [scores(*chains[i]) for i in range(ahead)]
        pending = []
        for i, ch in enumerate(chains):
            s_cur = s_q.pop(0)
            if i + ahead < len(chains):
                s_q.append(scores(*chains[i + ahead]))
            alpha, p = softmax(*ch, s_cur)
            pending.append((*ch, alpha, p))
            if len(pending) > behind:
                accumulate(*pending.pop(0))
        for item in pending:
            accumulate(*item)

    def body(j, carry):
        stream([(j, False)])
        return carry

    lax.fori_loop(0, qi, body, 0)
    stream([(qi, True)])

    lam = (jnp.exp(jnp.sum(lq1_ref[...] * lk1_ref[...], axis=-1, keepdims=True))
           - jnp.exp(jnp.sum(lq2_ref[...] * lk2_ref[...], axis=-1, keepdims=True)) + lambda_init)
    o1 = acc_scr[0, 0:DA_V_DIM, :] / acc_scr[0, DA_V_DIM:DA_V_DIM + 1, :]
    o2 = acc_scr[1, 0:DA_V_DIM, :] / acc_scr[1, DA_V_DIM:DA_V_DIM + 1, :]
    o = jnp.transpose(o1 - lam * o2)
    o = o * lax.rsqrt(jnp.mean(o * o, axis=-1, keepdims=True) + EPS) * g_ref[...]
    o = o * (1.0 - lambda_init) * _silu(z_ref[...].astype(F32))
    out_ref[...] = o.astype(BF16)


def _diff_attn_branch(y_main, bsz, seq, cos_t, sin_t, lq1, lk1, lq2, lk2, subln_g, lambda_init, tq, cg):
    ntok = bsz * seq
    tq = min(tq, seq)
    nq = seq // tq
    per_blk = D_BRANCH // DA_V_DIM

    def tile(blk):
        return pl.BlockSpec((tq, DA_V_DIM), lambda b, h, i: (b * nq + i, blk * per_blk + h))

    def full(blk):
        return pl.BlockSpec((seq, DA_V_DIM), lambda b, h, i: (b, blk * per_blk + h))

    def vec(n):
        return pl.BlockSpec((1, n), lambda b, h, i: (0, 0))

    q_tab = pl.BlockSpec((tq, LANES), lambda b, h, i: (i, 0))
    k_tab = pl.BlockSpec((seq, LANES), lambda b, h, i: (0, 0))
    lvec = [a.reshape(1, DA_HEAD_DIM).astype(F32) for a in (lq1, lk1, lq2, lk2)]
    return pl.pallas_call(
        functools.partial(_diff_attn_kernel, lambda_init=lambda_init, tq=tq, cg=min(cg, tq)),
        grid=(bsz, DA_HEADS, nq),
        in_specs=[tile(DA_BLK0), full(DA_BLK0 + 1), full(DA_BLK0 + 2), tile(DA_BLK0 + 3),
                  q_tab, q_tab, k_tab, k_tab,
                  vec(DA_HEAD_DIM), vec(DA_HEAD_DIM), vec(DA_HEAD_DIM), vec(DA_HEAD_DIM), vec(DA_V_DIM)],
        out_specs=pl.BlockSpec((tq, DA_V_DIM), lambda b, h, i: (b * nq + i, h)),
        out_shape=jax.ShapeDtypeStruct((ntok, D_BRANCH), BF16),
        scratch_shapes=[pltpu.VMEM((nq, VT_ROWS, tq), BF16), pltpu.VMEM((seq, DA_V_DIM), BF16),
                        pltpu.VMEM((2, DA_V_DIM, tq), BF16),
                        pltpu.VMEM((2, 1, tq), F32), pltpu.VMEM((2, VT_ROWS, tq), F32)],
        compiler_params=_params(("parallel", "parallel", "arbitrary")),
        name="diff_attn",
    )(y_main, y_main, y_main, y_main, cos_t, sin_t, cos_t, sin_t, *lvec,
      subln_g.reshape(1, DA_V_DIM).astype(F32))


def _mem_attn_kernel(q_ref, z_ref, k_ref, v_ref, out_ref):
    dh = XA_HEAD_DIM
    for h in range(XA_HEADS):
        sl = slice(h * dh, (h + 1) * dh)
        s = _dot_nt(q_ref[:, sl], k_ref[:, sl]) * (dh ** -0.5)
        p = jnp.exp(s - jnp.max(s, axis=-1, keepdims=True))
        p = p / jnp.sum(p, axis=-1, keepdims=True)
        o = _dot(p.astype(BF16), v_ref[:, sl])
        out_ref[:, sl] = (o * _silu(z_ref[:, sl].astype(F32))).astype(BF16)


def _mem_attn_branch(y_main, kv, bsz, seq, mem_len, tq):
    tq = min(tq, seq)
    nq = seq // tq
    return pl.pallas_call(
        _mem_attn_kernel,
        grid=(bsz, nq),
        in_specs=[pl.BlockSpec((tq, D_BRANCH), lambda b, i: (b * nq + i, XA_BLK0)),
                  pl.BlockSpec((tq, D_BRANCH), lambda b, i: (b * nq + i, XA_BLK0 + 1)),
                  pl.BlockSpec((mem_len, D_BRANCH), lambda b, i: (b, 0)),
                  pl.BlockSpec((mem_len, D_BRANCH), lambda b, i: (b, 1))],
        out_specs=pl.BlockSpec((tq, D_BRANCH), lambda b, i: (b * nq + i, 0)),
        out_shape=jax.ShapeDtypeStruct((bsz * seq, D_BRANCH), BF16),
        compiler_params=_params(("parallel", "parallel")),
        name="mem_attn",
    )(y_main, y_main, kv, kv)


def _merge_kernel(*refs):
    y_refs = refs[0:N_BRANCH]
    g_refs = refs[N_BRANCH:2 * N_BRANCH]
    w_refs = refs[2 * N_BRANCH:3 * N_BRANCH]
    out_ref, y0_scr, perm_scr = refs[3 * N_BRANCH:]

    @pl.when(pl.program_id(1) == 0)
    def _():
        rows = y_refs[0].shape[1]
        nslab = D_BRANCH // LANES
        for s in range(S5_T):
            ys = y_refs[0][s].astype(F32)
            for c in range(nslab):
                perm_scr[c, pl.ds(s, rows, stride=S5_T), :] = ys[:, c * LANES:(c + 1) * LANES]
        for c in range(nslab):
            y0_scr[:, c * LANES:(c + 1) * LANES] = perm_scr[c].astype(BF16)

    acc = jax.nn.sigmoid(g_refs[0][...].astype(F32)) * _dot(y0_scr[...], w_refs[0][0])
    for y_ref, g_ref, w_ref in zip(y_refs[1:], g_refs[1:], w_refs[1:]):
        acc = acc + jax.nn.sigmoid(g_ref[...].astype(F32)) * _dot(y_ref[...], w_ref[0])
    out_ref[...] = acc.astype(BF16)


def _merge(branches, y_main, w_branch, layer, tm, tn):
    ntok = y_main.shape[0]
    tm = min(tm, ntok)
    ncol = D_MODEL // tn
    y_specs = ([pl.BlockSpec((S5_T, tm // S5_T, D_BRANCH), lambda i, n: (0, i, 0))]
               + [pl.BlockSpec((tm, D_BRANCH), lambda i, n: (i, 0))] * (N_BRANCH - 1))
    g_specs = [pl.BlockSpec((tm, tn), lambda i, n, b=b: (i, (GATE_COL0 + b * D_MODEL) // tn + n))
               for b in range(N_BRANCH)]
    w_specs = [pl.BlockSpec((pl.Squeezed(), 1, D_BRANCH, tn), lambda i, n, b=b: (layer, b, 0, n))
               for b in range(N_BRANCH)]
    return pl.pallas_call(
        _merge_kernel,
        grid=(ntok // tm, ncol),
        in_specs=y_specs + g_specs + w_specs,
        out_specs=pl.BlockSpec((tm, tn), lambda i, n: (i, n)),
        out_shape=jax.ShapeDtypeStruct((ntok, D_MODEL), BF16),
        scratch_shapes=[pltpu.VMEM((tm, D_BRANCH), BF16), pltpu.VMEM((D_BRANCH // LANES, tm, LANES), F32)],
        compiler_params=_params(("parallel", "arbitrary")),
        name="merge",
    )(*branches, *([y_main] * N_BRANCH), *([w_branch] * N_BRANCH))


def _out_proj_kernel(m_ref, w_ref, g_ref, x_ref, out_ref):
    o = _dot(m_ref[...], w_ref[...])
    o = o * lax.rsqrt(jnp.mean(o * o, axis=-1, keepdims=True) + EPS) * g_ref[...]
    out_ref[...] = x_ref[...] + o


def _out_proj(merged, w_out, layer, g_post, x, tm):
    ntok = x.shape[0]
    tm = min(tm, ntok)
    return pl.pallas_call(
        _out_proj_kernel,
        grid=(ntok // tm,),
        in_specs=[pl.BlockSpec((tm, D_MODEL), lambda i: (i, 0)),
                  pl.BlockSpec((pl.Squeezed(), D_MODEL, D_MODEL), lambda i: (layer, 0, 0)),
                  pl.BlockSpec((1, D_MODEL), lambda i: (0, 0)),
                  pl.BlockSpec((tm, D_MODEL), lambda i: (i, 0))],
        out_specs=pl.BlockSpec((tm, D_MODEL), lambda i: (i, 0)),
        out_shape=jax.ShapeDtypeStruct((ntok, D_MODEL), F32),
        compiler_params=_params(("parallel",)),
        name="out_proj",
    )(merged, w_out, g_post.reshape(1, D_MODEL).astype(F32), x)


def kernel(x, mem, g_pre, w_in, s5_lam_re, s5_lam_im, s5_log_dt, s5_b_re, s5_b_im, s5_c_re, s5_c_im, s5_d, s5_w_glu, s5_b_glu, ml_conv_w, ml_conv_b, ml_b_i, ml_b_f, ml_norm_g, da_lq1, da_lk1, da_lq2, da_lk2, da_subln_g, g_mem, xa_w_kv, w_branch, w_out, g_post):
    bsz, seq, _ = x.shape
    mem_len = mem.shape[1]
    depth = w_in.shape[0]
    ntok = bsz * seq
    cos_t, sin_t = _rope_tables(seq)
    xf = x.reshape(ntok, D_MODEL).astype(F32)
    memf = mem.reshape(bsz * mem_len, D_MODEL).astype(F32)
    gate0 = 7 * D_BRANCH
    expanders = _s5_expanders()
    s5_ops = jax.vmap(_s5_operators)(s5_lam_re, s5_lam_im, s5_log_dt, s5_b_re, s5_b_im, s5_c_re, s5_c_im)
    w_branch_b = w_branch.astype(BF16)
    w_out_b = w_out.astype(BF16)
    w_kv_b = xa_w_kv.astype(BF16)
    w_glu_b = s5_w_glu.astype(BF16)
    gate_lane = jnp.arange(LANES) < 2 * ML_HEADS
    for l in range(depth):
        lambda_init = 0.8 - 0.6 * math.exp(-0.3 * l)
        w_l = w_in[l]
        w_main = jnp.concatenate([w_l[:, :gate0], w_l[:, gate0 + 2 * ML_HEADS:]], axis=1).astype(BF16)
        w_gate = jnp.where(gate_lane[None, :], w_l[:, gate0:gate0 + LANES], 0.0).astype(BF16)
        y_main, s5_in, gates = _in_proj(xf, g_pre[l], w_main, w_gate, tm=1024, tn=1024)
        kv = _norm_proj(memf, g_mem[l], w_kv_b, l, tm=1024, tn=1024)

        y_s5 = _s5_branch(s5_in, bsz, seq, s5_ops, l, expanders, s5_d[l].astype(F32), w_glu_b, s5_b_glu[l],
                          rt=512)
        y_ml = _mlstm_branch(y_main, gates, bsz, seq, ml_conv_w[l], ml_conv_b[l], ml_b_i[l], ml_b_f[l],
                             ml_norm_g[l])
        y_da = _diff_attn_branch(y_main, bsz, seq, cos_t, sin_t, da_lq1[l], da_lk1[l], da_lq2[l], da_lk2[l],
                                 da_subln_g[l], lambda_init, tq=1024, cg=256)
        y_xa = _mem_attn_branch(y_main, kv, bsz, seq, mem_len, tq=512)

        merged = _merge((y_s5, y_ml, y_da, y_xa), y_main, w_branch_b, l, tm=1024, tn=512)
        xf = _out_proj(merged, w_out_b, l, g_post[l], xf, tm=512)
    return xf.reshape(bsz, seq, D_MODEL).astype(x.dtype)
```

```python
import functools
import math

import jax
import jax.numpy as jnp
from jax import lax
from jax.experimental import pallas as pl
from jax.experimental.pallas import tpu as pltpu

F32 = jnp.float32
BF16 = jnp.bfloat16

D_MODEL = 2048
D_BRANCH = 1024
N_BRANCH = 4
S5_GROUP = 16
S5_GROUPS = D_BRANCH // S5_GROUP
S5_STATE = 64
ML_HEADS = 4
ML_HEAD_DIM = D_BRANCH // ML_HEADS
ML_CHUNK = 128
ML_CONV = 4
ML_TAIL = 16
DA_HEADS = 8
DA_HEAD_DIM = 64
DA_V_DIM = 2 * DA_HEAD_DIM
XA_HEADS = 4
XA_HEAD_DIM = D_BRANCH // XA_HEADS
ROPE_THETA = 10000.0
EPS = 1e-6
LOG2E = 1.4426950408889634

LANES = 128
VT_ROWS = DA_V_DIM + 16
S5_T = 16
S5_LANE_GROUPS = LANES // S5_GROUP
S5_NBLK = D_BRANCH // LANES
S5_SBLK = S5_LANE_GROUPS * S5_STATE
S5_SLAB = S5_SBLK // LANES
N_S5IN = 2 * D_BRANCH
N_MAIN = 11 * D_BRANCH + N_BRANCH * D_MODEL
ML_BLK0, DA_BLK0, XA_BLK0 = 0, 5, 9
GATE_COL0 = 11 * D_BRANCH
MIB = 1024 * 1024
VMEM_LIMIT = 48 * MIB


def _params(sem, vmem=VMEM_LIMIT):
    return pltpu.CompilerParams(dimension_semantics=sem, vmem_limit_bytes=vmem)


def _silu(x):
    return x * jax.nn.sigmoid(x)


def _dot(a, b):
    return jnp.dot(a, b, preferred_element_type=F32)


def _dot_nt(a, b):
    return lax.dot_general(a, b, (((1,), (1,)), ((), ())), preferred_element_type=F32)


def _rms_norm_bf16(x, g):
    ms = jnp.mean(x * x, axis=-1, keepdims=True)
    return (x * lax.rsqrt(ms + EPS) * g).astype(BF16)


def _norm_proj_kernel(x_ref, g_ref, w_ref, y_ref, h_scr):
    @pl.when(pl.program_id(1) == 0)
    def _():
        h_scr[...] = _rms_norm_bf16(x_ref[...], g_ref[...])

    y_ref[...] = _dot(h_scr[...], w_ref[...]).astype(y_ref.dtype)


def _norm_proj(x, g, w, layer, tm, tn):
    m, k = x.shape
    n = w.shape[2]
    tm = min(tm, m)
    return pl.pallas_call(
        _norm_proj_kernel,
        grid=(m // tm, n // tn),
        in_specs=[pl.BlockSpec((tm, k), lambda i, j: (i, 0)),
                  pl.BlockSpec((1, k), lambda i, j: (0, 0)),
                  pl.BlockSpec((pl.Squeezed(), k, tn), lambda i, j: (layer, 0, j))],
        out_specs=pl.BlockSpec((tm, tn), lambda i, j: (i, j)),
        out_shape=jax.ShapeDtypeStruct((m, n), BF16),
        scratch_shapes=[pltpu.VMEM((tm, k), BF16)],
        compiler_params=_params(("parallel", "arbitrary")),
        name="norm_proj",
    )(x, g.reshape(1, k), w)


def _pack_w_in_kernel(a_ref, b_ref, o_ref, *, first_shifted, shift):
    j = pl.program_id(2)

    @pl.when(j < first_shifted)
    def _():
        o_ref[...] = a_ref[...].astype(BF16)

    @pl.when(j >= first_shifted)
    def _():
        cat = jnp.concatenate([a_ref[...], b_ref[...]], axis=1)
        width = cat.shape[1]
        o_ref[...] = pltpu.roll(cat, width - shift, 1)[:, :o_ref.shape[1]].astype(BF16)


def _pack_w_in(w_in, gate0, ngate, tr, tn):
    depth, k, n_src = w_in.shape
    n_out = n_src - ngate
    per = tn // LANES
    return pl.pallas_call(
        functools.partial(_pack_w_in_kernel, first_shifted=gate0 // tn, shift=ngate),
        grid=(depth, k // tr, n_out // tn),
        in_specs=[pl.BlockSpec((pl.Squeezed(), tr, tn), lambda l, i, j: (l, i, j)),
                  pl.BlockSpec((pl.Squeezed(), tr, LANES), lambda l, i, j: (l, i, (j + 1) * per))],
        out_specs=pl.BlockSpec((pl.Squeezed(), tr, tn), lambda l, i, j: (l, i, j)),
        out_shape=jax.ShapeDtypeStruct((depth, k, n_out), BF16),
        compiler_params=_params(("parallel", "parallel", "parallel")),
        name="pack_w_in",
    )(w_in, w_in)


def _in_proj_kernel(x_ref, g_ref, w_ref, wa_ref, y_ref, s5_ref, a_ref, h_scr, perm_scr):
    j = pl.program_id(1)
    tm, tn = y_ref.shape
    n_s5 = N_S5IN // tn

    @pl.when(j == 0)
    def _():
        h = _rms_norm_bf16(x_ref[...], g_ref[...])
        h_scr[...] = h
        a_ref[...] = _dot(h, wa_ref[...])

    @pl.when(j < n_s5)
    def _():
        acc = _dot(h_scr[...], w_ref[...])
        for c in range(tn // LANES):
            perm_scr[c] = acc[:, c * LANES:(c + 1) * LANES]
        for s in range(S5_T):
            for c in range(tn // LANES):
                rows = perm_scr[c, pl.ds(s, tm // S5_T, stride=S5_T), :]
                s5_ref[s, :, c * LANES:(c + 1) * LANES] = rows.astype(BF16)

    @pl.when(j >= n_s5)
    def _():
        y_ref[...] = _dot(h_scr[...], w_ref[...]).astype(BF16)


def _in_proj(x, g, w, layer, w_aux, tm, tn):
    m, k = x.shape
    tm = min(tm, m)
    n_s5 = N_S5IN // tn
    return pl.pallas_call(
        _in_proj_kernel,
        grid=(m // tm, (N_S5IN + N_MAIN) // tn),
        in_specs=[pl.BlockSpec((tm, k), lambda i, j: (i, 0)),
                  pl.BlockSpec((1, k), lambda i, j: (0, 0)),
                  pl.BlockSpec((pl.Squeezed(), k, tn), lambda i, j: (layer, 0, j)),
                  pl.BlockSpec((k, LANES), lambda i, j: (0, 0))],
        out_specs=[pl.BlockSpec((tm, tn), lambda i, j: (i, jnp.maximum(j - n_s5, 0))),
                   pl.BlockSpec((S5_T, tm // S5_T, tn), lambda i, j: (0, i, jnp.minimum(j, n_s5 - 1))),
                   pl.BlockSpec((tm, LANES), lambda i, j: (i, 0))],
        out_shape=[jax.ShapeDtypeStruct((m, N_MAIN), BF16),
                   jax.ShapeDtypeStruct((S5_T, m // S5_T, N_S5IN), BF16),
                   jax.ShapeDtypeStruct((m, LANES), F32)],
        scratch_shapes=[pltpu.VMEM((tm, k), BF16), pltpu.VMEM((tn // LANES, tm, LANES), F32)],
        compiler_params=_params(("parallel", "arbitrary"), 56 * MIB),
        name="in_proj",
    )(x, g.reshape(1, k), w, w_aux)


def _s5_operators(lam_re, lam_im, log_dt, b_re, b_im, c_re, c_im):
    hi = lax.Precision.HIGHEST
    G, P, C, T, J, LG = S5_GROUPS, S5_STATE, S5_GROUP, S5_T, S5_NBLK, S5_LANE_GROUPS
    lam = lax.complex(jnp.minimum(lam_re.astype(F32), -1e-4), lam_im.astype(F32))
    dt = jnp.exp(log_dt.astype(F32))[:, None]
    z = lam * dt
    lam_bar = jnp.exp(z)
    b_bar = ((lam_bar - 1.0) / lam)[..., None] * lax.complex(b_re.astype(F32), b_im.astype(F32))
    c = lax.complex(c_re.astype(F32), c_im.astype(F32))
    steps = jnp.arange(T + 1, dtype=F32)
    pw = jnp.exp(z[None] * steps[:, None, None].astype(jnp.complex64))

    kern = jnp.real(jnp.einsum('gop,kgp,gpi->kgoi', c, pw[:T], b_bar, precision=hi))
    kc = kern.reshape(T, J, LG, C, C).transpose(1, 2, 4, 0, 3).reshape(J, LANES, T * C)

    bs = pw[T - 1 - jnp.arange(T)][..., None] * b_bar[None]
    bs = jnp.stack([jnp.real(bs), jnp.imag(bs)], axis=0)
    bs = bs.reshape(2, T, J, LG, P, C).transpose(2, 1, 3, 5, 0, 4)
    b2 = bs.reshape(J, T * LANES, 2 * P).astype(BF16)

    cs = c[None] * pw[1:T + 1][:, :, None, :]
    cs = jnp.stack([jnp.real(cs), -jnp.imag(cs)], axis=0)
    cs = cs.reshape(2, T, J, LG, C, P).transpose(2, 0, 3, 5, 1, 4)
    c2 = cs.reshape(J, 2 * S5_SBLK, T * C).astype(BF16)

    lam_t = pw[T].reshape(1, G * P)
    return kc, b2, c2, jnp.real(lam_t), jnp.imag(lam_t)


def _s5_expanders():
    r = jnp.arange(S5_T * S5_GROUP)[:, None]
    c = jnp.arange(S5_T * LANES)[None, :]
    e_to = ((r // S5_GROUP == c // LANES) & (r % S5_GROUP == c % S5_GROUP)).astype(BF16)
    r = jnp.arange(2 * S5_STATE)[:, None]
    c = jnp.arange(2 * S5_SBLK)[None, :]
    e_rp = ((r // S5_STATE == c // S5_SBLK) & (r % S5_STATE == c % S5_STATE)).astype(BF16)
    return e_to, e_rp


def _expand_block_diag(dst_scr, compact, exp_ref, row_group, col_group):
    rows = compact.shape[0]
    cols = exp_ref.shape[1]
    slab = 2 * LANES
    rg = (lax.broadcasted_iota(jnp.int32, (rows, slab), 0) // row_group) % S5_LANE_GROUPS
    cl = lax.broadcasted_iota(jnp.int32, (rows, slab), 1)
    for c0 in range(0, cols, slab):
        cg = ((cl + c0) // col_group) % S5_LANE_GROUPS
        blk = _dot(compact, exp_ref[:, c0:c0 + slab])
        dst_scr[:, c0:c0 + slab] = jnp.where(rg == cg, blk, 0.0).astype(BF16)


def _s5_local_kernel(u_ref, b2_ref, erp_ref, s_ref, min_scr):
    @pl.when(pl.program_id(1) == 0)
    def _():
        _expand_block_diag(min_scr, b2_ref[0], erp_ref, S5_GROUP, S5_STATE)

    ucat = jnp.concatenate([u_ref[s] for s in range(S5_T)], axis=1)
    s = _dot(ucat, min_scr[...])
    for c in range(2 * S5_SLAB):
        s_ref[c] = s[:, c * LANES:(c + 1) * LANES]


def _s5_scan_kernel(sr_ref, si_ref, ar_ref, ai_ref, xr_ref, xi_ref, *, bsz, nchunk):
    ar = [jnp.broadcast_to(ar_ref[:, c * LANES:(c + 1) * LANES], (bsz, LANES)) for c in range(S5_SLAB)]
    ai = [jnp.broadcast_to(ai_ref[:, c * LANES:(c + 1) * LANES], (bsz, LANES)) for c in range(S5_SLAB)]

    def step(k, carry):
        rows = pl.ds(k, bsz, stride=nchunk)
        new = []
        for c in range(S5_SLAB):
            xr, xi = carry[c]
            xr_ref[c, rows, :] = xr
            xi_ref[c, rows, :] = xi
            nr = ar[c] * xr - ai[c] * xi + sr_ref[c, rows, :]
            ni = ar[c] * xi + ai[c] * xr + si_ref[c, rows, :]
            new.append((nr, ni))
        return tuple(new)

    zero = jnp.zeros((bsz, LANES), F32)
    lax.fori_loop(0, nchunk, step, tuple((zero, zero) for _ in range(S5_SLAB)), unroll=4)


def _s5_out_kernel(u_ref, kc_ref, c2_ref, eto_ref, xr_ref, xi_ref, d_ref, out_ref, mi_scr, mo_scr, a2_scr):
    @pl.when(pl.program_id(1) == 0)
    def _():
        kc = kc_ref[0]
        lane = lax.broadcasted_iota(jnp.int32, kc.shape, 1)
        for s in range(S5_T):
            shifted = kc if s == 0 else jnp.where(lane >= s * S5_GROUP, pltpu.roll(kc, s * S5_GROUP, 1), 0.0)
            a2_scr[s * LANES:(s + 1) * LANES, :] = shifted.astype(BF16)
        _expand_block_diag(mi_scr, a2_scr[...], eto_ref, S5_GROUP, S5_GROUP)
        _expand_block_diag(mo_scr, c2_ref[0], eto_ref, S5_STATE, S5_GROUP)

    us = [u_ref[s] for s in range(S5_T)]
    ucat = jnp.concatenate(us, axis=1)
    xcat = jnp.concatenate([xr_ref[c] for c in range(S5_SLAB)] + [xi_ref[c] for c in range(S5_SLAB)],
                           axis=1).astype(BF16)
    d = d_ref[...]
    tile = 2 * LANES
    for c0 in range(0, S5_T * LANES, tile):
        y = _dot(ucat[:, :c0 + tile], mi_scr[:c0 + tile, c0:c0 + tile]) + _dot(xcat, mo_scr[:, c0:c0 + tile])
        for t in range(c0 // LANES, (c0 + tile) // LANES):
            yt = y[:, t * LANES - c0:(t + 1) * LANES - c0] + d * us[t].astype(F32)
            out_ref[t] = jax.nn.gelu(yt).astype(BF16)


def _s5_glu_kernel(g_ref, z_ref, w_ref, b_ref, out_ref):
    g = g_ref[...]
    a = _dot(g, w_ref[...]) + b_ref[...]
    out_ref[...] = (g.astype(F32) * jax.nn.sigmoid(a) * _silu(z_ref[...].astype(F32))).astype(BF16)


def _s5_branch(s5_in, bsz, seq, ops, layer, expanders, d_skip, w_glu, b_glu, rt):
    kc, b2, c2, lam_r, lam_i = ops
    e_to, e_rp = expanders
    nrow = bsz * seq // S5_T
    nchunk = seq // S5_T
    rt = min(rt, nrow)
    u_spec = pl.BlockSpec((S5_T, rt, LANES), lambda j, r: (0, r, j))

    def whole(a):
        return pl.BlockSpec(a.shape, lambda j, r: (0,) * a.ndim)

    def per_blk(a):
        return pl.BlockSpec((pl.Squeezed(), 1) + a.shape[2:], lambda j, r: (layer, j, 0, 0))

    s_loc = pl.pallas_call(
        _s5_local_kernel,
        grid=(S5_NBLK, nrow // rt),
        in_specs=[u_spec, per_blk(b2), whole(e_rp)],
        out_specs=pl.BlockSpec((2 * S5_SLAB, rt, LANES), lambda j, r: (j, r, 0)),
        out_shape=jax.ShapeDtypeStruct((S5_NBLK * 2 * S5_SLAB, nrow, LANES), F32),
        scratch_shapes=[pltpu.VMEM((S5_T * LANES, 2 * S5_SBLK), BF16)],
        compiler_params=_params(("parallel", "arbitrary")),
        name="s5_local",
    )(s5_in, b2, e_rp)

    xr, xi = pl.pallas_call(
        functools.partial(_s5_scan_kernel, bsz=bsz, nchunk=nchunk),
        grid=(S5_NBLK,),
        in_specs=[pl.BlockSpec((S5_SLAB, nrow, LANES), lambda j: (2 * j, 0, 0)),
                  pl.BlockSpec((S5_SLAB, nrow, LANES), lambda j: (2 * j + 1, 0, 0)),
                  pl.BlockSpec((pl.Squeezed(), 1, S5_SBLK), lambda j: (layer, 0, j)),
                  pl.BlockSpec((pl.Squeezed(), 1, S5_SBLK), lambda j: (layer, 0, j))],
        out_specs=[pl.BlockSpec((S5_SLAB, nrow, LANES), lambda j: (j, 0, 0))] * 2,
        out_shape=[jax.ShapeDtypeStruct((S5_NBLK * S5_SLAB, nrow, LANES), F32)] * 2,
        compiler_params=_params(("parallel",)),
        name="s5_scan",
    )(s_loc, s_loc, lam_r, lam_i)

    gel = pl.pallas_call(
        _s5_out_kernel,
        grid=(S5_NBLK, nrow // rt),
        in_specs=[u_spec, per_blk(kc), per_blk(c2), whole(e_to),
                  pl.BlockSpec((S5_SLAB, rt, LANES), lambda j, r: (j, r, 0)),
                  pl.BlockSpec((S5_SLAB, rt, LANES), lambda j, r: (j, r, 0)),
                  pl.BlockSpec((1, LANES), lambda j, r: (0, j))],
        out_specs=pl.BlockSpec((S5_T, rt, LANES), lambda j, r: (0, r, j)),
        out_shape=jax.ShapeDtypeStruct((S5_T, nrow, D_BRANCH), BF16),
        scratch_shapes=[pltpu.VMEM((S5_T * LANES, S5_T * LANES), BF16),
                        pltpu.VMEM((2 * S5_SBLK, S5_T * LANES), BF16),
                        pltpu.VMEM((S5_T * LANES, S5_T * S5_GROUP), BF16)],
        compiler_params=_params(("parallel", "arbitrary")),
        name="s5_out",
    )(s5_in, kc, c2, e_to, xr, xi, d_skip.reshape(1, D_BRANCH))

    row_blk = pl.BlockSpec((pl.Squeezed(), rt, D_BRANCH), lambda t, r: (t, r, 0))
    return pl.pallas_call(
        _s5_glu_kernel,
        grid=(S5_T, nrow // rt),
        in_specs=[row_blk,
                  pl.BlockSpec((pl.Squeezed(), rt, D_BRANCH), lambda t, r: (t, r, 1)),
                  pl.BlockSpec((pl.Squeezed(), D_BRANCH, D_BRANCH), lambda t, r: (layer, 0, 0)),
                  pl.BlockSpec((1, D_BRANCH), lambda t, r: (0, 0))],
        out_specs=row_blk,
        out_shape=jax.ShapeDtypeStruct((S5_T, nrow, D_BRANCH), BF16),
        compiler_params=_params(("parallel", "parallel")),
        name="s5_glu",
    )(gel, s5_in, w_glu, b_glu.reshape(1, D_BRANCH).astype(F32))


def _split3(x):
    hi = x.astype(BF16)
    r1 = x - hi.astype(F32)
    mid = r1.astype(BF16)
    lo = (r1 - mid.astype(F32)).astype(BF16)
    return hi, mid, lo


def _mlstm_kernel(q_ref, k_ref, v_ref, o_ref, z_ref, if_ref, cw_ref, cb_ref, bif_ref, ng_ref, out_ref,
                  ext_scr, c_scr, n_scr, m_scr):
    lc, dh, nh = ML_CHUNK, ML_HEAD_DIM, ML_HEADS
    tail = ML_TAIL
    cidx = pl.program_id(1)

    @pl.when(cidx == 0)
    def _():
        ext_scr[0:tail, :] = jnp.zeros((tail, 2 * D_BRANCH), BF16)
        c_scr[...] = jnp.zeros_like(c_scr)
        n_scr[...] = jnp.zeros_like(n_scr)
        m_scr[...] = jnp.zeros_like(m_scr)

    @pl.when(cidx > 0)
    def _():
        ext_scr[0:tail, :] = ext_scr[lc:lc + tail, :]

    ext_scr[tail:tail + lc, 0:D_BRANCH] = q_ref[...]
    ext_scr[tail:tail + lc, D_BRANCH:2 * D_BRANCH] = k_ref[...]
    ext = ext_scr[...]
    srow = lax.broadcasted_iota(jnp.int32, (lc, tail + lc), 0)
    scol = lax.broadcasted_iota(jnp.int32, (lc, tail + lc), 1)
    conv = cb_ref[...] + cw_ref[ML_CONV - 1:ML_CONV, :] * ext[tail:tail + lc].astype(F32)
    for d in range(1, ML_CONV):
        shift = jnp.where(scol == srow + (tail - d), 1.0, 0.0).astype(BF16)
        conv = conv + cw_ref[ML_CONV - 1 - d:ML_CONV - d, :] * _dot(shift, ext)
    qk = _silu(conv)

    gts = if_ref[...] + bif_ref[...]
    lf = jnp.minimum(gts, 0.0) - jnp.log(1.0 + jnp.exp(-jnp.abs(gts)))
    gts_t = jnp.transpose(gts)[0:16, :]
    lf_t = jnp.transpose(lf)[0:16, :]
    row = lax.broadcasted_iota(jnp.int32, (lc, lc), 0)
    col = lax.broadcasted_iota(jnp.int32, (lc, lc), 1)
    causal = col <= row
    tril = jnp.where(causal, 1.0, 0.0).astype(BF16)
    triu = jnp.where(row <= col, 1.0, 0.0).astype(BF16)
    acum_c = sum(_dot(tril, p) for p in _split3(lf))
    acum_r = sum(_dot(p, triu) for p in _split3(lf_t))

    for h in range(nh):
        sl = slice(h * dh, (h + 1) * dh)
        qh = qk[:, h * dh:(h + 1) * dh]
        kh = qk[:, D_BRANCH + h * dh:D_BRANCH + (h + 1) * dh] * (dh ** -0.5)
        vh = v_ref[:, sl]
        qb = qh.astype(BF16)
        a_c = acum_c[:, nh + h:nh + h + 1]
        a_r = acum_r[nh + h:nh + h + 1, :]
        i_c = gts[:, h:h + 1]
        i_r = gts_t[h:h + 1, :]
        gtot = a_c[lc - 1:lc, :]
        m_old = m_scr[h][:, 0:1]
        dmat = jnp.where(causal, a_c - a_r + i_r, -jnp.inf)
        inter = a_c + m_old
        m_row = jnp.maximum(jnp.max(dmat, axis=-1, keepdims=True), inter)
        s = _dot_nt(qb, kh.astype(BF16)) * jnp.exp(dmat - m_row)
        sc = jnp.exp(inter - m_row)
        num = _dot(s.astype(BF16), vh) + sc * _dot(qb, c_scr[h].astype(BF16))
        den = jnp.sum(s, axis=-1, keepdims=True) + sc * jnp.sum(qh * n_scr[h], axis=-1, keepdims=True)
        hout = num / jnp.maximum(jnp.abs(den), jnp.exp(-m_row))

        kw_log = gtot - a_c + i_c
        m_new = jnp.maximum(gtot + m_old, jnp.max(kw_log, axis=0, keepdims=True))
        kw = jnp.exp(kw_log - m_new)
        decay = jnp.exp(gtot + m_old - m_new)
        kk = kh * kw
        c_scr[h] = decay * c_scr[h] + _dot(jnp.transpose(kk).astype(BF16), vh)
        n_scr[h] = decay * n_scr[h] + jnp.sum(kk, axis=0, keepdims=True)
        m_scr[h] = jnp.broadcast_to(m_new, (1, LANES))

        hn = hout * lax.rsqrt(jnp.mean(hout * hout, axis=-1, keepdims=True) + EPS) * ng_ref[:, sl]
        y = jax.nn.sigmoid(o_ref[:, sl].astype(F32)) * hn * _silu(z_ref[:, sl].astype(F32))
        out_ref[:, sl] = y.astype(BF16)


def _mlstm_branch(y_main, gates, bsz, seq, conv_w, conv_b, b_i, b_f, norm_g):
    lc = ML_CHUNK
    nc = seq // lc
    bif = jnp.zeros((1, LANES), F32).at[0, 0:ML_HEADS].set(b_i.astype(F32))
    bif = bif.at[0, ML_HEADS:2 * ML_HEADS].set(b_f.astype(F32))

    def col(cb):
        return pl.BlockSpec((lc, D_BRANCH), lambda b, c: (b * nc + c, cb))

    def const(shape):
        return pl.BlockSpec(shape, lambda b, c: (0,) * len(shape))

    return pl.pallas_call(
        _mlstm_kernel,
        grid=(bsz, nc),
        in_specs=[col(ML_BLK0), col(ML_BLK0 + 1), col(ML_BLK0 + 2), col(ML_BLK0 + 3), col(ML_BLK0 + 4),
                  pl.BlockSpec((lc, LANES), lambda b, c: (b * nc + c, 0)),
                  const((ML_CONV, 2 * D_BRANCH)), const((1, 2 * D_BRANCH)), const((1, LANES)),
                  const((1, D_BRANCH))],
        out_specs=pl.BlockSpec((lc, D_BRANCH), lambda b, c: (b * nc + c, 0)),
        out_shape=jax.ShapeDtypeStruct((bsz * seq, D_BRANCH), BF16),
        scratch_shapes=[pltpu.VMEM((lc + ML_TAIL, 2 * D_BRANCH), BF16),
                        pltpu.VMEM((ML_HEADS, ML_HEAD_DIM, ML_HEAD_DIM), F32),
                        pltpu.VMEM((ML_HEADS, 1, ML_HEAD_DIM), F32),
                        pltpu.VMEM((ML_HEADS, 1, LANES), F32)],
        compiler_params=_params(("parallel", "arbitrary")),
        name="mlstm",
    )(y_main, y_main, y_main, y_main, y_main, gates,
      conv_w.astype(F32), conv_b.reshape(1, 2 * D_BRANCH).astype(F32), bif,
      norm_g.reshape(1, D_BRANCH).astype(F32))


def _rope_tables(seq):
    half = DA_HEAD_DIM // 2
    inv = 1.0 / (ROPE_THETA ** (jnp.arange(0, DA_HEAD_DIM, 2, dtype=F32) / DA_HEAD_DIM))
    ang = jnp.arange(seq, dtype=F32)[:, None] * inv[None, :]
    cos, sin = jnp.cos(ang), jnp.sin(ang)
    cos_t = jnp.tile(jnp.concatenate([cos, cos], axis=1), (1, LANES // DA_HEAD_DIM))
    sin_t = jnp.tile(jnp.concatenate([-sin, sin], axis=1), (1, LANES // DA_HEAD_DIM))
    del half
    return cos_t, sin_t


def _rope(x, cos, sin):
    half = DA_HEAD_DIM // 2
    lane = lax.broadcasted_iota(jnp.int32, (1, LANES), 1)
    first_half = (lane % DA_HEAD_DIM) < half
    partner = jnp.where(first_half, pltpu.roll(x, LANES - half, 1), pltpu.roll(x, half, 1))
    return x * cos + partner * sin


def _diff_attn_kernel(q_ref, k_ref, v_ref, z_ref, cosq_ref, sinq_ref, cosk_ref, sink_ref,
                      lq1_ref, lk1_ref, lq2_ref, lk2_ref, g_ref,
                      out_ref, vt_scr, kr_scr, qt_scr, m_scr, acc_scr, *, lambda_init, tq, cg):
    qi = pl.program_id(2)
    nkb = vt_scr.shape[0]

    @pl.when(qi == 0)
    def _():
        kr_scr[...] = _rope(k_ref[...].astype(F32), cosk_ref[...], sink_ref[...]).astype(BF16)
        ones_row = lax.broadcasted_iota(jnp.int32, (VT_ROWS - DA_V_DIM, tq), 0) == 0
        for jb in range(nkb):
            vt_scr[jb, 0:DA_V_DIM, :] = jnp.transpose(v_ref[jb * tq:(jb + 1) * tq, :].astype(F32)).astype(BF16)
            vt_scr[jb, DA_V_DIM:VT_ROWS, :] = jnp.where(ones_row, 1.0, 0.0).astype(BF16)

    q = _rope(q_ref[...].astype(F32), cosq_ref[...], sinq_ref[...]) * (DA_HEAD_DIM ** -0.5 * LOG2E)
    comp0 = lax.broadcasted_iota(jnp.int32, (1, DA_V_DIM), 1) < DA_HEAD_DIM
    qt_scr[0] = jnp.transpose(jnp.where(comp0, q, 0.0)).astype(BF16)
    qt_scr[1] = jnp.transpose(jnp.where(comp0, 0.0, q)).astype(BF16)

    m_scr[...] = jnp.full_like(m_scr, -jnp.inf)
    acc_scr[...] = jnp.zeros_like(acc_scr)

    def stream(blocks):
        kbs = [kr_scr[pl.ds(pl.multiple_of(j * tq, tq), tq), :] for j, _ in blocks]
        vtbs = [vt_scr[j] for j, _ in blocks]
        chains = [(b, g, c) for b in range(len(blocks)) for g in range(tq // cg) for c in range(2)]

        def nkeys(b, g):
            return (g + 1) * cg if blocks[b][1] else tq

        def scores(b, g, c):
            return _dot(kbs[b][:nkeys(b, g)], qt_scr[c, :, g * cg:(g + 1) * cg])

        def softmax(b, g, c, s):
            cols = slice(g * cg, (g + 1) * cg)
            if blocks[b][1]:
                kpos = lax.broadcasted_iota(jnp.int32, s.shape, 0)
                qpos = lax.broadcasted_iota(jnp.int32, s.shape, 1) + g * cg
                s = jnp.where(kpos <= qpos, s, -jnp.inf)
            m_prev = m_scr[c, :, cols]
            m_new = jnp.maximum(m_prev, jnp.max(s, axis=0, keepdims=True))
            m_scr[c, :, cols] = m_new
            return jnp.exp2(m_prev - m_new), jnp.exp2(s - m_new).astype(BF16)

        def accumulate(b, g, c, alpha, p):
            cols = slice(g * cg, (g + 1) * cg)
            acc_scr[c, :, cols] = alpha * acc_scr[c, :, cols] + _dot(vtbs[b][:, :nkeys(b, g)], p)

        ahead, behind = min(3, len(chains)), 2
        s_q = [scores(*chains[i]) for i in range(ahead)]
        pending = []
        for i, ch in enumerate(chains):
            s_cur = s_q.pop(0)
            if i + ahead < len(chains):
                s_q.append(scores(*chains[i + ahead]))
            alpha, p = softmax(*ch, s_cur)
            pending.append((*ch, alpha, p))
            if len(pending) > behind:
                accumulate(*pending.pop(0))
        for item in pending:
            accumulate(*item)

    def body(j, carry):
        stream([(j, False)])
        return carry

    lax.fori_loop(0, qi, body, 0)
    stream([(qi, True)])

    lam = (jnp.exp(jnp.sum(lq1_ref[...] * lk1_ref[...], axis=-1, keepdims=True))
           - jnp.exp(jnp.sum(lq2_ref[...] * lk2_ref[...], axis=-1, keepdims=True)) + lambda_init)
    o1 = acc_scr[0, 0:DA_V_DIM, :] / acc_scr[0, DA_V_DIM:DA_V_DIM + 1, :]
    o2 = acc_scr[1, 0:DA_V_DIM, :] / acc_scr[1, DA_V_DIM:DA_V_DIM + 1, :]
    o = jnp.transpose(o1 - lam * o2)
    o = o * lax.rsqrt(jnp.mean(o * o, axis=-1, keepdims=True) + EPS) * g_ref[...]
    o = o * (1.0 - lambda_init) * _silu(z_ref[...].astype(F32))
    out_ref[...] = o.astype(BF16)


def _diff_attn_branch(y_main, bsz, seq, cos_t, sin_t, lq1, lk1, lq2, lk2, subln_g, lambda_init, tq, cg):
    ntok = bsz * seq
    tq = min(tq, seq)
    nq = seq // tq
    per_blk = D_BRANCH // DA_V_DIM

    def tile(blk):
        return pl.BlockSpec((tq, DA_V_DIM), lambda b, h, i: (b * nq + i, blk * per_blk + h))

    def full(blk):
        return pl.BlockSpec((seq, DA_V_DIM), lambda b, h, i: (b, blk * per_blk + h))

    def vec(n):
        return pl.BlockSpec((1, n), lambda b, h, i: (0, 0))

    q_tab = pl.BlockSpec((tq, LANES), lambda b, h, i: (i, 0))
    k_tab = pl.BlockSpec((seq, LANES), lambda b, h, i: (0, 0))
    lvec = [a.reshape(1, DA_HEAD_DIM).astype(F32) for a in (lq1, lk1, lq2, lk2)]
    return pl.pallas_call(
        functools.partial(_diff_attn_kernel, lambda_init=lambda_init, tq=tq, cg=min(cg, tq)),
        grid=(bsz, DA_HEADS, nq),
        in_specs=[tile(DA_BLK0), full(DA_BLK0 + 1), full(DA_BLK0 + 2), tile(DA_BLK0 + 3),
                  q_tab, q_tab, k_tab, k_tab,
                  vec(DA_HEAD_DIM), vec(DA_HEAD_DIM), vec(DA_HEAD_DIM), vec(DA_HEAD_DIM), vec(DA_V_DIM)],
        out_specs=pl.BlockSpec((tq, DA_V_DIM), lambda b, h, i: (b * nq + i, h)),
        out_shape=jax.ShapeDtypeStruct((ntok, D_BRANCH), BF16),
        scratch_shapes=[pltpu.VMEM((nq, VT_ROWS, tq), BF16), pltpu.VMEM((seq, DA_V_DIM), BF16),
                        pltpu.VMEM((2, DA_V_DIM, tq), BF16),
                        pltpu.VMEM((2, 1, tq), F32), pltpu.VMEM((2, VT_ROWS, tq), F32)],
        compiler_params=_params(("parallel", "parallel", "arbitrary")),
        name="diff_attn",
    )(y_main, y_main, y_main, y_main, cos_t, sin_t, cos_t, sin_t, *lvec,
      subln_g.reshape(1, DA_V_DIM).astype(F32))


def _mem_attn_kernel(q_ref, z_ref, k_ref, v_ref, out_ref):
    dh = XA_HEAD_DIM
    for h in range(XA_HEADS):
        sl = slice(h * dh, (h + 1) * dh)
        s = _dot_nt(q_ref[:, sl], k_ref[:, sl]) * (dh ** -0.5)
        p = jnp.exp(s - jnp.max(s, axis=-1, keepdims=True))
        p = p / jnp.sum(p, axis=-1, keepdims=True)
        o = _dot(p.astype(BF16), v_ref[:, sl])
        out_ref[:, sl] = (o * _silu(z_ref[:, sl].astype(F32))).astype(BF16)


def _mem_attn_branch(y_main, kv, bsz, seq, mem_len, tq):
    tq = min(tq, seq)
    nq = seq // tq
    return pl.pallas_call(
        _mem_attn_kernel,
        grid=(bsz, nq),
        in_specs=[pl.BlockSpec((tq, D_BRANCH), lambda b, i: (b * nq + i, XA_BLK0)),
                  pl.BlockSpec((tq, D_BRANCH), lambda b, i: (b * nq + i, XA_BLK0 + 1)),
                  pl.BlockSpec((mem_len, D_BRANCH), lambda b, i: (b, 0)),
                  pl.BlockSpec((mem_len, D_BRANCH), lambda b, i: (b, 1))],
        out_specs=pl.BlockSpec((tq, D_BRANCH), lambda b, i: (b * nq + i, 0)),
        out_shape=jax.ShapeDtypeStruct((bsz * seq, D_BRANCH), BF16),
        compiler_params=_params(("parallel", "parallel")),
        name="mem_attn",
    )(y_main, y_main, kv, kv)


def _merge_kernel(*refs):
    y_refs = refs[0:N_BRANCH]
    g_refs = refs[N_BRANCH:2 * N_BRANCH]
    w_refs = refs[2 * N_BRANCH:3 * N_BRANCH]
    out_ref, y0_scr, perm_scr = refs[3 * N_BRANCH:]

    @pl.when(pl.program_id(1) == 0)
    def _():
        rows = y_refs[0].shape[1]
        nslab = D_BRANCH // LANES
        for s in range(S5_T):
            ys = y_refs[0][s].astype(F32)
            for c in range(nslab):
                perm_scr[c, pl.ds(s, rows, stride=S5_T), :] = ys[:, c * LANES:(c + 1) * LANES]
        for c in range(nslab):
            y0_scr[:, c * LANES:(c + 1) * LANES] = perm_scr[c].astype(BF16)

    acc = jax.nn.sigmoid(g_refs[0][...].astype(F32)) * _dot(y0_scr[...], w_refs[0][0])
    for y_ref, g_ref, w_ref in zip(y_refs[1:], g_refs[1:], w_refs[1:]):
        acc = acc + jax.nn.sigmoid(g_ref[...].astype(F32)) * _dot(y_ref[...], w_ref[0])
    out_ref[...] = acc.astype(BF16)


def _merge(branches, y_main, w_branch, layer, tm, tn):
    ntok = y_main.shape[0]
    tm = min(tm, ntok)
    ncol = D_MODEL // tn
    y_specs = ([pl.BlockSpec((S5_T, tm // S5_T, D_BRANCH), lambda i, n: (0, i, 0))]
               + [pl.BlockSpec((tm, D_BRANCH), lambda i, n: (i, 0))] * (N_BRANCH - 1))
    g_specs = [pl.BlockSpec((tm, tn), lambda i, n, b=b: (i, (GATE_COL0 + b * D_MODEL) // tn + n))
               for b in range(N_BRANCH)]
    w_specs = [pl.BlockSpec((pl.Squeezed(), 1, D_BRANCH, tn), lambda i, n, b=b: (layer, b, 0, n))
               for b in range(N_BRANCH)]
    return pl.pallas_call(
        _merge_kernel,
        grid=(ntok // tm, ncol),
        in_specs=y_specs + g_specs + w_specs,
        out_specs=pl.BlockSpec((tm, tn), lambda i, n: (i, n)),
        out_shape=jax.ShapeDtypeStruct((ntok, D_MODEL), BF16),
        scratch_shapes=[pltpu.VMEM((tm, D_BRANCH), BF16), pltpu.VMEM((D_BRANCH // LANES, tm, LANES), F32)],
        compiler_params=_params(("parallel", "arbitrary")),
        name="merge",
    )(*branches, *([y_main] * N_BRANCH), *([w_branch] * N_BRANCH))


def _out_proj_kernel(m_ref, w_ref, g_ref, x_ref, out_ref):
    o = _dot(m_ref[...], w_ref[...])
    o = o * lax.rsqrt(jnp.mean(o * o, axis=-1, keepdims=True) + EPS) * g_ref[...]
    out_ref[...] = x_ref[...] + o


def _out_proj(merged, w_out, layer, g_post, x, tm):
    ntok = x.shape[0]
    tm = min(tm, ntok)
    return pl.pallas_call(
        _out_proj_kernel,
        grid=(ntok // tm,),
        in_specs=[pl.BlockSpec((tm, D_MODEL), lambda i: (i, 0)),
                  pl.BlockSpec((pl.Squeezed(), D_MODEL, D_MODEL), lambda i: (layer, 0, 0)),
                  pl.BlockSpec((1, D_MODEL), lambda i: (0, 0)),
                  pl.BlockSpec((tm, D_MODEL), lambda i: (i, 0))],
        out_specs=pl.BlockSpec((tm, D_MODEL), lambda i: (i, 0)),
        out_shape=jax.ShapeDtypeStruct((ntok, D_MODEL), F32),
        compiler_params=_params(("parallel",)),
        name="out_proj",
    )(merged, w_out, g_post.reshape(1, D_MODEL).astype(F32), x)


def kernel(x, mem, g_pre, w_in, s5_lam_re, s5_lam_im, s5_log_dt, s5_b_re, s5_b_im, s5_c_re, s5_c_im, s5_d, s5_w_glu, s5_b_glu, ml_conv_w, ml_conv_b, ml_b_i, ml_b_f, ml_norm_g, da_lq1, da_lk1, da_lq2, da_lk2, da_subln_g, g_mem, xa_w_kv, w_branch, w_out, g_post):
    bsz, seq, _ = x.shape
    mem_len = mem.shape[1]
    depth = w_in.shape[0]
    ntok = bsz * seq
    cos_t, sin_t = _rope_tables(seq)
    xf = x.reshape(ntok, D_MODEL).astype(F32)
    memf = mem.reshape(bsz * mem_len, D_MODEL).astype(F32)
    gate0 = 7 * D_BRANCH
    expanders = _s5_expanders()
    s5_ops = jax.vmap(_s5_operators)(s5_lam_re, s5_lam_im, s5_log_dt, s5_b_re, s5_b_im, s5_c_re, s5_c_im)
    w_branch_b = w_branch.astype(BF16)
    w_out_b = w_out.astype(BF16)
    w_kv_b = xa_w_kv.astype(BF16)
    w_glu_b = s5_w_glu.astype(BF16)
    gate_lane = jnp.arange(LANES) < 2 * ML_HEADS
    w_main = _pack_w_in(w_in, gate0, 2 * ML_HEADS, tr=512, tn=1024)
    for l in range(depth):
        lambda_init = 0.8 - 0.6 * math.exp(-0.3 * l)
        w_gate = jnp.where(gate_lane[None, :], w_in[l, :, gate0:gate0 + LANES], 0.0).astype(BF16)
        y_main, s5_in, gates = _in_proj(xf, g_pre[l], w_main, l, w_gate, tm=1024, tn=1024)
        kv = _norm_proj(memf, g_mem[l], w_kv_b, l, tm=1024, tn=1024)

        y_s5 = _s5_branch(s5_in, bsz, seq, s5_ops, l, expanders, s5_d[l].astype(F32), w_glu_b, s5_b_glu[l],
                          rt=512)
        y_ml = _mlstm_branch(y_main, gates, bsz, seq, ml_conv_w[l], ml_conv_b[l], ml_b_i[l], ml_b_f[l],
                             ml_norm_g[l])
        y_da = _diff_attn_branch(y_main, bsz, seq, cos_t, sin_t, da_lq1[l], da_lk1[l], da_lq2[l], da_lk2[l],
                                 da_subln_g[l], lambda_init, tq=1024, cg=256)
        y_xa = _mem_attn_branch(y_main, kv, bsz, seq, mem_len, tq=512)

        merged = _merge((y_s5, y_ml, y_da, y_xa), y_main, w_branch_b, l, tm=1024, tn=512)
        xf = _out_proj(merged, w_out_b, l, g_post[l], xf, tm=512)
    return xf.reshape(bsz, seq, D_MODEL).astype(x.dtype)
```

```python
import functools
import math

import jax
import jax.numpy as jnp
from jax import lax
from jax.experimental import pallas as pl
from jax.experimental.pallas import tpu as pltpu

F32 = jnp.float32
BF16 = jnp.bfloat16

D_MODEL = 2048
D_BRANCH = 1024
N_BRANCH = 4
S5_GROUP = 16
S5_GROUPS = D_BRANCH // S5_GROUP
S5_STATE = 64
ML_HEADS = 4
ML_HEAD_DIM = D_BRANCH // ML_HEADS
ML_CHUNK = 128
ML_CONV = 4
ML_TAIL = 16
DA_HEADS = 8
DA_HEAD_DIM = 64
DA_V_DIM = 2 * DA_HEAD_DIM
XA_HEADS = 4
XA_HEAD_DIM = D_BRANCH // XA_HEADS
ROPE_THETA = 10000.0
EPS = 1e-6
LOG2E = 1.4426950408889634

LANES = 128
VT_ROWS = DA_V_DIM + 16
S5_T = 16
S5_LANE_GROUPS = LANES // S5_GROUP
S5_NBLK = D_BRANCH // LANES
S5_SBLK = S5_LANE_GROUPS * S5_STATE
S5_SLAB = S5_SBLK // LANES
N_S5IN = 2 * D_BRANCH
N_MAIN = 11 * D_BRANCH + N_BRANCH * D_MODEL
ML_BLK0, DA_BLK0, XA_BLK0 = 0, 5, 9
GATE_COL0 = 11 * D_BRANCH
MIB = 1024 * 1024
VMEM_LIMIT = 48 * MIB


def _params(sem, vmem=VMEM_LIMIT):
    return pltpu.CompilerParams(dimension_semantics=sem, vmem_limit_bytes=vmem)


def _silu(x):
    return x * jax.nn.sigmoid(x)


def _dot(a, b):
    return jnp.dot(a, b, preferred_element_type=F32)


def _dot_nt(a, b):
    return lax.dot_general(a, b, (((1,), (1,)), ((), ())), preferred_element_type=F32)


def _rms_norm_bf16(x, g):
    ms = jnp.mean(x * x, axis=-1, keepdims=True)
    return (x * lax.rsqrt(ms + EPS) * g).astype(BF16)


def _norm_proj_kernel(x_ref, g_ref, w_ref, y_ref, h_scr):
    @pl.when(pl.program_id(1) == 0)
    def _():
        h_scr[...] = _rms_norm_bf16(x_ref[...], g_ref[...])

    y_ref[...] = _dot(h_scr[...], w_ref[...]).astype(y_ref.dtype)


def _norm_proj(x, g, w, layer, tm, tn):
    m, k = x.shape
    n = w.shape[2]
    tm = min(tm, m)
    return pl.pallas_call(
        _norm_proj_kernel,
        grid=(m // tm, n // tn),
        in_specs=[pl.BlockSpec((tm, k), lambda i, j: (i, 0)),
                  pl.BlockSpec((1, k), lambda i, j: (0, 0)),
                  pl.BlockSpec((pl.Squeezed(), k, tn), lambda i, j: (layer, 0, j))],
        out_specs=pl.BlockSpec((tm, tn), lambda i, j: (i, j)),
        out_shape=jax.ShapeDtypeStruct((m, n), BF16),
        scratch_shapes=[pltpu.VMEM((tm, k), BF16)],
        compiler_params=_params(("parallel", "arbitrary")),
        name="norm_proj",
    )(x, g.reshape(1, k), w)


def _in_proj_kernel(x_ref, g_ref, w_ref, wa_ref, y_ref, s5_ref, a_ref, h_scr, perm_scr):
    j = pl.program_id(1)
    tm, tn = y_ref.shape
    n_s5 = N_S5IN // tn

    @pl.when(j == 0)
    def _():
        h = _rms_norm_bf16(x_ref[...], g_ref[...])
        h_scr[...] = h
        a_ref[...] = _dot(h, wa_ref[...])

    @pl.when(j < n_s5)
    def _():
        acc = _dot(h_scr[...], w_ref[...])
        for c in range(tn // LANES):
            perm_scr[c] = acc[:, c * LANES:(c + 1) * LANES]
        for s in range(S5_T):
            for c in range(tn // LANES):
                rows = perm_scr[c, pl.ds(s, tm // S5_T, stride=S5_T), :]
                s5_ref[s, :, c * LANES:(c + 1) * LANES] = rows.astype(BF16)

    @pl.when(j >= n_s5)
    def _():
        y_ref[...] = _dot(h_scr[...], w_ref[...]).astype(BF16)


def _in_proj(x, g, w, w_aux, tm, tn):
    m, k = x.shape
    tm = min(tm, m)
    n_s5 = N_S5IN // tn
    return pl.pallas_call(
        _in_proj_kernel,
        grid=(m // tm, (N_S5IN + N_MAIN) // tn),
        in_specs=[pl.BlockSpec((tm, k), lambda i, j: (i, 0)),
                  pl.BlockSpec((1, k), lambda i, j: (0, 0)),
                  pl.BlockSpec((k, tn), lambda i, j: (0, j)),
                  pl.BlockSpec((k, LANES), lambda i, j: (0, 0))],
        out_specs=[pl.BlockSpec((tm, tn), lambda i, j: (i, jnp.maximum(j - n_s5, 0))),
                   pl.BlockSpec((S5_T, tm // S5_T, tn), lambda i, j: (0, i, jnp.minimum(j, n_s5 - 1))),
                   pl.BlockSpec((tm, LANES), lambda i, j: (i, 0))],
        out_shape=[jax.ShapeDtypeStruct((m, N_MAIN), BF16),
                   jax.ShapeDtypeStruct((S5_T, m // S5_T, N_S5IN), BF16),
                   jax.ShapeDtypeStruct((m, LANES), F32)],
        scratch_shapes=[pltpu.VMEM((tm, k), BF16), pltpu.VMEM((tn // LANES, tm, LANES), F32)],
        compiler_params=_params(("parallel", "arbitrary"), 56 * MIB),
        name="in_proj",
    )(x, g.reshape(1, k), w, w_aux)


def _s5_operators(lam_re, lam_im, log_dt, b_re, b_im, c_re, c_im):
    hi = lax.Precision.HIGHEST
    G, P, C, T, J, LG = S5_GROUPS, S5_STATE, S5_GROUP, S5_T, S5_NBLK, S5_LANE_GROUPS
    lam = lax.complex(jnp.minimum(lam_re.astype(F32), -1e-4), lam_im.astype(F32))
    dt = jnp.exp(log_dt.astype(F32))[:, None]
    z = lam * dt
    lam_bar = jnp.exp(z)
    b_bar = ((lam_bar - 1.0) / lam)[..., None] * lax.complex(b_re.astype(F32), b_im.astype(F32))
    c = lax.complex(c_re.astype(F32), c_im.astype(F32))
    steps = jnp.arange(T + 1, dtype=F32)
    pw = jnp.exp(z[None] * steps[:, None, None].astype(jnp.complex64))

    kern = jnp.real(jnp.einsum('gop,kgp,gpi->kgoi', c, pw[:T], b_bar, precision=hi))
    kc = kern.reshape(T, J, LG, C, C).transpose(1, 2, 4, 0, 3).reshape(J, LANES, T * C)

    bs = pw[T - 1 - jnp.arange(T)][..., None] * b_bar[None]
    bs = jnp.stack([jnp.real(bs), jnp.imag(bs)], axis=0)
    bs = bs.reshape(2, T, J, LG, P, C).transpose(2, 1, 3, 5, 0, 4)
    b2 = bs.reshape(J, T * LANES, 2 * P).astype(BF16)

    cs = c[None] * pw[1:T + 1][:, :, None, :]
    cs = jnp.stack([jnp.real(cs), -jnp.imag(cs)], axis=0)
    cs = cs.reshape(2, T, J, LG, C, P).transpose(2, 0, 3, 5, 1, 4)
    c2 = cs.reshape(J, 2 * S5_SBLK, T * C).astype(BF16)

    lam_t = pw[T].reshape(1, G * P)
    return kc, b2, c2, jnp.real(lam_t), jnp.imag(lam_t)


def _s5_expanders():
    r = jnp.arange(S5_T * S5_GROUP)[:, None]
    c = jnp.arange(S5_T * LANES)[None, :]
    e_to = ((r // S5_GROUP == c // LANES) & (r % S5_GROUP == c % S5_GROUP)).astype(BF16)
    r = jnp.arange(2 * S5_STATE)[:, None]
    c = jnp.arange(2 * S5_SBLK)[None, :]
    e_rp = ((r // S5_STATE == c // S5_SBLK) & (r % S5_STATE == c % S5_STATE)).astype(BF16)
    return e_to, e_rp


def _expand_block_diag(dst_scr, compact, exp_ref, row_group, col_group):
    rows = compact.shape[0]
    cols = exp_ref.shape[1]
    slab = 2 * LANES
    rg = (lax.broadcasted_iota(jnp.int32, (rows, slab), 0) // row_group) % S5_LANE_GROUPS
    cl = lax.broadcasted_iota(jnp.int32, (rows, slab), 1)
    for c0 in range(0, cols, slab):
        cg = ((cl + c0) // col_group) % S5_LANE_GROUPS
        blk = _dot(compact, exp_ref[:, c0:c0 + slab])
        dst_scr[:, c0:c0 + slab] = jnp.where(rg == cg, blk, 0.0).astype(BF16)


def _s5_local_kernel(u_ref, b2_ref, erp_ref, s_ref, min_scr):
    @pl.when(pl.program_id(1) == 0)
    def _():
        _expand_block_diag(min_scr, b2_ref[0], erp_ref, S5_GROUP, S5_STATE)

    ucat = jnp.concatenate([u_ref[s] for s in range(S5_T)], axis=1)
    s = _dot(ucat, min_scr[...])
    for c in range(2 * S5_SLAB):
        s_ref[c] = s[:, c * LANES:(c + 1) * LANES]


def _s5_scan_kernel(sr_ref, si_ref, ar_ref, ai_ref, xr_ref, xi_ref, *, bsz, nchunk):
    ar = [jnp.broadcast_to(ar_ref[:, c * LANES:(c + 1) * LANES], (bsz, LANES)) for c in range(S5_SLAB)]
    ai = [jnp.broadcast_to(ai_ref[:, c * LANES:(c + 1) * LANES], (bsz, LANES)) for c in range(S5_SLAB)]

    def step(k, carry):
        rows = pl.ds(k, bsz, stride=nchunk)
        new = []
        for c in range(S5_SLAB):
            xr, xi = carry[c]
            xr_ref[c, rows, :] = xr
            xi_ref[c, rows, :] = xi
            nr = ar[c] * xr - ai[c] * xi + sr_ref[c, rows, :]
            ni = ar[c] * xi + ai[c] * xr + si_ref[c, rows, :]
            new.append((nr, ni))
        return tuple(new)

    zero = jnp.zeros((bsz, LANES), F32)
    lax.fori_loop(0, nchunk, step, tuple((zero, zero) for _ in range(S5_SLAB)), unroll=4)


def _s5_out_kernel(u_ref, kc_ref, c2_ref, eto_ref, xr_ref, xi_ref, d_ref, out_ref, mi_scr, mo_scr, a2_scr):
    @pl.when(pl.program_id(1) == 0)
    def _():
        kc = kc_ref[0]
        lane = lax.broadcasted_iota(jnp.int32, kc.shape, 1)
        for s in range(S5_T):
            shifted = kc if s == 0 else jnp.where(lane >= s * S5_GROUP, pltpu.roll(kc, s * S5_GROUP, 1), 0.0)
            a2_scr[s * LANES:(s + 1) * LANES, :] = shifted.astype(BF16)
        _expand_block_diag(mi_scr, a2_scr[...], eto_ref, S5_GROUP, S5_GROUP)
        _expand_block_diag(mo_scr, c2_ref[0], eto_ref, S5_STATE, S5_GROUP)

    us = [u_ref[s] for s in range(S5_T)]
    ucat = jnp.concatenate(us, axis=1)
    xcat = jnp.concatenate([xr_ref[c] for c in range(S5_SLAB)] + [xi_ref[c] for c in range(S5_SLAB)],
                           axis=1).astype(BF16)
    d = d_ref[...]
    tile = 2 * LANES
    for c0 in range(0, S5_T * LANES, tile):
        y = _dot(ucat[:, :c0 + tile], mi_scr[:c0 + tile, c0:c0 + tile]) + _dot(xcat, mo_scr[:, c0:c0 + tile])
        for t in range(c0 // LANES, (c0 + tile) // LANES):
            yt = y[:, t * LANES - c0:(t + 1) * LANES - c0] + d * us[t].astype(F32)
            out_ref[t] = jax.nn.gelu(yt).astype(BF16)


def _s5_glu_kernel(g_ref, z_ref, w_ref, b_ref, out_ref):
    g = g_ref[...]
    a = _dot(g, w_ref[...]) + b_ref[...]
    out_ref[...] = (g.astype(F32) * jax.nn.sigmoid(a) * _silu(z_ref[...].astype(F32))).astype(BF16)


def _s5_branch(s5_in, bsz, seq, ops, layer, expanders, d_skip, w_glu, b_glu, rt):
    kc, b2, c2, lam_r, lam_i = ops
    e_to, e_rp = expanders
    nrow = bsz * seq // S5_T
    nchunk = seq // S5_T
    rt = min(rt, nrow)
    u_spec = pl.BlockSpec((S5_T, rt, LANES), lambda j, r: (0, r, j))

    def whole(a):
        return pl.BlockSpec(a.shape, lambda j, r: (0,) * a.ndim)

    def per_blk(a):
        return pl.BlockSpec((pl.Squeezed(), 1) + a.shape[2:], lambda j, r: (layer, j, 0, 0))

    s_loc = pl.pallas_call(
        _s5_local_kernel,
        grid=(S5_NBLK, nrow // rt),
        in_specs=[u_spec, per_blk(b2), whole(e_rp)],
        out_specs=pl.BlockSpec((2 * S5_SLAB, rt, LANES), lambda j, r: (j, r, 0)),
        out_shape=jax.ShapeDtypeStruct((S5_NBLK * 2 * S5_SLAB, nrow, LANES), F32),
        scratch_shapes=[pltpu.VMEM((S5_T * LANES, 2 * S5_SBLK), BF16)],
        compiler_params=_params(("parallel", "arbitrary")),
        name="s5_local",
    )(s5_in, b2, e_rp)

    xr, xi = pl.pallas_call(
        functools.partial(_s5_scan_kernel, bsz=bsz, nchunk=nchunk),
        grid=(S5_NBLK,),
        in_specs=[pl.BlockSpec((S5_SLAB, nrow, LANES), lambda j: (2 * j, 0, 0)),
                  pl.BlockSpec((S5_SLAB, nrow, LANES), lambda j: (2 * j + 1, 0, 0)),
                  pl.BlockSpec((pl.Squeezed(), 1, S5_SBLK), lambda j: (layer, 0, j)),
                  pl.BlockSpec((pl.Squeezed(), 1, S5_SBLK), lambda j: (layer, 0, j))],
        out_specs=[pl.BlockSpec((S5_SLAB, nrow, LANES), lambda j: (j, 0, 0))] * 2,
        out_shape=[jax.ShapeDtypeStruct((S5_NBLK * S5_SLAB, nrow, LANES), F32)] * 2,
        compiler_params=_params(("parallel",)),
        name="s5_scan",
    )(s_loc, s_loc, lam_r, lam_i)

    gel = pl.pallas_call(
        _s5_out_kernel,
        grid=(S5_NBLK, nrow // rt),
        in_specs=[u_spec, per_blk(kc), per_blk(c2), whole(e_to),
                  pl.BlockSpec((S5_SLAB, rt, LANES), lambda j, r: (j, r, 0)),
                  pl.BlockSpec((S5_SLAB, rt, LANES), lambda j, r: (j, r, 0)),
                  pl.BlockSpec((1, LANES), lambda j, r: (0, j))],
        out_specs=pl.BlockSpec((S5_T, rt, LANES), lambda j, r: (0, r, j)),
        out_shape=jax.ShapeDtypeStruct((S5_T, nrow, D_BRANCH), BF16),
        scratch_shapes=[pltpu.VMEM((S5_T * LANES, S5_T * LANES), BF16),
                        pltpu.VMEM((2 * S5_SBLK, S5_T * LANES), BF16),
                        pltpu.VMEM((S5_T * LANES, S5_T * S5_GROUP), BF16)],
        compiler_params=_params(("parallel", "arbitrary")),
        name="s5_out",
    )(s5_in, kc, c2, e_to, xr, xi, d_skip.reshape(1, D_BRANCH))

    row_blk = pl.BlockSpec((pl.Squeezed(), rt, D_BRANCH), lambda t, r: (t, r, 0))
    return pl.pallas_call(
        _s5_glu_kernel,
        grid=(S5_T, nrow // rt),
        in_specs=[row_blk,
                  pl.BlockSpec((pl.Squeezed(), rt, D_BRANCH), lambda t, r: (t, r, 1)),
                  pl.BlockSpec((pl.Squeezed(), D_BRANCH, D_BRANCH), lambda t, r: (layer, 0, 0)),
                  pl.BlockSpec((1, D_BRANCH), lambda t, r: (0, 0))],
        out_specs=row_blk,
        out_shape=jax.ShapeDtypeStruct((S5_T, nrow, D_BRANCH), BF16),
        compiler_params=_params(("parallel", "parallel")),
        name="s5_glu",
    )(gel, s5_in, w_glu, b_glu.reshape(1, D_BRANCH).astype(F32))


def _split3(x):
    hi = x.astype(BF16)
    r1 = x - hi.astype(F32)
    mid = r1.astype(BF16)
    lo = (r1 - mid.astype(F32)).astype(BF16)
    return hi, mid, lo


def _mlstm_kernel(q_ref, k_ref, v_ref, o_ref, z_ref, if_ref, cw_ref, cb_ref, bif_ref, ng_ref, out_ref,
                  ext_scr, c_scr, n_scr, m_scr):
    lc, dh, nh = ML_CHUNK, ML_HEAD_DIM, ML_HEADS
    tail = ML_TAIL
    cidx = pl.program_id(1)

    @pl.when(cidx == 0)
    def _():
        ext_scr[0:tail, :] = jnp.zeros((tail, 2 * D_BRANCH), BF16)
        c_scr[...] = jnp.zeros_like(c_scr)
        n_scr[...] = jnp.zeros_like(n_scr)
        m_scr[...] = jnp.zeros_like(m_scr)

    @pl.when(cidx > 0)
    def _():
        ext_scr[0:tail, :] = ext_scr[lc:lc + tail, :]

    ext_scr[tail:tail + lc, 0:D_BRANCH] = q_ref[...]
    ext_scr[tail:tail + lc, D_BRANCH:2 * D_BRANCH] = k_ref[...]
    ext = ext_scr[...]
    srow = lax.broadcasted_iota(jnp.int32, (lc, tail + lc), 0)
    scol = lax.broadcasted_iota(jnp.int32, (lc, tail + lc), 1)
    conv = cb_ref[...] + cw_ref[ML_CONV - 1:ML_CONV, :] * ext[tail:tail + lc].astype(F32)
    for d in range(1, ML_CONV):
        shift = jnp.where(scol == srow + (tail - d), 1.0, 0.0).astype(BF16)
        conv = conv + cw_ref[ML_CONV - 1 - d:ML_CONV - d, :] * _dot(shift, ext)
    qk = _silu(conv)

    gts = if_ref[...] + bif_ref[...]
    lf = jnp.minimum(gts, 0.0) - jnp.log(1.0 + jnp.exp(-jnp.abs(gts)))
    gts_t = jnp.transpose(gts)[0:16, :]
    lf_t = jnp.transpose(lf)[0:16, :]
    row = lax.broadcasted_iota(jnp.int32, (lc, lc), 0)
    col = lax.broadcasted_iota(jnp.int32, (lc, lc), 1)
    causal = col <= row
    tril = jnp.where(causal, 1.0, 0.0).astype(BF16)
    triu = jnp.where(row <= col, 1.0, 0.0).astype(BF16)
    acum_c = sum(_dot(tril, p) for p in _split3(lf))
    acum_r = sum(_dot(p, triu) for p in _split3(lf_t))

    for h in range(nh):
        sl = slice(h * dh, (h + 1) * dh)
        qh = qk[:, h * dh:(h + 1) * dh]
        kh = qk[:, D_BRANCH + h * dh:D_BRANCH + (h + 1) * dh] * (dh ** -0.5)
        vh = v_ref[:, sl]
        qb = qh.astype(BF16)
        a_c = acum_c[:, nh + h:nh + h + 1]
        a_r = acum_r[nh + h:nh + h + 1, :]
        i_c = gts[:, h:h + 1]
        i_r = gts_t[h:h + 1, :]
        gtot = a_c[lc - 1:lc, :]
        m_old = m_scr[h][:, 0:1]
        dmat = jnp.where(causal, a_c - a_r + i_r, -jnp.inf)
        inter = a_c + m_old
        m_row = jnp.maximum(jnp.max(dmat, axis=-1, keepdims=True), inter)
        s = _dot_nt(qb, kh.astype(BF16)) * jnp.exp(dmat - m_row)
        sc = jnp.exp(inter - m_row)
        num = _dot(s.astype(BF16), vh) + sc * _dot(qb, c_scr[h].astype(BF16))
        den = jnp.sum(s, axis=-1, keepdims=True) + sc * jnp.sum(qh * n_scr[h], axis=-1, keepdims=True)
        hout = num / jnp.maximum(jnp.abs(den), jnp.exp(-m_row))

        kw_log = gtot - a_c + i_c
        m_new = jnp.maximum(gtot + m_old, jnp.max(kw_log, axis=0, keepdims=True))
        kw = jnp.exp(kw_log - m_new)
        decay = jnp.exp(gtot + m_old - m_new)
        kk = kh * kw
        c_scr[h] = decay * c_scr[h] + _dot(jnp.transpose(kk).astype(BF16), vh)
        n_scr[h] = decay * n_scr[h] + jnp.sum(kk, axis=0, keepdims=True)
        m_scr[h] = jnp.broadcast_to(m_new, (1, LANES))

        hn = hout * lax.rsqrt(jnp.mean(hout * hout, axis=-1, keepdims=True) + EPS) * ng_ref[:, sl]
        y = jax.nn.sigmoid(o_ref[:, sl].astype(F32)) * hn * _silu(z_ref[:, sl].astype(F32))
        out_ref[:, sl] = y.astype(BF16)


def _mlstm_branch(y_main, gates, bsz, seq, conv_w, conv_b, b_i, b_f, norm_g):
    lc = ML_CHUNK
    nc = seq // lc
    bif = jnp.zeros((1, LANES), F32).at[0, 0:ML_HEADS].set(b_i.astype(F32))
    bif = bif.at[0, ML_HEADS:2 * ML_HEADS].set(b_f.astype(F32))

    def col(cb):
        return pl.BlockSpec((lc, D_BRANCH), lambda b, c: (b * nc + c, cb))

    def const(shape):
        return pl.BlockSpec(shape, lambda b, c: (0,) * len(shape))

    return pl.pallas_call(
        _mlstm_kernel,
        grid=(bsz, nc),
        in_specs=[col(ML_BLK0), col(ML_BLK0 + 1), col(ML_BLK0 + 2), col(ML_BLK0 + 3), col(ML_BLK0 + 4),
                  pl.BlockSpec((lc, LANES), lambda b, c: (b * nc + c, 0)),
                  const((ML_CONV, 2 * D_BRANCH)), const((1, 2 * D_BRANCH)), const((1, LANES)),
                  const((1, D_BRANCH))],
        out_specs=pl.BlockSpec((lc, D_BRANCH), lambda b, c: (b * nc + c, 0)),
        out_shape=jax.ShapeDtypeStruct((bsz * seq, D_BRANCH), BF16),
        scratch_shapes=[pltpu.VMEM((lc + ML_TAIL, 2 * D_BRANCH), BF16),
                        pltpu.VMEM((ML_HEADS, ML_HEAD_DIM, ML_HEAD_DIM), F32),
                        pltpu.VMEM((ML_HEADS, 1, ML_HEAD_DIM), F32),
                        pltpu.VMEM((ML_HEADS, 1, LANES), F32)],
        compiler_params=_params(("parallel", "arbitrary")),
        name="mlstm",
    )(y_main, y_main, y_main, y_main, y_main, gates,
      conv_w.astype(F32), conv_b.reshape(1, 2 * D_BRANCH).astype(F32), bif,
      norm_g.reshape(1, D_BRANCH).astype(F32))


def _rope_tables(seq):
    half = DA_HEAD_DIM // 2
    inv = 1.0 / (ROPE_THETA ** (jnp.arange(0, DA_HEAD_DIM, 2, dtype=F32) / DA_HEAD_DIM))
    ang = jnp.arange(seq, dtype=F32)[:, None] * inv[None, :]
    cos, sin = jnp.cos(ang), jnp.sin(ang)
    cos_t = jnp.tile(jnp.concatenate([cos, cos], axis=1), (1, LANES // DA_HEAD_DIM))
    sin_t = jnp.tile(jnp.concatenate([-sin, sin], axis=1), (1, LANES // DA_HEAD_DIM))
    del half
    return cos_t, sin_t


def _rope(x, cos, sin):
    half = DA_HEAD_DIM // 2
    lane = lax.broadcasted_iota(jnp.int32, (1, LANES), 1)
    first_half = (lane % DA_HEAD_DIM) < half
    partner = jnp.where(first_half, pltpu.roll(x, LANES - half, 1), pltpu.roll(x, half, 1))
    return x * cos + partner * sin


def _diff_attn_kernel(q_ref, k_ref, v_ref, z_ref, cosq_ref, sinq_ref, cosk_ref, sink_ref,
                      lq1_ref, lk1_ref, lq2_ref, lk2_ref, g_ref,
                      out_ref, vt_scr, kr_scr, qt_scr, m_scr, acc_scr, *, lambda_init, tq, cg):
    qi = pl.program_id(2)
    nkb = vt_scr.shape[0]

    @pl.when(qi == 0)
    def _():
        kr_scr[...] = _rope(k_ref[...].astype(F32), cosk_ref[...], sink_ref[...]).astype(BF16)
        ones_row = lax.broadcasted_iota(jnp.int32, (VT_ROWS - DA_V_DIM, tq), 0) == 0
        for jb in range(nkb):
            vt_scr[jb, 0:DA_V_DIM, :] = jnp.transpose(v_ref[jb * tq:(jb + 1) * tq, :].astype(F32)).astype(BF16)
            vt_scr[jb, DA_V_DIM:VT_ROWS, :] = jnp.where(ones_row, 1.0, 0.0).astype(BF16)

    q = _rope(q_ref[...].astype(F32), cosq_ref[...], sinq_ref[...]) * (DA_HEAD_DIM ** -0.5 * LOG2E)
    comp0 = lax.broadcasted_iota(jnp.int32, (1, DA_V_DIM), 1) < DA_HEAD_DIM
    qt_scr[0] = jnp.transpose(jnp.where(comp0, q, 0.0)).astype(BF16)
    qt_scr[1] = jnp.transpose(jnp.where(comp0, 0.0, q)).astype(BF16)

    m_scr[...] = jnp.full_like(m_scr, -jnp.inf)
    acc_scr[...] = jnp.zeros_like(acc_scr)

    def stream(blocks):
        kbs = [kr_scr[pl.ds(pl.multiple_of(j * tq, tq), tq), :] for j, _ in blocks]
        vtbs = [vt_scr[j] for j, _ in blocks]
        chains = [(b, g, c) for b in range(len(blocks)) for g in range(tq // cg) for c in range(2)]

        def nkeys(b, g):
            return (g + 1) * cg if blocks[b][1] else tq

        def scores(b, g, c):
            return _dot(kbs[b][:nkeys(b, g)], qt_scr[c, :, g * cg:(g + 1) * cg])

        def softmax(b, g, c, s):
            cols = slice(g * cg, (g + 1) * cg)
            if blocks[b][1]:
                kpos = lax.broadcasted_iota(jnp.int32, s.shape, 0)
                qpos = lax.broadcasted_iota(jnp.int32, s.shape, 1) + g * cg
                s = jnp.where(kpos <= qpos, s, -jnp.inf)
            m_prev = m_scr[c, :, cols]
            m_new = jnp.maximum(m_prev, jnp.max(s, axis=0, keepdims=True))
            m_scr[c, :, cols] = m_new
            return jnp.exp2(m_prev - m_new), jnp.exp2(s - m_new).astype(BF16)

        def accumulate(b, g, c, alpha, p):
            cols = slice(g * cg, (g + 1) * cg)
            acc_scr[c, :, cols] = alpha * acc_scr[c, :, cols] + _dot(vtbs[b][:, :nkeys(b, g)], p)

        ahead, behind = min(3, len(chains)), 2
        s_q = [scores(*chains[i]) for i in range(ahead)]
        pending = []
        for i, ch in enumerate(chains):
            s_cur = s_q.pop(0)
            if i + ahead < len(chains):
                s_q.append(scores(*chains[i + ahead]))
            alpha, p = softmax(*ch, s_cur)
            pending.append((*ch, alpha, p))
            if len(pending) > behind:
                accumulate(*pending.pop(0))
        for item in pending:
            accumulate(*item)

    def body(j, carry):
        stream([(j, False)])
        return carry

    lax.fori_loop(0, qi, body, 0)
    stream([(qi, True)])

    lam = (jnp.exp(jnp.sum(lq1_ref[...] * lk1_ref[...], axis=-1, keepdims=True))
           - jnp.exp(jnp.sum(lq2_ref[...] * lk2_ref[...], axis=-1, keepdims=True)) + lambda_init)
    o1 = acc_scr[0, 0:DA_V_DIM, :] / acc_scr[0, DA_V_DIM:DA_V_DIM + 1, :]
    o2 = acc_scr[1, 0:DA_V_DIM, :] / acc_scr[1, DA_V_DIM:DA_V_DIM + 1, :]
    o = jnp.transpose(o1 - lam * o2)
    o = o * lax.rsqrt(jnp.mean(o * o, axis=-1, keepdims=True) + EPS) * g_ref[...]
    o = o * (1.0 - lambda_init) * _silu(z_ref[...].astype(F32))
    out_ref[...] = o.astype(BF16)


def _diff_attn_branch(y_main, bsz, seq, cos_t, sin_t, lq1, lk1, lq2, lk2, subln_g, lambda_init, tq, cg):
    ntok = bsz * seq
    tq = min(tq, seq)
    nq = seq // tq
    per_blk = D_BRANCH // DA_V_DIM

    def tile(blk):
        return pl.BlockSpec((tq, DA_V_DIM), lambda b, h, i: (b * nq + i, blk * per_blk + h))

    def full(blk):
        return pl.BlockSpec((seq, DA_V_DIM), lambda b, h, i: (b, blk * per_blk + h))

    def vec(n):
        return pl.BlockSpec((1, n), lambda b, h, i: (0, 0))

    q_tab = pl.BlockSpec((tq, LANES), lambda b, h, i: (i, 0))
    k_tab = pl.BlockSpec((seq, LANES), lambda b, h, i: (0, 0))
    lvec = [a.reshape(1, DA_HEAD_DIM).astype(F32) for a in (lq1, lk1, lq2, lk2)]
    return pl.pallas_call(
        functools.partial(_diff_attn_kernel, lambda_init=lambda_init, tq=tq, cg=min(cg, tq)),
        grid=(bsz, DA_HEADS, nq),
        in_specs=[tile(DA_BLK0), full(DA_BLK0 + 1), full(DA_BLK0 + 2), tile(DA_BLK0 + 3),
                  q_tab, q_tab, k_tab, k_tab,
                  vec(DA_HEAD_DIM), vec(DA_HEAD_DIM), vec(DA_HEAD_DIM), vec(DA_HEAD_DIM), vec(DA_V_DIM)],
        out_specs=pl.BlockSpec((tq, DA_V_DIM), lambda b, h, i: (b * nq + i, h)),
        out_shape=jax.ShapeDtypeStruct((ntok, D_BRANCH), BF16),
        scratch_shapes=[pltpu.VMEM((nq, VT_ROWS, tq), BF16), pltpu.VMEM((seq, DA_V_DIM), BF16),
                        pltpu.VMEM((2, DA_V_DIM, tq), BF16),
                        pltpu.VMEM((2, 1, tq), F32), pltpu.VMEM((2, VT_ROWS, tq), F32)],
        compiler_params=_params(("parallel", "parallel", "arbitrary")),
        name="diff_attn",
    )(y_main, y_main, y_main, y_main, cos_t, sin_t, cos_t, sin_t, *lvec,
      subln_g.reshape(1, DA_V_DIM).astype(F32))


def _mem_attn_kernel(q_ref, z_ref, k_ref, v_ref, out_ref):
    dh = XA_HEAD_DIM
    for h in range(XA_HEADS):
        sl = slice(h * dh, (h + 1) * dh)
        s = _dot_nt(q_ref[:, sl], k_ref[:, sl]) * (dh ** -0.5)
        p = jnp.exp(s - jnp.max(s, axis=-1, keepdims=True))
        p = p / jnp.sum(p, axis=-1, keepdims=True)
        o = _dot(p.astype(BF16), v_ref[:, sl])
        out_ref[:, sl] = (o * _silu(z_ref[:, sl].astype(F32))).astype(BF16)


def _mem_attn_branch(y_main, kv, bsz, seq, mem_len, tq):
    tq = min(tq, seq)
    nq = seq // tq
    return pl.pallas_call(
        _mem_attn_kernel,
        grid=(bsz, nq),
        in_specs=[pl.BlockSpec((tq, D_BRANCH), lambda b, i: (b * nq + i, XA_BLK0)),
                  pl.BlockSpec((tq, D_BRANCH), lambda b, i: (b * nq + i, XA_BLK0 + 1)),
                  pl.BlockSpec((mem_len, D_BRANCH), lambda b, i: (b, 0)),
                  pl.BlockSpec((mem_len, D_BRANCH), lambda b, i: (b, 1))],
        out_specs=pl.BlockSpec((tq, D_BRANCH), lambda b, i: (b * nq + i, 0)),
        out_shape=jax.ShapeDtypeStruct((bsz * seq, D_BRANCH), BF16),
        compiler_params=_params(("parallel", "parallel")),
        name="mem_attn",
    )(y_main, y_main, kv, kv)


def _merge_kernel(*refs):
    y_refs = refs[0:N_BRANCH]
    g_refs = refs[N_BRANCH:2 * N_BRANCH]
    w_refs = refs[2 * N_BRANCH:3 * N_BRANCH]
    out_ref, y0_scr, perm_scr = refs[3 * N_BRANCH:]

    @pl.when(pl.program_id(1) == 0)
    def _():
        rows = y_refs[0].shape[1]
        nslab = D_BRANCH // LANES
        for s in range(S5_T):
            ys = y_refs[0][s].astype(F32)
            for c in range(nslab):
                perm_scr[c, pl.ds(s, rows, stride=S5_T), :] = ys[:, c * LANES:(c + 1) * LANES]
        for c in range(nslab):
            y0_scr[:, c * LANES:(c + 1) * LANES] = perm_scr[c].astype(BF16)

    acc = jax.nn.sigmoid(g_refs[0][...].astype(F32)) * _dot(y0_scr[...], w_refs[0][0])
    for y_ref, g_ref, w_ref in zip(y_refs[1:], g_refs[1:], w_refs[1:]):
        acc = acc + jax.nn.sigmoid(g_ref[...].astype(F32)) * _dot(y_ref[...], w_ref[0])
    out_ref[...] = acc.astype(BF16)


def _merge(branches, y_main, w_branch, layer, tm, tn):
    ntok = y_main.shape[0]
    tm = min(tm, ntok)
    ncol = D_MODEL // tn
    y_specs = ([pl.BlockSpec((S5_T, tm // S5_T, D_BRANCH), lambda i, n: (0, i, 0))]
               + [pl.BlockSpec((tm, D_BRANCH), lambda i, n: (i, 0))] * (N_BRANCH - 1))
    g_specs = [pl.BlockSpec((tm, tn), lambda i, n, b=b: (i, (GATE_COL0 + b * D_MODEL) // tn + n))
               for b in range(N_BRANCH)]
    w_specs = [pl.BlockSpec((pl.Squeezed(), 1, D_BRANCH, tn), lambda i, n, b=b: (layer, b, 0, n))
               for b in range(N_BRANCH)]
    return pl.pallas_call(
        _merge_kernel,
        grid=(ntok // tm, ncol),
        in_specs=y_specs + g_specs + w_specs,
        out_specs=pl.BlockSpec((tm, tn), lambda i, n: (i, n)),
        out_shape=jax.ShapeDtypeStruct((ntok, D_MODEL), BF16),
        scratch_shapes=[pltpu.VMEM((tm, D_BRANCH), BF16), pltpu.VMEM((D_BRANCH // LANES, tm, LANES), F32)],
        compiler_params=_params(("parallel", "arbitrary")),
        name="merge",
    )(*branches, *([y_main] * N_BRANCH), *([w_branch] * N_BRANCH))


def _out_proj_kernel(m_ref, w_ref, g_ref, x_ref, out_ref):
    o = _dot(m_ref[...], w_ref[...])
    o = o * lax.rsqrt(jnp.mean(o * o, axis=-1, keepdims=True) + EPS) * g_ref[...]
    out_ref[...] = x_ref[...] + o


def _out_proj(merged, w_out, layer, g_post, x, tm):
    ntok = x.shape[0]
    tm = min(tm, ntok)
    return pl.pallas_call(
        _out_proj_kernel,
        grid=(ntok // tm,),
        in_specs=[pl.BlockSpec((tm, D_MODEL), lambda i: (i, 0)),
                  pl.BlockSpec((pl.Squeezed(), D_MODEL, D_MODEL), lambda i: (layer, 0, 0)),
                  pl.BlockSpec((1, D_MODEL), lambda i: (0, 0)),
                  pl.BlockSpec((tm, D_MODEL), lambda i: (i, 0))],
        out_specs=pl.BlockSpec((tm, D_MODEL), lambda i: (i, 0)),
        out_shape=jax.ShapeDtypeStruct((ntok, D_MODEL), F32),
        compiler_params=_params(("parallel",)),
        name="out_proj",
    )(merged, w_out, g_post.reshape(1, D_MODEL).astype(F32), x)


def kernel(x, mem, g_pre, w_in, s5_lam_re, s5_lam_im, s5_log_dt, s5_b_re, s5_b_im, s5_c_re, s5_c_im, s5_d, s5_w_glu, s5_b_glu, ml_conv_w, ml_conv_b, ml_b_i, ml_b_f, ml_norm_g, da_lq1, da_lk1, da_lq2, da_lk2, da_subln_g, g_mem, xa_w_kv, w_branch, w_out, g_post):
    bsz, seq, _ = x.shape
    mem_len = mem.shape[1]
    depth = w_in.shape[0]
    ntok = bsz * seq
    cos_t, sin_t = _rope_tables(seq)
    xf = x.reshape(ntok, D_MODEL).astype(F32)
    memf = mem.reshape(bsz * mem_len, D_MODEL).astype(F32)
    gate0 = 7 * D_BRANCH
    expanders = _s5_expanders()
    s5_ops = jax.vmap(_s5_operators)(s5_lam_re, s5_lam_im, s5_log_dt, s5_b_re, s5_b_im, s5_c_re, s5_c_im)
    w_branch_b = w_branch.astype(BF16)
    w_out_b = w_out.astype(BF16)
    w_kv_b = xa_w_kv.astype(BF16)
    w_glu_b = s5_w_glu.astype(BF16)
    gate_lane = jnp.arange(LANES) < 2 * ML_HEADS
    for l in range(depth):
        lambda_init = 0.8 - 0.6 * math.exp(-0.3 * l)
        w_l = w_in[l]
        w_main = jnp.concatenate([w_l[:, :gate0], w_l[:, gate0 + 2 * ML_HEADS:]], axis=1).astype(BF16)
        w_gate = jnp.where(gate_lane[None, :], w_l[:, gate0:gate0 + LANES], 0.0).astype(BF16)
        y_main, s5_in, gates = _in_proj(xf, g_pre[l], w_main, w_gate, tm=1024, tn=1024)
        kv = _norm_proj(memf, g_mem[l], w_kv_b, l, tm=1024, tn=1024)

        y_s5 = _s5_branch(s5_in, bsz, seq, s5_ops, l, expanders, s5_d[l].astype(F32), w_glu_b, s5_b_glu[l],
                          rt=512)
        y_ml = _mlstm_branch(y_main, gates, bsz, seq, ml_conv_w[l], ml_conv_b[l], ml_b_i[l], ml_b_f[l],
                             ml_norm_g[l])
        y_da = _diff_attn_branch(y_main, bsz, seq, cos_t, sin_t, da_lq1[l], da_lk1[l], da_lq2[l], da_lk2[l],
                                 da_subln_g[l], lambda_init, tq=2048, cg=256)
        y_xa = _mem_attn_branch(y_main, kv, bsz, seq, mem_len, tq=512)

        merged = _merge((y_s5, y_ml, y_da, y_xa), y_main, w_branch_b, l, tm=1024, tn=512)
        xf = _out_proj(merged, w_out_b, l, g_post[l], xf, tm=512)
    return xf.reshape(bsz, seq, D_MODEL).astype(x.dtype)
```

```python
import functools
import math

import jax
import jax.numpy as jnp
from jax import lax
from jax.experimental import pallas as pl
from jax.experimental.pallas import tpu as pltpu

F32 = jnp.float32
BF16 = jnp.bfloat16

D_MODEL = 2048
D_BRANCH = 1024
N_BRANCH = 4
S5_GROUP = 16
S5_GROUPS = D_BRANCH // S5_GROUP
S5_STATE = 64
ML_HEADS = 4
ML_HEAD_DIM = D_BRANCH // ML_HEADS
ML_CHUNK = 128
ML_CONV = 4
ML_TAIL = 16
DA_HEADS = 8
DA_HEAD_DIM = 64
DA_V_DIM = 2 * DA_HEAD_DIM
XA_HEADS = 4
XA_HEAD_DIM = D_BRANCH // XA_HEADS
ROPE_THETA = 10000.0
EPS = 1e-6
LOG2E = 1.4426950408889634

LANES = 128
VT_ROWS = DA_V_DIM + 16
S5_T = 16
S5_LANE_GROUPS = LANES // S5_GROUP
S5_NBLK = D_BRANCH // LANES
S5_SBLK = S5_LANE_GROUPS * S5_STATE
S5_SLAB = S5_SBLK // LANES
N_S5IN = 2 * D_BRANCH
N_MAIN = 11 * D_BRANCH + N_BRANCH * D_MODEL
ML_BLK0, DA_BLK0, XA_BLK0 = 0, 5, 9
GATE_COL0 = 11 * D_BRANCH
MIB = 1024 * 1024
VMEM_LIMIT = 48 * MIB


TILES = dict(
    in_proj=(1024, 1024),
    kv_proj=(1024, 1024),
    s5_rows=512,
    s5_glu_rows=1024,
    attn_q=2048,
    attn_cols=256,
    mem_q=512,
    merge=(1024, 512),
    out_rows=512,
)


def _params(sem, vmem=VMEM_LIMIT):
    return pltpu.CompilerParams(dimension_semantics=sem, vmem_limit_bytes=vmem)


def _silu(x):
    return x * jax.nn.sigmoid(x)


def _dot(a, b):
    return jnp.dot(a, b, preferred_element_type=F32)


def _dot_nt(a, b):
    return lax.dot_general(a, b, (((1,), (1,)), ((), ())), preferred_element_type=F32)


def _rms_norm_bf16(x, g):
    ms = jnp.mean(x * x, axis=-1, keepdims=True)
    return (x * lax.rsqrt(ms + EPS) * g).astype(BF16)


def _norm_proj_kernel(x_ref, g_ref, w_ref, y_ref, h_scr):
    @pl.when(pl.program_id(1) == 0)
    def _():
        h_scr[...] = _rms_norm_bf16(x_ref[...], g_ref[...])

    y_ref[...] = _dot(h_scr[...], w_ref[...]).astype(y_ref.dtype)


def _norm_proj(x, g, w, layer, tm, tn):
    m, k = x.shape
    n = w.shape[2]
    tm = min(tm, m)
    return pl.pallas_call(
        _norm_proj_kernel,
        grid=(m // tm, n // tn),
        in_specs=[pl.BlockSpec((tm, k), lambda i, j: (i, 0)),
                  pl.BlockSpec((1, k), lambda i, j: (0, 0)),
                  pl.BlockSpec((pl.Squeezed(), k, tn), lambda i, j: (layer, 0, j))],
        out_specs=pl.BlockSpec((tm, tn), lambda i, j: (i, j)),
        out_shape=jax.ShapeDtypeStruct((m, n), BF16),
        scratch_shapes=[pltpu.VMEM((tm, k), BF16)],
        compiler_params=_params(("parallel", "arbitrary")),
        name="norm_proj",
    )(x, g.reshape(1, k), w)


def _in_proj_kernel(x_ref, g_ref, w_ref, wa_ref, y_ref, s5_ref, a_ref, h_scr, perm_scr):
    j = pl.program_id(1)
    tm, tn = y_ref.shape
    n_s5 = N_S5IN // tn

    @pl.when(j == 0)
    def _():
        h = _rms_norm_bf16(x_ref[...], g_ref[...])
        h_scr[...] = h
        a_ref[...] = _dot(h, wa_ref[...])

    @pl.when(j < n_s5)
    def _():
        acc = _dot(h_scr[...], w_ref[...])
        for c in range(tn // LANES):
            perm_scr[c] = acc[:, c * LANES:(c + 1) * LANES]
        for s in range(S5_T):
            for c in range(tn // LANES):
                rows = perm_scr[c, pl.ds(s, tm // S5_T, stride=S5_T), :]
                s5_ref[s, :, c * LANES:(c + 1) * LANES] = rows.astype(BF16)

    @pl.when(j >= n_s5)
    def _():
        y_ref[...] = _dot(h_scr[...], w_ref[...]).astype(BF16)


def _in_proj(x, g, w, w_aux, tm, tn):
    m, k = x.shape
    tm = min(tm, m)
    n_s5 = N_S5IN // tn
    return pl.pallas_call(
        _in_proj_kernel,
        grid=(m // tm, (N_S5IN + N_MAIN) // tn),
        in_specs=[pl.BlockSpec((tm, k), lambda i, j: (i, 0)),
                  pl.BlockSpec((1, k), lambda i, j: (0, 0)),
                  pl.BlockSpec((k, tn), lambda i, j: (0, j)),
                  pl.BlockSpec((k, LANES), lambda i, j: (0, 0))],
        out_specs=[pl.BlockSpec((tm, tn), lambda i, j: (i, jnp.maximum(j - n_s5, 0))),
                   pl.BlockSpec((S5_T, tm // S5_T, tn), lambda i, j: (0, i, jnp.minimum(j, n_s5 - 1))),
                   pl.BlockSpec((tm, LANES), lambda i, j: (i, 0))],
        out_shape=[jax.ShapeDtypeStruct((m, N_MAIN), BF16),
                   jax.ShapeDtypeStruct((S5_T, m // S5_T, N_S5IN), BF16),
                   jax.ShapeDtypeStruct((m, LANES), F32)],
        scratch_shapes=[pltpu.VMEM((tm, k), BF16), pltpu.VMEM((tn // LANES, tm, LANES), F32)],
        compiler_params=_params(("parallel", "arbitrary"), 56 * MIB),
        name="in_proj",
    )(x, g.reshape(1, k), w, w_aux)


def _s5_operators(lam_re, lam_im, log_dt, b_re, b_im, c_re, c_im):
    hi = lax.Precision.HIGHEST
    G, P, C, T, J, LG = S5_GROUPS, S5_STATE, S5_GROUP, S5_T, S5_NBLK, S5_LANE_GROUPS
    lam = lax.complex(jnp.minimum(lam_re.astype(F32), -1e-4), lam_im.astype(F32))
    dt = jnp.exp(log_dt.astype(F32))[:, None]
    z = lam * dt
    lam_bar = jnp.exp(z)
    b_bar = ((lam_bar - 1.0) / lam)[..., None] * lax.complex(b_re.astype(F32), b_im.astype(F32))
    c = lax.complex(c_re.astype(F32), c_im.astype(F32))
    steps = jnp.arange(T + 1, dtype=F32)
    pw = jnp.exp(z[None] * steps[:, None, None].astype(jnp.complex64))

    kern = jnp.real(jnp.einsum('gop,kgp,gpi->kgoi', c, pw[:T], b_bar, precision=hi))
    kc = kern.reshape(T, J, LG, C, C).transpose(1, 2, 4, 0, 3).reshape(J, LANES, T * C)

    bs = pw[T - 1 - jnp.arange(T)][..., None] * b_bar[None]
    bs = jnp.stack([jnp.real(bs), jnp.imag(bs)], axis=0)
    bs = bs.reshape(2, T, J, LG, P, C).transpose(2, 1, 3, 5, 0, 4)
    b2 = bs.reshape(J, T * LANES, 2 * P).astype(BF16)

    cs = c[None] * pw[1:T + 1][:, :, None, :]
    cs = jnp.stack([jnp.real(cs), -jnp.imag(cs)], axis=0)
    cs = cs.reshape(2, T, J, LG, C, P).transpose(2, 0, 3, 5, 1, 4)
    c2 = cs.reshape(J, 2 * S5_SBLK, T * C).astype(BF16)

    lam_t = pw[T].reshape(1, G * P)
    return kc, b2, c2, jnp.real(lam_t), jnp.imag(lam_t)


def _s5_expanders():
    r = jnp.arange(S5_T * S5_GROUP)[:, None]
    c = jnp.arange(S5_T * LANES)[None, :]
    e_to = ((r // S5_GROUP == c // LANES) & (r % S5_GROUP == c % S5_GROUP)).astype(BF16)
    r = jnp.arange(2 * S5_STATE)[:, None]
    c = jnp.arange(2 * S5_SBLK)[None, :]
    e_rp = ((r // S5_STATE == c // S5_SBLK) & (r % S5_STATE == c % S5_STATE)).astype(BF16)
    return e_to, e_rp


def _expand_block_diag(dst_scr, compact, exp_ref, row_group, col_group):
    rows = compact.shape[0]
    cols = exp_ref.shape[1]
    slab = 2 * LANES
    rg = (lax.broadcasted_iota(jnp.int32, (rows, slab), 0) // row_group) % S5_LANE_GROUPS
    cl = lax.broadcasted_iota(jnp.int32, (rows, slab), 1)
    for c0 in range(0, cols, slab):
        cg = ((cl + c0) // col_group) % S5_LANE_GROUPS
        blk = _dot(compact, exp_ref[:, c0:c0 + slab])
        dst_scr[:, c0:c0 + slab] = jnp.where(rg == cg, blk, 0.0).astype(BF16)


def _s5_local_kernel(u_ref, b2_ref, erp_ref, s_ref, min_scr):
    @pl.when(pl.program_id(1) == 0)
    def _():
        _expand_block_diag(min_scr, b2_ref[0], erp_ref, S5_GROUP, S5_STATE)

    ucat = jnp.concatenate([u_ref[s] for s in range(S5_T)], axis=1)
    s = _dot(ucat, min_scr[...])
    for c in range(2 * S5_SLAB):
        s_ref[c] = s[:, c * LANES:(c + 1) * LANES]


def _s5_scan_kernel(sr_ref, si_ref, ar_ref, ai_ref, xr_ref, xi_ref, *, bsz, nchunk):
    ar = [jnp.broadcast_to(ar_ref[:, c * LANES:(c + 1) * LANES], (bsz, LANES)) for c in range(S5_SLAB)]
    ai = [jnp.broadcast_to(ai_ref[:, c * LANES:(c + 1) * LANES], (bsz, LANES)) for c in range(S5_SLAB)]

    def step(k, carry):
        rows = pl.ds(k, bsz, stride=nchunk)
        new = []
        for c in range(S5_SLAB):
            xr, xi = carry[c]
            xr_ref[c, rows, :] = xr
            xi_ref[c, rows, :] = xi
            nr = ar[c] * xr - ai[c] * xi + sr_ref[c, rows, :]
            ni = ar[c] * xi + ai[c] * xr + si_ref[c, rows, :]
            new.append((nr, ni))
        return tuple(new)

    zero = jnp.zeros((bsz, LANES), F32)
    lax.fori_loop(0, nchunk, step, tuple((zero, zero) for _ in range(S5_SLAB)), unroll=4)


def _s5_out_kernel(u_ref, kc_ref, c2_ref, eto_ref, xr_ref, xi_ref, d_ref, out_ref, mi_scr, mo_scr, a2_scr):
    @pl.when(pl.program_id(1) == 0)
    def _():
        kc = kc_ref[0]
        lane = lax.broadcasted_iota(jnp.int32, kc.shape, 1)
        for s in range(S5_T):
            shifted = kc if s == 0 else jnp.where(lane >= s * S5_GROUP, pltpu.roll(kc, s * S5_GROUP, 1), 0.0)
            a2_scr[s * LANES:(s + 1) * LANES, :] = shifted.astype(BF16)
        _expand_block_diag(mi_scr, a2_scr[...], eto_ref, S5_GROUP, S5_GROUP)
        _expand_block_diag(mo_scr, c2_ref[0], eto_ref, S5_STATE, S5_GROUP)

    us = [u_ref[s] for s in range(S5_T)]
    ucat = jnp.concatenate(us, axis=1)
    xcat = jnp.concatenate([xr_ref[c] for c in range(S5_SLAB)] + [xi_ref[c] for c in range(S5_SLAB)],
                           axis=1).astype(BF16)
    d = d_ref[...]
    tile = 2 * LANES
    for c0 in range(0, S5_T * LANES, tile):
        y = _dot(ucat[:, :c0 + tile], mi_scr[:c0 + tile, c0:c0 + tile]) + _dot(xcat, mo_scr[:, c0:c0 + tile])
        for t in range(c0 // LANES, (c0 + tile) // LANES):
            yt = y[:, t * LANES - c0:(t + 1) * LANES - c0] + d * us[t].astype(F32)
            out_ref[t] = jax.nn.gelu(yt).astype(BF16)


def _s5_glu_kernel(g_ref, z_ref, w_ref, b_ref, out_ref):
    g = g_ref[...]
    a = _dot(g, w_ref[...]) + b_ref[...]
    out_ref[...] = (g.astype(F32) * jax.nn.sigmoid(a) * _silu(z_ref[...].astype(F32))).astype(BF16)


def _s5_branch(s5_in, bsz, seq, ops, layer, expanders, d_skip, w_glu, b_glu, rt, rt_glu):
    kc, b2, c2, lam_r, lam_i = ops
    e_to, e_rp = expanders
    nrow = bsz * seq // S5_T
    nchunk = seq // S5_T
    rt = min(rt, nrow)
    u_spec = pl.BlockSpec((S5_T, rt, LANES), lambda j, r: (0, r, j))

    def whole(a):
        return pl.BlockSpec(a.shape, lambda j, r: (0,) * a.ndim)

    def per_blk(a):
        return pl.BlockSpec((pl.Squeezed(), 1) + a.shape[2:], lambda j, r: (layer, j, 0, 0))

    s_loc = pl.pallas_call(
        _s5_local_kernel,
        grid=(S5_NBLK, nrow // rt),
        in_specs=[u_spec, per_blk(b2), whole(e_rp)],
        out_specs=pl.BlockSpec((2 * S5_SLAB, rt, LANES), lambda j, r: (j, r, 0)),
        out_shape=jax.ShapeDtypeStruct((S5_NBLK * 2 * S5_SLAB, nrow, LANES), F32),
        scratch_shapes=[pltpu.VMEM((S5_T * LANES, 2 * S5_SBLK), BF16)],
        compiler_params=_params(("parallel", "arbitrary")),
        name="s5_local",
    )(s5_in, b2, e_rp)

    xr, xi = pl.pallas_call(
        functools.partial(_s5_scan_kernel, bsz=bsz, nchunk=nchunk),
        grid=(S5_NBLK,),
        in_specs=[pl.BlockSpec((S5_SLAB, nrow, LANES), lambda j: (2 * j, 0, 0)),
                  pl.BlockSpec((S5_SLAB, nrow, LANES), lambda j: (2 * j + 1, 0, 0)),
                  pl.BlockSpec((pl.Squeezed(), 1, S5_SBLK), lambda j: (layer, 0, j)),
                  pl.BlockSpec((pl.Squeezed(), 1, S5_SBLK), lambda j: (layer, 0, j))],
        out_specs=[pl.BlockSpec((S5_SLAB, nrow, LANES), lambda j: (j, 0, 0))] * 2,
        out_shape=[jax.ShapeDtypeStruct((S5_NBLK * S5_SLAB, nrow, LANES), F32)] * 2,
        compiler_params=_params(("parallel",)),
        name="s5_scan",
    )(s_loc, s_loc, lam_r, lam_i)

    gel = pl.pallas_call(
        _s5_out_kernel,
        grid=(S5_NBLK, nrow // rt),
        in_specs=[u_spec, per_blk(kc), per_blk(c2), whole(e_to),
                  pl.BlockSpec((S5_SLAB, rt, LANES), lambda j, r: (j, r, 0)),
                  pl.BlockSpec((S5_SLAB, rt, LANES), lambda j, r: (j, r, 0)),
                  pl.BlockSpec((1, LANES), lambda j, r: (0, j))],
        out_specs=pl.BlockSpec((S5_T, rt, LANES), lambda j, r: (0, r, j)),
        out_shape=jax.ShapeDtypeStruct((S5_T, nrow, D_BRANCH), BF16),
        scratch_shapes=[pltpu.VMEM((S5_T * LANES, S5_T * LANES), BF16),
                        pltpu.VMEM((2 * S5_SBLK, S5_T * LANES), BF16),
                        pltpu.VMEM((S5_T * LANES, S5_T * S5_GROUP), BF16)],
        compiler_params=_params(("parallel", "arbitrary")),
        name="s5_out",
    )(s5_in, kc, c2, e_to, xr, xi, d_skip.reshape(1, D_BRANCH))

    rt_glu = min(rt_glu, nrow)
    row_blk = pl.BlockSpec((pl.Squeezed(), rt_glu, D_BRANCH), lambda t, r: (t, r, 0))
    return pl.pallas_call(
        _s5_glu_kernel,
        grid=(S5_T, nrow // rt_glu),
        in_specs=[row_blk,
                  pl.BlockSpec((pl.Squeezed(), rt_glu, D_BRANCH), lambda t, r: (t, r, 1)),
                  pl.BlockSpec((pl.Squeezed(), D_BRANCH, D_BRANCH), lambda t, r: (layer, 0, 0)),
                  pl.BlockSpec((1, D_BRANCH), lambda t, r: (0, 0))],
        out_specs=row_blk,
        out_shape=jax.ShapeDtypeStruct((S5_T, nrow, D_BRANCH), BF16),
        compiler_params=_params(("parallel", "parallel")),
        name="s5_glu",
    )(gel, s5_in, w_glu, b_glu.reshape(1, D_BRANCH).astype(F32))


def _split3(x):
    hi = x.astype(BF16)
    r1 = x - hi.astype(F32)
    mid = r1.astype(BF16)
    lo = (r1 - mid.astype(F32)).astype(BF16)
    return hi, mid, lo


def _mlstm_kernel(q_ref, k_ref, v_ref, o_ref, z_ref, if_ref, cw_ref, cb_ref, bif_ref, ng_ref, out_ref,
                  ext_scr, c_scr, n_scr, m_scr):
    lc, dh, nh = ML_CHUNK, ML_HEAD_DIM, ML_HEADS
    tail = ML_TAIL
    cidx = pl.program_id(1)

    @pl.when(cidx == 0)
    def _():
        ext_scr[0:tail, :] = jnp.zeros((tail, 2 * D_BRANCH), BF16)
        c_scr[...] = jnp.zeros_like(c_scr)
        n_scr[...] = jnp.zeros_like(n_scr)
        m_scr[...] = jnp.zeros_like(m_scr)

    @pl.when(cidx > 0)
    def _():
        ext_scr[0:tail, :] = ext_scr[lc:lc + tail, :]

    ext_scr[tail:tail + lc, 0:D_BRANCH] = q_ref[...]
    ext_scr[tail:tail + lc, D_BRANCH:2 * D_BRANCH] = k_ref[...]
    ext = ext_scr[...]
    srow = lax.broadcasted_iota(jnp.int32, (lc, tail + lc), 0)
    scol = lax.broadcasted_iota(jnp.int32, (lc, tail + lc), 1)
    conv = cb_ref[...] + cw_ref[ML_CONV - 1:ML_CONV, :] * ext[tail:tail + lc].astype(F32)
    for d in range(1, ML_CONV):
        shift = jnp.where(scol == srow + (tail - d), 1.0, 0.0).astype(BF16)
        conv = conv + cw_ref[ML_CONV - 1 - d:ML_CONV - d, :] * _dot(shift, ext)
    qk = _silu(conv)

    gts = if_ref[...] + bif_ref[...]
    lf = jnp.minimum(gts, 0.0) - jnp.log(1.0 + jnp.exp(-jnp.abs(gts)))
    gts_t = jnp.transpose(gts)[0:16, :]
    lf_t = jnp.transpose(lf)[0:16, :]
    row = lax.broadcasted_iota(jnp.int32, (lc, lc), 0)
    col = lax.broadcasted_iota(jnp.int32, (lc, lc), 1)
    causal = col <= row
    tril = jnp.where(causal, 1.0, 0.0).astype(BF16)
    triu = jnp.where(row <= col, 1.0, 0.0).astype(BF16)
    acum_c = sum(_dot(tril, p) for p in _split3(lf))
    acum_r = sum(_dot(p, triu) for p in _split3(lf_t))

    for h in range(nh):
        sl = slice(h * dh, (h + 1) * dh)
        qh = qk[:, h * dh:(h + 1) * dh]
        kh = qk[:, D_BRANCH + h * dh:D_BRANCH + (h + 1) * dh] * (dh ** -0.5)
        vh = v_ref[:, sl]
        qb = qh.astype(BF16)
        a_c = acum_c[:, nh + h:nh + h + 1]
        a_r = acum_r[nh + h:nh + h + 1, :]
        i_c = gts[:, h:h + 1]
        i_r = gts_t[h:h + 1, :]
        gtot = a_c[lc - 1:lc, :]
        m_old = m_scr[h][:, 0:1]
        dmat = jnp.where(causal, a_c - a_r + i_r, -jnp.inf)
        inter = a_c + m_old
        m_row = jnp.maximum(jnp.max(dmat, axis=-1, keepdims=True), inter)
        s = _dot_nt(qb, kh.astype(BF16)) * jnp.exp(dmat - m_row)
        sc = jnp.exp(inter - m_row)
        num = _dot(s.astype(BF16), vh) + sc * _dot(qb, c_scr[h].astype(BF16))
        den = jnp.sum(s, axis=-1, keepdims=True) + sc * jnp.sum(qh * n_scr[h], axis=-1, keepdims=True)
        hout = num / jnp.maximum(jnp.abs(den), jnp.exp(-m_row))

        kw_log = gtot - a_c + i_c
        m_new = jnp.maximum(gtot + m_old, jnp.max(kw_log, axis=0, keepdims=True))
        kw = jnp.exp(kw_log - m_new)
        decay = jnp.exp(gtot + m_old - m_new)
        kk = kh * kw
        c_scr[h] = decay * c_scr[h] + _dot(jnp.transpose(kk).astype(BF16), vh)
        n_scr[h] = decay * n_scr[h] + jnp.sum(kk, axis=0, keepdims=True)
        m_scr[h] = jnp.broadcast_to(m_new, (1, LANES))

        hn = hout * lax.rsqrt(jnp.mean(hout * hout, axis=-1, keepdims=True) + EPS) * ng_ref[:, sl]
        y = jax.nn.sigmoid(o_ref[:, sl].astype(F32)) * hn * _silu(z_ref[:, sl].astype(F32))
        out_ref[:, sl] = y.astype(BF16)


def _mlstm_branch(y_main, gates, bsz, seq, conv_w, conv_b, b_i, b_f, norm_g):
    lc = ML_CHUNK
    nc = seq // lc
    bif = jnp.zeros((1, LANES), F32).at[0, 0:ML_HEADS].set(b_i.astype(F32))
    bif = bif.at[0, ML_HEADS:2 * ML_HEADS].set(b_f.astype(F32))

    def col(cb):
        return pl.BlockSpec((lc, D_BRANCH), lambda b, c: (b * nc + c, cb))

    def const(shape):
        return pl.BlockSpec(shape, lambda b, c: (0,) * len(shape))

    return pl.pallas_call(
        _mlstm_kernel,
        grid=(bsz, nc),
        in_specs=[col(ML_BLK0), col(ML_BLK0 + 1), col(ML_BLK0 + 2), col(ML_BLK0 + 3), col(ML_BLK0 + 4),
                  pl.BlockSpec((lc, LANES), lambda b, c: (b * nc + c, 0)),
                  const((ML_CONV, 2 * D_BRANCH)), const((1, 2 * D_BRANCH)), const((1, LANES)),
                  const((1, D_BRANCH))],
        out_specs=pl.BlockSpec((lc, D_BRANCH), lambda b, c: (b * nc + c, 0)),
        out_shape=jax.ShapeDtypeStruct((bsz * seq, D_BRANCH), BF16),
        scratch_shapes=[pltpu.VMEM((lc + ML_TAIL, 2 * D_BRANCH), BF16),
                        pltpu.VMEM((ML_HEADS, ML_HEAD_DIM, ML_HEAD_DIM), F32),
                        pltpu.VMEM((ML_HEADS, 1, ML_HEAD_DIM), F32),
                        pltpu.VMEM((ML_HEADS, 1, LANES), F32)],
        compiler_params=_params(("parallel", "arbitrary")),
        name="mlstm",
    )(y_main, y_main, y_main, y_main, y_main, gates,
      conv_w.astype(F32), conv_b.reshape(1, 2 * D_BRANCH).astype(F32), bif,
      norm_g.reshape(1, D_BRANCH).astype(F32))


def _rope_tables(seq):
    inv =1.0 / (ROPE_THETA ** (jnp.arange(0, DA_HEAD_DIM, 2, dtype=F32) / DA_HEAD_DIM))
    ang = jnp.arange(seq, dtype=F32)[:, None] * inv[None, :]
    cos, sin = jnp.cos(ang), jnp.sin(ang)
    cos_t = jnp.tile(jnp.concatenate([cos, cos], axis=1), (1, LANES // DA_HEAD_DIM))
    sin_t = jnp.tile(jnp.concatenate([-sin, sin], axis=1), (1, LANES // DA_HEAD_DIM))
    return cos_t, sin_t


def _rope(x, cos, sin):
    half = DA_HEAD_DIM // 2
    lane = lax.broadcasted_iota(jnp.int32, (1, LANES), 1)
    first_half = (lane % DA_HEAD_DIM) < half
    partner = jnp.where(first_half, pltpu.roll(x, LANES - half, 1), pltpu.roll(x, half, 1))
    return x * cos + partner * sin


def _diff_attn_kernel(q_ref, k_ref, v_ref, z_ref, cosq_ref, sinq_ref, cosk_ref, sink_ref,
                      lq1_ref, lk1_ref, lq2_ref, lk2_ref, g_ref,
                      out_ref, vt_scr, kr_scr, qt_scr, m_scr, acc_scr, *, lambda_init, tq, cg):
    qi = pl.program_id(2)
    nkb = vt_scr.shape[0]

    @pl.when(qi == 0)
    def _():
        kr_scr[...] = _rope(k_ref[...].astype(F32), cosk_ref[...], sink_ref[...]).astype(BF16)
        ones_row = lax.broadcasted_iota(jnp.int32, (VT_ROWS - DA_V_DIM, tq), 0) == 0
        for jb in range(nkb):
            vt_scr[jb, 0:DA_V_DIM, :] = jnp.transpose(v_ref[jb * tq:(jb + 1) * tq, :].astype(F32)).astype(BF16)
            vt_scr[jb, DA_V_DIM:VT_ROWS, :] = jnp.where(ones_row, 1.0, 0.0).astype(BF16)

    q = _rope(q_ref[...].astype(F32), cosq_ref[...], sinq_ref[...]) * (DA_HEAD_DIM ** -0.5 * LOG2E)
    comp0 = lax.broadcasted_iota(jnp.int32, (1, DA_V_DIM), 1) < DA_HEAD_DIM
    qt_scr[0] = jnp.transpose(jnp.where(comp0, q, 0.0)).astype(BF16)
    qt_scr[1] = jnp.transpose(jnp.where(comp0, 0.0, q)).astype(BF16)

    m_scr[...] = jnp.full_like(m_scr, -jnp.inf)
    acc_scr[...] = jnp.zeros_like(acc_scr)

    def stream(blocks):
        kbs = [kr_scr[pl.ds(pl.multiple_of(j * tq, tq), tq), :] for j, _ in blocks]
        vtbs = [vt_scr[j] for j, _ in blocks]
        chains = [(b, g, c) for b in range(len(blocks)) for g in range(tq // cg) for c in range(2)]

        def nkeys(b, g):
            return (g + 1) * cg if blocks[b][1] else tq

        def scores(b, g, c):
            return _dot(kbs[b][:nkeys(b, g)], qt_scr[c, :, g * cg:(g + 1) * cg])

        def softmax(b, g, c, s):
            cols = slice(g * cg, (g + 1) * cg)
            if blocks[b][1]:
                kpos = lax.broadcasted_iota(jnp.int32, s.shape, 0)
                qpos = lax.broadcasted_iota(jnp.int32, s.shape, 1) + g * cg
                s = jnp.where(kpos <= qpos, s, -jnp.inf)
            m_prev = m_scr[c, :, cols]
            m_new = jnp.maximum(m_prev, jnp.max(s, axis=0, keepdims=True))
            m_scr[c, :, cols] = m_new
            return jnp.exp2(m_prev - m_new), jnp.exp2(s - m_new).astype(BF16)

        def accumulate(b, g, c, alpha, p):
            cols = slice(g * cg, (g + 1) * cg)
            acc_scr[c, :, cols] = alpha * acc_scr[c, :, cols] + _dot(vtbs[b][:, :nkeys(b, g)], p)

        ahead, behind = min(3, len(chains)), 2
        s_q = [scores(*chains[i]) for i in range(ahead)]
        pending = []
        for i, ch in enumerate(chains):
            s_cur = s_q.pop(0)
            if i + ahead < len(chains):
                s_q.append(scores(*chains[i + ahead]))
            alpha, p = softmax(*ch, s_cur)
            pending.append((*ch, alpha, p))
            if len(pending) > behind:
                accumulate(*pending.pop(0))
        for item in pending:
            accumulate(*item)

    def body(j, carry):
        stream([(j, False)])
        return carry

    lax.fori_loop(0, qi, body, 0)
    stream([(qi, True)])

    lam = (jnp.exp(jnp.sum(lq1_ref[...] * lk1_ref[...], axis=-1, keepdims=True))
           - jnp.exp(jnp.sum(lq2_ref[...] * lk2_ref[...], axis=-1, keepdims=True)) + lambda_init)
    o1 = acc_scr[0, 0:DA_V_DIM, :] / acc_scr[0, DA_V_DIM:DA_V_DIM + 1, :]
    o2 = acc_scr[1, 0:DA_V_DIM, :] / acc_scr[1, DA_V_DIM:DA_V_DIM + 1, :]
    o = jnp.transpose(o1 - lam * o2)
    o = o * lax.rsqrt(jnp.mean(o * o, axis=-1, keepdims=True) + EPS) * g_ref[...]
    o = o * (1.0 - lambda_init) * _silu(z_ref[...].astype(F32))
    out_ref[...] = o.astype(BF16)


def _diff_attn_branch(y_main, bsz, seq, cos_t, sin_t, lq1, lk1, lq2, lk2, subln_g, lambda_init, tq, cg):
    ntok = bsz * seq
    tq = min(tq, seq)
    nq = seq // tq
    per_blk = D_BRANCH // DA_V_DIM

    def tile(blk):
        return pl.BlockSpec((tq, DA_V_DIM), lambda b, h, i: (b * nq + i, blk * per_blk + h))

    def full(blk):
        return pl.BlockSpec((seq, DA_V_DIM), lambda b, h, i: (b, blk * per_blk + h))

    def vec(n):
        return pl.BlockSpec((1, n), lambda b, h, i: (0, 0))

    q_tab = pl.BlockSpec((tq, LANES), lambda b, h, i: (i, 0))
    k_tab = pl.BlockSpec((seq, LANES), lambda b, h, i: (0, 0))
    lvec = [a.reshape(1, DA_HEAD_DIM).astype(F32) for a in (lq1, lk1, lq2, lk2)]
    return pl.pallas_call(
        functools.partial(_diff_attn_kernel, lambda_init=lambda_init, tq=tq, cg=min(cg, tq)),
        grid=(bsz, DA_HEADS, nq),
        in_specs=[tile(DA_BLK0), full(DA_BLK0 + 1), full(DA_BLK0 + 2), tile(DA_BLK0 + 3),
                  q_tab, q_tab, k_tab, k_tab,
                  vec(DA_HEAD_DIM), vec(DA_HEAD_DIM), vec(DA_HEAD_DIM), vec(DA_HEAD_DIM), vec(DA_V_DIM)],
        out_specs=pl.BlockSpec((tq, DA_V_DIM), lambda b, h, i: (b * nq + i, h)),
        out_shape=jax.ShapeDtypeStruct((ntok, D_BRANCH), BF16),
        scratch_shapes=[pltpu.VMEM((nq, VT_ROWS, tq), BF16), pltpu.VMEM((seq, DA_V_DIM), BF16),
                        pltpu.VMEM((2, DA_V_DIM, tq), BF16),
                        pltpu.VMEM((2, 1, tq), F32), pltpu.VMEM((2, VT_ROWS, tq), F32)],
        compiler_params=_params(("parallel", "parallel", "arbitrary")),
        name="diff_attn",
    )(y_main, y_main, y_main, y_main, cos_t, sin_t, cos_t, sin_t, *lvec,
      subln_g.reshape(1, DA_V_DIM).astype(F32))


def _mem_attn_kernel(q_ref, z_ref, k_ref, v_ref, out_ref):
    dh = XA_HEAD_DIM
    for h in range(XA_HEADS):
        sl = slice(h * dh, (h + 1) * dh)
        s = _dot_nt(q_ref[:, sl], k_ref[:, sl]) * (dh ** -0.5)
        p = jnp.exp(s - jnp.max(s, axis=-1, keepdims=True))
        p = p / jnp.sum(p, axis=-1, keepdims=True)
        o = _dot(p.astype(BF16), v_ref[:, sl])
        out_ref[:, sl] = (o * _silu(z_ref[:, sl].astype(F32))).astype(BF16)


def _mem_attn_branch(y_main, kv, bsz, seq, mem_len, tq):
    tq = min(tq, seq)
    nq = seq // tq
    return pl.pallas_call(
        _mem_attn_kernel,
        grid=(bsz, nq),
        in_specs=[pl.BlockSpec((tq, D_BRANCH), lambda b, i: (b * nq + i, XA_BLK0)),
                  pl.BlockSpec((tq, D_BRANCH), lambda b, i: (b * nq + i, XA_BLK0 + 1)),
                  pl.BlockSpec((mem_len, D_BRANCH), lambda b, i: (b, 0)),
                  pl.BlockSpec((mem_len, D_BRANCH), lambda b, i: (b, 1))],
        out_specs=pl.BlockSpec((tq, D_BRANCH), lambda b, i: (b * nq + i, 0)),
        out_shape=jax.ShapeDtypeStruct((bsz * seq, D_BRANCH), BF16),
        compiler_params=_params(("parallel", "parallel")),
        name="mem_attn",
    )(y_main, y_main, kv, kv)


def _merge_kernel(*refs):
    y_refs = refs[0:N_BRANCH]
    g_refs = refs[N_BRANCH:2 * N_BRANCH]
    w_refs = refs[2 * N_BRANCH:3 * N_BRANCH]
    out_ref, y0_scr, perm_scr = refs[3 * N_BRANCH:]

    @pl.when(pl.program_id(1) == 0)
    def _():
        rows = y_refs[0].shape[1]
        nslab = D_BRANCH // LANES
        for s in range(S5_T):
            ys = y_refs[0][s].astype(F32)
            for c in range(nslab):
                perm_scr[c, pl.ds(s, rows, stride=S5_T), :] = ys[:, c * LANES:(c + 1) * LANES]
        for c in range(nslab):
            y0_scr[:, c * LANES:(c + 1) * LANES] = perm_scr[c].astype(BF16)

    acc = jax.nn.sigmoid(g_refs[0][...].astype(F32)) * _dot(y0_scr[...], w_refs[0][0])
    for y_ref, g_ref, w_ref in zip(y_refs[1:], g_refs[1:], w_refs[1:]):
        acc = acc + jax.nn.sigmoid(g_ref[...].astype(F32)) * _dot(y_ref[...], w_ref[0])
    out_ref[...] = acc.astype(BF16)


def _merge(branches, y_main, w_branch, layer, tm, tn):
    ntok = y_main.shape[0]
    tm = min(tm, ntok)
    ncol = D_MODEL // tn
    y_specs = ([pl.BlockSpec((S5_T, tm // S5_T, D_BRANCH), lambda i, n: (0, i, 0))]
               + [pl.BlockSpec((tm, D_BRANCH), lambda i, n: (i, 0))] * (N_BRANCH - 1))
    g_specs = [pl.BlockSpec((tm, tn), lambda i, n, b=b: (i, (GATE_COL0 + b * D_MODEL) // tn + n))
               for b in range(N_BRANCH)]
    w_specs = [pl.BlockSpec((pl.Squeezed(), 1, D_BRANCH, tn), lambda i, n, b=b: (layer, b, 0, n))
               for b in range(N_BRANCH)]
    return pl.pallas_call(
        _merge_kernel,
        grid=(ntok // tm, ncol),
        in_specs=y_specs + g_specs + w_specs,
        out_specs=pl.BlockSpec((tm, tn), lambda i, n: (i, n)),
        out_shape=jax.ShapeDtypeStruct((ntok, D_MODEL), BF16),
        scratch_shapes=[pltpu.VMEM((tm, D_BRANCH), BF16), pltpu.VMEM((D_BRANCH // LANES, tm, LANES), F32)],
        compiler_params=_params(("parallel", "arbitrary")),
        name="merge",
    )(*branches, *([y_main] * N_BRANCH), *([w_branch] * N_BRANCH))


def _out_proj_kernel(m_ref, w_ref, g_ref, x_ref, out_ref):
    o = _dot(m_ref[...], w_ref[...])
    o = o * lax.rsqrt(jnp.mean(o * o, axis=-1, keepdims=True) + EPS) * g_ref[...]
    out_ref[...] = x_ref[...] + o


def _out_proj(merged, w_out, layer, g_post, x, tm):
    ntok = x.shape[0]
    tm = min(tm, ntok)
    return pl.pallas_call(
        _out_proj_kernel,
        grid=(ntok // tm,),
        in_specs=[pl.BlockSpec((tm, D_MODEL), lambda i: (i, 0)),
                  pl.BlockSpec((pl.Squeezed(), D_MODEL, D_MODEL), lambda i: (layer, 0, 0)),
                  pl.BlockSpec((1, D_MODEL), lambda i: (0, 0)),
                  pl.BlockSpec((tm, D_MODEL), lambda i: (i, 0))],
        out_specs=pl.BlockSpec((tm, D_MODEL), lambda i: (i, 0)),
        out_shape=jax.ShapeDtypeStruct((ntok, D_MODEL), F32),
        compiler_params=_params(("parallel",)),
        name="out_proj",
    )(merged, w_out, g_post.reshape(1, D_MODEL).astype(F32), x)


def kernel(x, mem, g_pre, w_in, s5_lam_re, s5_lam_im, s5_log_dt, s5_b_re, s5_b_im, s5_c_re, s5_c_im, s5_d, s5_w_glu, s5_b_glu, ml_conv_w, ml_conv_b, ml_b_i, ml_b_f, ml_norm_g, da_lq1, da_lk1, da_lq2, da_lk2, da_subln_g, g_mem, xa_w_kv, w_branch, w_out, g_post):
    bsz, seq, _ = x.shape
    mem_len = mem.shape[1]
    depth = w_in.shape[0]
    ntok = bsz * seq
    cos_t, sin_t = _rope_tables(seq)
    xf = x.reshape(ntok, D_MODEL).astype(F32)
    memf = mem.reshape(bsz * mem_len, D_MODEL).astype(F32)
    gate0 = 7 * D_BRANCH
    expanders = _s5_expanders()
    s5_ops = jax.vmap(_s5_operators)(s5_lam_re, s5_lam_im, s5_log_dt, s5_b_re, s5_b_im, s5_c_re, s5_c_im)
    w_branch_b = w_branch.astype(BF16)
    w_out_b = w_out.astype(BF16)
    w_kv_b = xa_w_kv.astype(BF16)
    w_glu_b = s5_w_glu.astype(BF16)
    gate_lane = jnp.arange(LANES) < 2 * ML_HEADS
    for l in range(depth):
        lambda_init = 0.8 - 0.6 * math.exp(-0.3 * l)
        w_l = w_in[l]
        w_main = jnp.concatenate([w_l[:, :gate0], w_l[:, gate0 + 2 * ML_HEADS:]], axis=1).astype(BF16)
        w_gate = jnp.where(gate_lane[None, :], w_l[:, gate0:gate0 + LANES], 0.0).astype(BF16)
        y_main, s5_in, gates = _in_proj(xf, g_pre[l], w_main, w_gate, *TILES["in_proj"])
        kv = _norm_proj(memf, g_mem[l], w_kv_b, l, *TILES["kv_proj"])

        y_s5 = _s5_branch(s5_in, bsz, seq, s5_ops, l, expanders, s5_d[l].astype(F32), w_glu_b, s5_b_glu[l],
                          TILES["s5_rows"], TILES["s5_glu_rows"])
        y_ml = _mlstm_branch(y_main, gates, bsz, seq, ml_conv_w[l], ml_conv_b[l], ml_b_i[l], ml_b_f[l],
                             ml_norm_g[l])
        y_da = _diff_attn_branch(y_main, bsz, seq, cos_t, sin_t, da_lq1[l], da_lk1[l], da_lq2[l], da_lk2[l],
                                 da_subln_g[l], lambda_init, TILES["attn_q"], TILES["attn_cols"])
        y_xa = _mem_attn_branch(y_main, kv, bsz, seq, mem_len, TILES["mem_q"])

        merged = _merge((y_s5, y_ml, y_da, y_xa), y_main, w_branch_b, l, *TILES["merge"])
        xf = _out_proj(merged, w_out_b, l, g_post[l], xf, TILES["out_rows"])
    return xf.reshape(bsz, seq, D_MODEL).astype(x.dtype)
```
